```python
import math
import jax, jax.numpy as jnp
from jax import lax
import numpy as np

D_MODEL = 1024
BATCH = 4
SEQ = 4096
DEPTH = 1

MEM_LEN = 256
EPS = 1e-6

GDN_HEADS = 8
GDN_DK = 128
GDN_DV = 128
GDN_QK = GDN_HEADS * GDN_DK
GDN_V = GDN_HEADS * GDN_DV
CONV_WIDTH = 4
CHUNK = 64

DIFF_HEADS = 8
DIFF_DH = 64
DIFF_QK = DIFF_HEADS * 2 * DIFF_DH
DIFF_V = DIFF_HEADS * 2 * DIFF_DH
ROPE_DIM = DIFF_DH // 4
ROPE_THETA = 500000.0
Q_BLOCK = 128

X_HEADS = 4
X_DH = 128
X_W = X_HEADS * X_DH

D_FF = -(-8 * D_MODEL // (3 * 256)) * 256

IN_SIZES = [2 * GDN_QK + GDN_V, GDN_V, GDN_HEADS, GDN_HEADS, DIFF_QK, DIFF_QK, DIFF_V]
IN_SPLITS = [int(s) for s in np.cumsum(IN_SIZES)[:-1]]
IN_TOTAL = int(sum(IN_SIZES))

kernel_name = "hybrid_gdn_diffattn_gated_merge"


def rmsnorm(x, g):
    xf = x.astype(jnp.float32)
    out = xf * lax.rsqrt(jnp.mean(xf * xf, axis=-1, keepdims=True) + EPS) * g.astype(jnp.float32)
    return out.astype(x.dtype)


def l2norm(x):
    return x * lax.rsqrt(jnp.sum(x * x, axis=-1, keepdims=True) + EPS)


def partial_rope(x, positions):
    half = ROPE_DIM // 2
    inv_freq = ROPE_THETA ** (-jnp.arange(0, ROPE_DIM, 2, dtype=jnp.float32) / ROPE_DIM)
    ang = positions.astype(jnp.float32)[..., None] * inv_freq
    cos = jnp.cos(ang)[:, :, None, :]
    sin = jnp.sin(ang)[:, :, None, :]
    xf = x.astype(jnp.float32)
    x1, x2, rest = xf[..., :half], xf[..., half:ROPE_DIM], xf[..., ROPE_DIM:]
    out = jnp.concatenate([x1 * cos - x2 * sin, x2 * cos + x1 * sin, rest], axis=-1)
    return out.astype(x.dtype)


def causal_depthwise_conv(x, w):
    S = x.shape[1]
    xp = jnp.pad(x, ((0, 0), (CONV_WIDTH - 1, 0), (0, 0)))
    out = xp[:, 0:S] * w[0]
    for j in range(1, CONV_WIDTH):
        out = out + xp[:, j:j + S] * w[j]
    return out


def gated_delta_rule(q, k, v, g, beta):
    B, S, H, DK = q.shape
    DV = v.shape[-1]
    N = S // CHUNK
    f32 = jnp.float32
    q = l2norm(q.astype(f32)) * (DK ** -0.5)
    k = l2norm(k.astype(f32))
    v = v.astype(f32)

    def chunks(t):
        t = t.reshape((B, N, CHUNK, H) + t.shape[3:])
        return jnp.moveaxis(t, 3, 1)

    q, k, v = chunks(q), chunks(k), chunks(v)
    g = jnp.cumsum(chunks(g.astype(f32)), axis=-1)
    beta = chunks(beta.astype(f32))
    kb = k * beta[..., None]
    vb = v * beta[..., None]

    idx = jnp.arange(CHUNK)
    incl = idx[:, None] >= idx[None, :]
    strict = idx[:, None] > idx[None, :]
    gdiff = g[..., :, None] - g[..., None, :]
    decay = jnp.where(incl, jnp.exp(jnp.where(incl, gdiff, 0.0)), 0.0)
    L = jnp.where(strict, jnp.einsum('bhncd,bhnmd->bhncm', kb, k) * decay, 0.0)
    eye = jnp.eye(CHUNK, dtype=f32)
    T = lax.linalg.triangular_solve(L + eye, jnp.broadcast_to(eye, L.shape),
                                    left_side=True, lower=True)
    u = T @ vb
    w = T @ (kb * jnp.exp(g)[..., None])
    attn = jnp.einsum('bhncd,bhnmd->bhncm', q, k) * decay
    g_last = g[..., -1]
    k_dec = k * jnp.exp(g_last[..., None] - g)[..., None]
    q_dec = q * jnp.exp(g)[..., None]

    def step(state, inp):
        u_i, w_i, q_i, k_i, a_i, gl_i = inp
        v_new = u_i - jnp.einsum('bhcd,bhde->bhce', w_i, state)
        o_i = (jnp.einsum('bhcd,bhde->bhce', q_i, state)
               + jnp.einsum('bhcm,bhme->bhce', a_i, v_new))
        state = (state * jnp.exp(gl_i)[..., None, None]
                 + jnp.einsum('bhcd,bhce->bhde', k_i, v_new))
        return state, o_i

    s0 = jnp.zeros((B, H, DK, DV), f32)
    xs = tuple(jnp.moveaxis(t, 2, 0) for t in (u, w, q_dec, k_dec, attn, g_last))
    _, o = lax.scan(step, s0, xs)
    return jnp.transpose(o, (1, 0, 3, 2, 4)).reshape(B, S, H, DV)


def differential_attention(q, k, v, lam):
    B, S, H, _, DH = q.shape
    nb = S // Q_BLOCK
    f32 = jnp.float32
    qb = jnp.moveaxis(q.reshape(B, nb, Q_BLOCK, H, 2, DH), 1, 0)
    starts = jnp.arange(nb, dtype=jnp.int32) * Q_BLOCK
    kf = k.astype(f32)
    vf = v.astype(f32)
    key_idx = jnp.arange(S, dtype=jnp.int32)

    def one_block(args):
        q_blk, start = args
        s = jnp.einsum('bqhmd,bkhmd->bhmqk', q_blk.astype(f32), kf)
        q_idx = start + jnp.arange(Q_BLOCK, dtype=jnp.int32)
        mask = q_idx[:, None] >= key_idx[None, :]
        p = jax.nn.softmax(jnp.where(mask, s, -jnp.inf), axis=-1)
        a = p[:, :, 0] - lam * p[:, :, 1]
        return jnp.einsum('bhqk,bkhe->bqhe', a, vf)

    o = lax.map(one_block, (qb, starts))
    return jnp.moveaxis(o, 0, 1).reshape(B, S, H, 2 * DH)


def setup_inputs(seed: int = 0) -> dict:
    key = jax.random.key(seed)
    ks = jax.random.split(key, 32)
    f32 = jnp.float32

    def nrm(k, shape, fan_in):
        return jax.random.normal(k, shape, f32) * (fan_in ** -0.5)

    def gain(k, shape):
        return 1.0 + 0.01 * jax.random.normal(k, shape, f32)

    x = jax.random.normal(ks[0], (BATCH, SEQ, D_MODEL), f32)
    mem = jax.random.normal(ks[1], (BATCH, MEM_LEN, D_MODEL), f32)
    offs = jax.random.randint(ks[2], (BATCH, 1), 0, 1024, dtype=jnp.int32)
    positions = (jnp.arange(SEQ, dtype=jnp.int32)[None, :] + offs).astype(jnp.int32)

    dt = jnp.exp(jax.random.uniform(ks[6], (DEPTH, GDN_HEADS), f32,
                                    math.log(0.001), math.log(0.1)))
    dt_bias = dt + jnp.log(-jnp.expm1(-dt))
    a_log = jnp.log(jax.random.uniform(ks[5], (DEPTH, GDN_HEADS), f32, 1.0, 16.0))

    return {
        "x": x,
        "mem": mem,
        "positions": positions,
        "g_mix": gain(ks[3], (DEPTH, D_MODEL)),
        "w_in": nrm(ks[4], (DEPTH, D_MODEL, IN_TOTAL), D_MODEL),
        "conv_w": nrm(ks[7], (DEPTH, CONV_WIDTH, 2 * GDN_QK + GDN_V), CONV_WIDTH),
        "a_log": a_log,
        "dt_bias": dt_bias,
        "gdn_norm_g": gain(ks[8], (DEPTH, GDN_DV)),
        "lambda_q1": 0.1 * jax.random.normal(ks[9], (DEPTH, DIFF_DH), f32),
        "lambda_k1": 0.1 * jax.random.normal(ks[10], (DEPTH, DIFF_DH), f32),
        "lambda_q2": 0.1 * jax.random.normal(ks[11], (DEPTH, DIFF_DH), f32),
        "lambda_k2": 0.1 * jax.random.normal(ks[12], (DEPTH, DIFF_DH), f32),
        "diff_norm_g": gain(ks[13], (DEPTH, 2 * DIFF_DH)),
        "w_branch_gate": nrm(ks[14], (DEPTH, D_MODEL, 2 * D_MODEL), D_MODEL),
        "w_out_a": nrm(ks[15], (DEPTH, GDN_V, D_MODEL), GDN_V),
        "w_out_b": nrm(ks[16], (DEPTH, DIFF_V, D_MODEL), DIFF_V),
        "w_o": nrm(ks[17], (DEPTH, D_MODEL, D_MODEL), D_MODEL),
        "g_cross": gain(ks[18], (DEPTH, D_MODEL)),
        "g_mem": gain(ks[19], (DEPTH, D_MODEL)),
        "w_cq": nrm(ks[20], (DEPTH, D_MODEL, X_W), D_MODEL),
        "w_ckv": nrm(ks[21], (DEPTH, D_MODEL, 2 * X_W), D_MODEL),
        "w_co": nrm(ks[22], (DEPTH, X_W, D_MODEL), X_W),
        "g_ffn": gain(ks[23], (DEPTH, D_MODEL)),
        "w_ffn_in": nrm(ks[24], (DEPTH, D_MODEL, 2 * D_FF), D_MODEL),
        "w_ffn_out": nrm(ks[25], (DEPTH, D_FF, D_MODEL), D_FF),
        "g_final": gain(ks[26], (D_MODEL,)),
    }


def reference(x, mem, positions, g_mix, w_in, conv_w, a_log, dt_bias, gdn_norm_g,
              lambda_q1, lambda_k1, lambda_q2, lambda_k2, diff_norm_g, w_branch_gate,
              w_out_a, w_out_b, w_o, g_cross, g_mem, w_cq, w_ckv, w_co,
              g_ffn, w_ffn_in, w_ffn_out, g_final):
    B, S, _ = x.shape
    M = mem.shape[1]
    f32 = jnp.float32
    for l in range(DEPTH):
        h = rmsnorm(x, g_mix[l])
        proj = h @ w_in[l]
        qkv_a, z_a, a_a, b_a, q_b, k_b, v_b = jnp.split(proj, IN_SPLITS, axis=-1)

        qkv_a = jax.nn.silu(causal_depthwise_conv(qkv_a, conv_w[l]))
        q_a, k_a, v_a = jnp.split(qkv_a, [GDN_QK, 2 * GDN_QK], axis=-1)
        q_a = q_a.reshape(B, S, GDN_HEADS, GDN_DK)
        k_a = k_a.reshape(B, S, GDN_HEADS, GDN_DK)
        v_a = v_a.reshape(B, S, GDN_HEADS, GDN_DV)
        g_a = -jnp.exp(a_log[l].astype(f32)) * jax.nn.softplus(
            a_a.astype(f32) + dt_bias[l].astype(f32))
        beta_a = jax.nn.sigmoid(b_a.astype(f32))
        o_a = gated_delta_rule(q_a, k_a, v_a, g_a, beta_a)
        o_a = rmsnorm(o_a, gdn_norm_g[l]) * jax.nn.silu(
            z_a.reshape(B, S, GDN_HEADS, GDN_DV).astype(f32))
        o_a = o_a.reshape(B, S, GDN_V).astype(x.dtype)

        lam_init = 0.8 - 0.6 * math.exp(-0.3 * l)
        lam = (jnp.exp(jnp.sum(lambda_q1[l].astype(f32) * lambda_k1[l].astype(f32)))
               - jnp.exp(jnp.sum(lambda_q2[l].astype(f32) * lambda_k2[l].astype(f32)))
               + lam_init)
        qd = partial_rope(q_b.reshape(B, S, DIFF_HEADS * 2, DIFF_DH), positions) * (DIFF_DH ** -0.5)
        kd = partial_rope(k_b.reshape(B, S, DIFF_HEADS * 2, DIFF_DH), positions)
        qd = qd.reshape(B, S, DIFF_HEADS, 2, DIFF_DH)
        kd = kd.reshape(B, S, DIFF_HEADS, 2, DIFF_DH)
        vd = v_b.reshape(B, S, DIFF_HEADS, 2 * DIFF_DH)
        o_b = differential_attention(qd, kd, vd, lam)
        o_b = (rmsnorm(o_b, diff_norm_g[l]) * (1.0 - lam_init)).reshape(B, S, DIFF_V).astype(x.dtype)

        gates = jax.nn.sigmoid(h @ w_branch_gate[l])
        gate_a, gate_b = jnp.split(gates, 2, axis=-1)
        merged = gate_a * (o_a @ w_out_a[l]) + gate_b * (o_b @ w_out_b[l])
        x = x + (merged @ w_o[l]).astype(x.dtype)

        hx = rmsnorm(x, g_cross[l])
        hm = rmsnorm(mem, g_mem[l])
        qc = (hx @ w_cq[l]).reshape(B, S, X_HEADS, X_DH) * (X_DH ** -0.5)
        kc, vc = jnp.split(hm @ w_ckv[l], 2, axis=-1)
        kc = kc.reshape(B, M, X_HEADS, X_DH)
        vc = vc.reshape(B, M, X_HEADS, X_DH)
        sc = jnp.einsum('bshd,bmhd->bhsm', qc.astype(f32), kc.astype(f32))
        pc = jax.nn.softmax(sc, axis=-1)
        oc = jnp.einsum('bhsm,bmhd->bshd', pc, vc.astype(f32)).reshape(B, S, X_W).astype(x.dtype)
        x = x + (oc @ w_co[l]).astype(x.dtype)

        hf = rmsnorm(x, g_ffn[l])
        gf, uf = jnp.split(hf @ w_ffn_in[l], 2, axis=-1)
        x = x + ((jax.nn.silu(gf) * uf) @ w_ffn_out[l]).astype(x.dtype)

    return rmsnorm(x, g_final)
```

```python
import functools
import math

import jax
import jax.numpy as jnp
from jax import lax
from jax.experimental import pallas as pl
from jax.experimental.pallas import tpu as pltpu

F32 = jnp.float32
BF16 = jnp.bfloat16

EPS = 1e-6
LANES = 128
GDN_HEADS = 8
GDN_D = 128
CONV_WIDTH = 4
CHUNK = 64
DIFF_HEADS = 8
DIFF_DH = 64
ROPE_DIM = DIFF_DH // 4
ROPE_THETA = 500000.0
X_HEADS = 4
X_DH = 128
NEG_BIG = -1e30

VMEM_LIMIT = 48 * 1024 * 1024

U_QA, U_KA, U_VA, U_Z, U_QB, U_KB, U_VB, U_GA, U_GB, U_END = 0, 8, 16, 24, 32, 40, 48, 56, 64, 72


def _dot(a, b):
    return jnp.dot(a, b, preferred_element_type=F32)


def _dot_nt(a, b):
    return lax.dot_general(a, b, (((1,), (1,)), ((), ())), preferred_element_type=F32)


def _hi_lo(a):
    hi = a.astype(BF16)
    lo = (a - hi.astype(F32)).astype(BF16)
    return hi, lo


def _split3(a):
    hi = a.astype(BF16)
    r = a - hi.astype(F32)
    mid = r.astype(BF16)
    lo = (r - mid.astype(F32)).astype(BF16)
    return hi, mid, lo


def _dot_f32(a, b):
    ah, al = _hi_lo(a)
    bh, bl = _hi_lo(b)
    return _dot(ah, bh) + (_dot(ah, bl) + _dot(al, bh))


def _dot_nt_f32(a, b):
    ah, al = _hi_lo(a)
    bh, bl = _hi_lo(b)
    return _dot_nt(ah, bh) + (_dot_nt(ah, bl) + _dot_nt(al, bh))


def _dot_mask(mask_bf16, b):
    hi, mid, lo = _split3(b)
    return _dot(mask_bf16, hi) + (_dot(mask_bf16, mid) + _dot(mask_bf16, lo))


def _rms(x, g):
    ms = jnp.mean(x * x, axis=-1, keepdims=True)
    return x * lax.rsqrt(ms + EPS) * g


def _sigmoid(x):
    return 1.0 / (1.0 + jnp.exp(-x))


def _silu(x):
    return x * _sigmoid(x)


def _softplus(x):
    return jnp.maximum(x, 0.0) + jnp.log(1.0 + jnp.exp(-jnp.abs(x)))


def _rope_table_kernel(pos_ref, inv_ref, c_ref, s1_ref, s2_ref):
    ang = pos_ref[...] * inv_ref[...]
    cos = jnp.cos(ang)
    sin = jnp.sin(ang)
    lane = lax.broadcasted_iota(jnp.int32, ang.shape, 1) & (DIFF_DH - 1)
    half = ROPE_DIM // 2
    c_ref[...] = jnp.where(lane < ROPE_DIM, cos, 1.0)
    s1_ref[...] = jnp.where(lane < half, -sin, 0.0)
    s2_ref[...] = jnp.where((lane >= half) & (lane < ROPE_DIM), sin, 0.0)


def _rope_tables(positions, tm=1024):
    T = positions.size
    pos = positions.astype(F32).reshape(T, 1)
    half = ROPE_DIM // 2
    inv_freq = ROPE_THETA ** (-jnp.arange(0, ROPE_DIM, 2, dtype=F32) / ROPE_DIM)
    inv = jnp.tile(inv_freq, LANES // half).reshape(1, LANES)
    out = jax.ShapeDtypeStruct((T, LANES), F32)
    return pl.pallas_call(
        _rope_table_kernel,
        grid=(T // tm,),
        in_specs=[pl.BlockSpec((tm, 1), lambda i: (i, 0)),
                  pl.BlockSpec((1, LANES), lambda i: (0, 0))],
        out_specs=[pl.BlockSpec((tm, LANES), lambda i: (i, 0))] * 3,
        out_shape=[out, out, out],
        name="rope_tables",
    )(pos, inv)


def _in_proj_kernel(x_ref, g_ref, w_ref, wab_ref, c_ref, s1_ref, s2_ref, out_ref, ab_ref, h_ref,
                    *, tn):
    j = pl.program_id(1)

    @pl.when(j == 0)
    def _():
        hb = _rms(x_ref[...], g_ref[...]).astype(BF16)
        h_ref[...] = hb
        ab_ref[...] = _dot(hb, wab_ref[...])

    acc = _dot(h_ref[...], w_ref[...])
    upt = tn // LANES
    rope_lo, rope_hi = U_QB // upt, U_VB // upt
    q_hi = U_KB // upt
    gate_lo = U_GA // upt
    is_rope = (j >= rope_lo) & (j < rope_hi)
    is_gate = j >= gate_lo

    @pl.when(is_rope)
    def _():
        reps = tn // LANES
        c = jnp.tile(c_ref[...], (1, reps))
        s1 = jnp.tile(s1_ref[...], (1, reps))
        s2 = jnp.tile(s2_ref[...], (1, reps))
        half = ROPE_DIM // 2
        y = acc * c + pltpu.roll(acc, tn - half, 1) * s1 + pltpu.roll(acc, half, 1) * s2
        scale = jnp.where(j < q_hi, DIFF_DH ** -0.5, 1.0)
        out_ref[...] = (y * scale).astype(out_ref.dtype)

    @pl.when(is_gate)
    def _():
        out_ref[...] = _sigmoid(acc).astype(out_ref.dtype)

    @pl.when(jnp.logical_not(is_rope | is_gate))
    def _():
        out_ref[...] = acc.astype(out_ref.dtype)


def _in_proj(x2d, g_mix, w_wide, w_ab, tables, tm=512, tn=512):
    T, D = x2d.shape
    N = w_wide.shape[1]
    c, s1, s2 = tables
    return pl.pallas_call(
        functools.partial(_in_proj_kernel, tn=tn),
        grid=(T // tm, N // tn),
        in_specs=[pl.BlockSpec((tm, D), lambda i, j: (i, 0)),
                  pl.BlockSpec((1, D), lambda i, j: (0, 0)),
                  pl.BlockSpec((D, tn), lambda i, j: (0, j)),
                  pl.BlockSpec((D, LANES), lambda i, j: (0, 0)),
                  pl.BlockSpec((tm, LANES), lambda i, j: (i, 0)),
                  pl.BlockSpec((tm, LANES), lambda i, j: (i, 0)),
                  pl.BlockSpec((tm, LANES), lambda i, j: (i, 0))],
        out_specs=[pl.BlockSpec((tm, tn), lambda i, j: (i, j)),
                   pl.BlockSpec((tm, LANES), lambda i, j: (i, 0))],
        out_shape=[jax.ShapeDtypeStruct((T, N), BF16),
                   jax.ShapeDtypeStruct((T, LANES), F32)],
        scratch_shapes=[pltpu.VMEM((tm, D), BF16)],
        compiler_params=pltpu.CompilerParams(
            dimension_semantics=("parallel", "arbitrary"), vmem_limit_bytes=VMEM_LIMIT),
        name="in_proj",
    )(x2d, g_mix, w_wide, w_ab, c, s1, s2)


GDN_TS = 4 * CHUNK
SOLVE_BLOCK = 16
CHUNK_SHIFT = CHUNK.bit_length() - 1
SOLVE_SHIFT = SOLVE_BLOCK.bit_length() - 1
assert (1 << CHUNK_SHIFT) == CHUNK and (1 << SOLVE_SHIFT) == SOLVE_BLOCK and CHUNK // SOLVE_BLOCK == 4


def _gdn_prep_kernel(q_ref, k_ref, v_ref, ab_ref, cwq_ref, cwk_ref, cwv_ref, alog_ref, dtb_ref,
                     u_ref, w_ref, qd_ref, kd_ref, attn_ref, egl_ref, cbuf_ref):
    TS = GDN_TS
    h = pl.program_id(1)
    s = pl.program_id(2)

    @pl.when(s == 0)
    def _():
        cbuf_ref[:, 0:8, :] = jnp.zeros((3, 8, GDN_D), F32)

    def conv_silu(idx, x_ref, cw_ref):
        cbuf_ref[idx, 8:8 + TS, :] = x_ref[...].astype(F32)
        base = 8 - (CONV_WIDTH - 1)
        acc = cbuf_ref[idx, base:base + TS, :] * cw_ref[0:1, :]
        for t in range(1, CONV_WIDTH):
            acc = acc + cbuf_ref[idx, base + t:base + t + TS, :] * cw_ref[t:t + 1, :]
        cbuf_ref[idx, 0:8, :] = cbuf_ref[idx, TS:TS + 8, :]
        return _silu(acc)

    q = conv_silu(0, q_ref, cwq_ref)
    k = conv_silu(1, k_ref, cwk_ref)
    v = conv_silu(2, v_ref, cwv_ref)
    q = q * lax.rsqrt(jnp.sum(q * q, axis=-1, keepdims=True) + EPS) * (GDN_D ** -0.5)
    k = k * lax.rsqrt(jnp.sum(k * k, axis=-1, keepdims=True) + EPS)

    ab = ab_ref[...]
    lane = lax.broadcasted_iota(jnp.int32, ab.shape, 1)
    g_all = -jnp.exp(alog_ref[...]) * _softplus(ab + dtb_ref[...])
    b_all = _sigmoid(ab)
    g_col = jnp.sum(jnp.where(lane == h, g_all, 0.0), axis=-1, keepdims=True)
    beta = jnp.sum(jnp.where(lane == h + GDN_HEADS, b_all, 0.0), axis=-1, keepdims=True)

    ri = lax.broadcasted_iota(jnp.int32, (TS, TS), 0)
    ci = lax.broadcasted_iota(jnp.int32, (TS, TS), 1)
    same = (ri >> CHUNK_SHIFT) == (ci >> CHUNK_SHIFT)
    incl = same & (ri >= ci)
    strict = same & (ri > ci)

    g_b = jnp.broadcast_to(g_col, (TS, LANES))
    gc_col = _dot_mask(jnp.where(incl, 1.0, 0.0).astype(BF16), g_b)[:, 0:1]
    gl_col = _dot_mask(jnp.where(same, 1.0, 0.0).astype(BF16), g_b)[:, 0:1]
    x_t = jnp.where(same & (ri <= ci), g_col, 0.0)
    gc_row = _dot_mask(jnp.ones((8, TS), BF16), x_t)[0:1, :]
    decay = jnp.where(incl, jnp.exp(jnp.where(incl, gc_col - gc_row, 0.0)), 0.0)

    kb = k * beta
    vb = v * beta
    lmat = jnp.where(strict, _dot_nt_f32(kb, k) * decay, 0.0)

    blk = (ri >> SOLVE_SHIFT) == (ci >> SOLVE_SHIFT)
    eye = jnp.where(ri == ci, 1.0, 0.0)
    ld = jnp.where(blk, lmat, 0.0)
    lo = lmat - ld
    p = _dot_f32(ld, ld)
    td = eye - ld
    td = td + _dot_f32(td, p)
    p = _dot_f32(p, p)
    td = td + _dot_f32(td, p)
    p = _dot_f32(p, p)
    td = td + _dot_f32(td, p)
    n = _dot_f32(td, lo)
    n2 = _dot_f32(n, n)
    m1 = td + _dot_f32(n2, td)
    tinv = m1 - _dot_f32(n, m1)

    eg = jnp.exp(gc_col)
    rhs = jnp.concatenate([vb, kb * eg], axis=1)
    uw = _dot_f32(tinv, rhs)
    u_ref[...] = uw[:, :GDN_D].astype(u_ref.dtype)
    w_ref[...] = uw[:, GDN_D:].astype(w_ref.dtype)
    qd_ref[...] = (q * eg).astype(qd_ref.dtype)
    kd_ref[...] = (k * jnp.exp(gl_col - gc_col)).astype(kd_ref.dtype)
    attn = jnp.where(incl, _dot_nt(q.astype(BF16), k.astype(BF16)) * decay, 0.0)
    attn_ref[...] = attn.astype(attn_ref.dtype)

    r8 = lax.broadcasted_iota(jnp.int32, (8, TS), 0)
    c8 = lax.broadcasted_iota(jnp.int32, (8, TS), 1)
    sel = jnp.where(c8 == r8 * CHUNK + (CHUNK - 1), 1.0, 0.0).astype(BF16)
    egl_ref[...] = jnp.exp(_dot_mask(sel, jnp.broadcast_to(gc_col, (TS, LANES))))


def _gdn_prep(wide, ab, conv_w, a_log, dt_bias, B, S):
    T = B * S
    TS = GDN_TS
    ns = S // TS
    H = GDN_HEADS
    row = lambda b, h, s: b * ns + s
    tok = lambda unit: pl.BlockSpec((TS, LANES), lambda b, h, s: (row(b, h, s), unit + h))
    cw = lambda unit: pl.BlockSpec((CONV_WIDTH, LANES), lambda b, h, s: (0, unit + h))
    vec = pl.BlockSpec((1, LANES), lambda b, h, s: (0, 0))
    head_out = pl.BlockSpec((TS, LANES), lambda b, h, s: (row(b, h, s), h))
    act = jax.ShapeDtypeStruct((T, H * GDN_D), BF16)
    return pl.pallas_call(
        _gdn_prep_kernel,
        grid=(B, H, ns),
        in_specs=[tok(U_QA), tok(U_KA), tok(U_VA),
                  pl.BlockSpec((TS, LANES), lambda b, h, s: (row(b, h, s), 0)),
                  cw(U_QA), cw(U_KA), cw(U_VA), vec, vec],
        out_specs=[head_out, head_out, head_out, head_out,
                   pl.BlockSpec((TS, TS), lambda b, h, s: (row(b, h, s), h)),
                   pl.BlockSpec((8, LANES), lambda b, h, s: (row(b, h, s), h))],
        out_shape=[act, act, act, act,
                   jax.ShapeDtypeStruct((T, H * TS), BF16),
                   jax.ShapeDtypeStruct((B * ns * 8, H * LANES), F32)],
        scratch_shapes=[pltpu.VMEM((3, TS + 8, GDN_D), F32)],
        compiler_params=pltpu.CompilerParams(
            dimension_semantics=("parallel", "parallel", "arbitrary"), vmem_limit_bytes=VMEM_LIMIT),
        name="gdn_prep",
    )(wide, wide, wide, ab, conv_w, conv_w, conv_w, a_log, dt_bias)


def _gdn_scan_kernel(u_ref, w_ref, qd_ref, kd_ref, attn_ref, egl_ref, z_ref, gn_ref, o_ref,
                     state_ref, vnew_ref, oq_ref):
    TS = GDN_TS
    D = GDN_D
    s = pl.program_id(1)

    @pl.when(s == 0)
    def _():
        state_ref[...] = jnp.zeros(state_ref.shape, F32)

    for c in range(TS // CHUNK):
        r = slice(c * CHUNK, (c + 1) * CHUNK)
        for h in range(GDN_HEADS):
            cs = slice(h * D, (h + 1) * D)
            st = state_ref[h]
            wq = jnp.concatenate([w_ref[r, cs], qd_ref[r, cs]], axis=0)
            ws = _dot(wq, st.astype(BF16))
            v_new = (u_ref[r, cs].astype(F32) - ws[:CHUNK]).astype(BF16)
            vnew_ref[h, r, :] = v_new
            oq_ref[r, cs] = ws[CHUNK:]
            ktv = lax.dot_general(kd_ref[r, cs], v_new, (((0,), (0,)), ((), ())),
                                  preferred_element_type=F32)
            state_ref[h] = st * egl_ref[c:c + 1, cs] + ktv

    for h in range(GDN_HEADS):
        cs = slice(h * D, (h + 1) * D)
        o = oq_ref[:, cs] + _dot(attn_ref[:, h * TS:(h + 1) * TS], vnew_ref[h])
        o = _rms(o, gn_ref[...]) * _silu(z_ref[:, cs].astype(F32))
        o_ref[:, cs] = o.astype(o_ref.dtype)


def _gdn_scan(u, w, qd, kd, attn, egl, wide, gdn_norm_g, B, S):
    T = B * S
    TS = GDN_TS
    ns = S // TS
    HD = GDN_HEADS * GDN_D
    tokw = pl.BlockSpec((TS, HD), lambda b, s: (b * ns + s, 0))
    return pl.pallas_call(
        _gdn_scan_kernel,
        grid=(B, ns),
        in_specs=[tokw, tokw, tokw, tokw,
                  pl.BlockSpec((TS, GDN_HEADS * TS), lambda b, s: (b * ns + s, 0)),
                  pl.BlockSpec((8, HD), lambda b, s: (b * ns + s, 0)),
                  pl.BlockSpec((TS, HD), lambda b, s: (b * ns + s, U_Z * LANES // HD)),
                  pl.BlockSpec((1, GDN_D), lambda b, s: (0, 0))],
        out_specs=tokw,
        out_shape=jax.ShapeDtypeStruct((T, HD), BF16),
        scratch_shapes=[pltpu.VMEM((GDN_HEADS, GDN_D, GDN_D), F32),
                        pltpu.VMEM((GDN_HEADS, TS, GDN_D), BF16),
                        pltpu.VMEM((TS, HD), F32)],
        compiler_params=pltpu.CompilerParams(
            dimension_semantics=("parallel", "arbitrary"), vmem_limit_bytes=VMEM_LIMIT),
        name="gdn_scan",
    )(u, w, qd, kd, attn, egl, wide, gdn_norm_g)


def _diff_attn_kernel(q_ref, k_ref, v_ref, lq1_ref, lk1_ref, lq2_ref, lk2_ref, gn_ref, o_ref,
                      m_ref, l_ref, acc_ref, *, tq, tk, lam_init):
    i = pl.program_id(2)
    j = pl.program_id(3)

    @pl.when(j == 0)
    def _():
        m_ref[...] = jnp.full(m_ref.shape, NEG_BIG, F32)
        l_ref[...] = jnp.zeros(l_ref.shape, F32)
        acc_ref[...] = jnp.zeros(acc_ref.shape, F32)

    @pl.when(j <= i)
    def _():
        q = q_ref[...]
        k = k_ref[...]
        v = v_ref[...]
        lane = lax.broadcasted_iota(jnp.int32, q.shape, 1)
        zero = jnp.zeros_like(q)
        qs = (jnp.where(lane < DIFF_DH, q, zero), jnp.where(lane >= DIFF_DH, q, zero))
        qi = i * tq + lax.broadcasted_iota(jnp.int32, (tq, tk), 0)
        ki = j * tk + lax.broadcasted_iota(jnp.int32, (tq, tk), 1)
        mask = qi >= ki
        for mp in range(2):
            sc = jnp.where(mask, _dot_nt(qs[mp], k), NEG_BIG)
            m_prev = m_ref[mp]
            m_new = jnp.maximum(m_prev, jnp.max(sc, axis=-1, keepdims=True))
            alpha = jnp.exp(m_prev - m_new)
            p = jnp.exp(sc - m_new)
            l_ref[mp] = alpha * l_ref[mp] + jnp.sum(p, axis=-1, keepdims=True)
            acc_ref[mp] = alpha * acc_ref[mp] + _dot(p.astype(BF16), v)
            m_ref[mp] = m_new

    @pl.when(j == i)
    def _():
        lam = (jnp.exp(jnp.sum(lq1_ref[...] * lk1_ref[...], axis=-1, keepdims=True))
               - jnp.exp(jnp.sum(lq2_ref[...] * lk2_ref[...], axis=-1, keepdims=True))
               + lam_init)
        o = acc_ref[0] / l_ref[0] - lam * (acc_ref[1] / l_ref[1])
        o = _rms(o, gn_ref[...]) * (1.0 - lam_init)
        o_ref[...] = o.astype(o_ref.dtype)


def _diff_attn(wide, lq1, lk1, lq2, lk2, diff_norm_g, lam_init, B, S, tq=512, tk=512):
    T = B * S
    nq, nk = S // tq, S // tk
    assert tq == tk
    lam_spec = pl.BlockSpec((1, DIFF_DH), lambda b, h, i, j: (0, 0))
    return pl.pallas_call(
        functools.partial(_diff_attn_kernel, tq=tq, tk=tk, lam_init=lam_init),
        grid=(B, DIFF_HEADS, nq, nk),
        in_specs=[pl.BlockSpec((tq, LANES), lambda b, h, i, j: (b * nq + i, U_QB + h)),
                  pl.BlockSpec((tk, LANES), lambda b, h, i, j: (b * nk + jnp.minimum(j, i), U_KB + h)),
                  pl.BlockSpec((tk, LANES), lambda b, h, i, j: (b * nk + jnp.minimum(j, i), U_VB + h)),
                  lam_spec, lam_spec, lam_spec, lam_spec,
                  pl.BlockSpec((1, 2 * DIFF_DH), lambda b, h, i, j: (0, 0))],
        out_specs=pl.BlockSpec((tq, LANES), lambda b, h, i, j: (b * nq + i, h)),
        out_shape=jax.ShapeDtypeStruct((T, DIFF_HEADS * 2 * DIFF_DH), BF16),
        scratch_shapes=[pltpu.VMEM((2, tq, 1), F32),
                        pltpu.VMEM((2, tq, 1), F32),
                        pltpu.VMEM((2, tq, 2 * DIFF_DH), F32)],
        compiler_params=pltpu.CompilerParams(
            dimension_semantics=("parallel", "parallel", "parallel", "arbitrary"),
            vmem_limit_bytes=VMEM_LIMIT),
        name="diff_attn",
    )(wide, wide, wide, lq1, lk1, lq2, lk2, diff_norm_g)


def _merge_kernel(oa_ref, ob_ref, ga_ref, gb_ref, x_ref, woa_ref, wob_ref, wo_ref, x1_ref):
    ya = _dot(oa_ref[...], woa_ref[...])
    yb = _dot(ob_ref[...], wob_ref[...])
    merged = ga_ref[...].astype(F32) * ya + gb_ref[...].astype(F32) * yb
    x1_ref[...] = x_ref[...] + _dot(merged.astype(BF16), wo_ref[...])


def _merge(o_a, o_b, wide, x2d, w_out_a, w_out_b, w_o, tm=512):
    T, D = x2d.shape
    tokd = pl.BlockSpec((tm, D), lambda i: (i, 0))
    wspec = pl.BlockSpec((D, D), lambda i: (0, 0))
    return pl.pallas_call(
        _merge_kernel,
        grid=(T // tm,),
        in_specs=[tokd, tokd,
                  pl.BlockSpec((tm, D), lambda i: (i, U_GA * LANES // D)),
                  pl.BlockSpec((tm, D), lambda i: (i, U_GB * LANES // D)),
                  tokd, wspec, wspec, wspec],
        out_specs=tokd,
        out_shape=jax.ShapeDtypeStruct((T, D), F32),
        compiler_params=pltpu.CompilerParams(
            dimension_semantics=("parallel",), vmem_limit_bytes=VMEM_LIMIT),
        name="merge",
    )(o_a, o_b, wide, wide, x2d, w_out_a, w_out_b, w_o)


def _mem_kv_kernel(m_ref, g_ref, w_ref, o_ref):
    o_ref[...] = _dot(_rms(m_ref[...], g_ref[...]).astype(BF16), w_ref[...]).astype(o_ref.dtype)


def _mem_kv(mem2d, g_mem, w_ckv):
    R, D = mem2d.shape
    N = w_ckv.shape[1]
    return pl.pallas_call(
        _mem_kv_kernel,
        grid=(1,),
        in_specs=[pl.BlockSpec((R, D), lambda i: (0, 0)),
                  pl.BlockSpec((1, D), lambda i: (0, 0)),
                  pl.BlockSpec((D, N), lambda i: (0, 0))],
        out_specs=pl.BlockSpec((R, N), lambda i: (0, 0)),
        out_shape=jax.ShapeDtypeStruct((R, N), BF16),
        compiler_params=pltpu.CompilerParams(vmem_limit_bytes=VMEM_LIMIT),
        name="mem_kv",
    )(mem2d, g_mem, w_ckv)


def _cross_kernel(x_ref, g_ref, wq_ref, kv_ref, wo_ref, o_ref):
    x = x_ref[...]
    hx = _rms(x, g_ref[...]).astype(BF16)
    qc = (_dot(hx, wq_ref[...]) * (X_DH ** -0.5)).astype(BF16)
    xw = X_HEADS * X_DH
    outs = []
    for hh in range(X_HEADS):
        cs = slice(hh * X_DH, (hh + 1) * X_DH)
        sc = _dot_nt(qc[:, cs], kv_ref[:, cs])
        p = jnp.exp(sc - jnp.max(sc, axis=-1, keepdims=True))
        p = p / jnp.sum(p, axis=-1, keepdims=True)
        outs.append(_dot(p.astype(BF16), kv_ref[:, xw + hh * X_DH:xw + (hh + 1) * X_DH]))
    oc = jnp.concatenate(outs, axis=1).astype(BF16)
    o_ref[...] = x + _dot(oc, wo_ref[...])


def _cross(x1, g_cross, w_cq, kv, w_co, S, M, tm=512):
    T, D = x1.shape
    xw = X_HEADS * X_DH
    per_b = S // tm
    tokd = pl.BlockSpec((tm, D), lambda i: (i, 0))
    return pl.pallas_call(
        _cross_kernel,
        grid=(T // tm,),
        in_specs=[tokd,
                  pl.BlockSpec((1, D), lambda i: (0, 0)),
                  pl.BlockSpec((D, xw), lambda i: (0, 0)),
                  pl.BlockSpec((M, 2 * xw), lambda i: (i // per_b, 0)),
                  pl.BlockSpec((xw, D), lambda i: (0, 0))],
        out_specs=tokd,
        out_shape=jax.ShapeDtypeStruct((T, D), F32),
        compiler_params=pltpu.CompilerParams(
            dimension_semantics=("parallel",), vmem_limit_bytes=VMEM_LIMIT),
        name="cross",
    )(x1, g_cross, w_cq, kv, w_co)


def _ffn_kernel(x_ref, g_ref, wg_ref, wu_ref, wo_ref, gf_ref, o_ref, h_ref, acc_ref):
    f = pl.program_id(1)

    @pl.when(f == 0)
    def _():
        h_ref[...] = _rms(x_ref[...], g_ref[...]).astype(BF16)
        acc_ref[...] = jnp.zeros(acc_ref.shape, F32)

    hb = h_ref[...]
    act = _silu(_dot(hb, wg_ref[...])) * _dot(hb, wu_ref[...])
    acc_ref[...] += _dot(act.astype(BF16), wo_ref[...])

    @pl.when(f == pl.num_programs(1) - 1)
    def _():
        o_ref[...] = _rms(x_ref[...] + acc_ref[...], gf_ref[...])


def _ffn(x2, g_ffn, w_ffn_in, w_ffn_out, g_final, tm=512, tf=256):
    T, D = x2.shape
    F = w_ffn_out.shape[0]
    nf = F // tf
    tokd = pl.BlockSpec((tm, D), lambda i, f: (i, 0))
    vec = pl.BlockSpec((1, D), lambda i, f: (0, 0))
    return pl.pallas_call(
        _ffn_kernel,
        grid=(T // tm, nf),
        in_specs=[tokd, vec,
                  pl.BlockSpec((D, tf), lambda i, f: (0, f)),
                  pl.BlockSpec((D, tf), lambda i, f: (0, nf + f)),
                  pl.BlockSpec((tf, D), lambda i, f: (f, 0)),
                  vec],
        out_specs=tokd,
        out_shape=jax.ShapeDtypeStruct((T, D), F32),
        scratch_shapes=[pltpu.VMEM((tm, D), BF16), pltpu.VMEM((tm, D), F32)],
        compiler_params=pltpu.CompilerParams(
            dimension_semantics=("parallel", "arbitrary"), vmem_limit_bytes=VMEM_LIMIT),
        name="ffn",
    )(x2, g_ffn, w_ffn_in, w_ffn_in, w_ffn_out, g_final)


def _pad_lanes(v):
    v = v.reshape(1, -1).astype(F32)
    return jnp.pad(v, ((0, 0), (0, LANES - v.shape[1])))


def kernel(x, mem, positions, g_mix, w_in, conv_w, a_log, dt_bias, gdn_norm_g, lambda_q1, lambda_k1, lambda_q2, lambda_k2, diff_norm_g, w_branch_gate, w_out_a, w_out_b, w_o, g_cross, g_mem, w_cq, w_ckv, w_co, g_ffn, w_ffn_in, w_ffn_out, g_final):
    B, S, D = x.shape
    M = mem.shape[1]
    depth = w_in.shape[0]
    assert depth == 1, "the final rmsnorm is fused into the (single) layer's ffn call"
    qkvz = 4 * GDN_HEADS * GDN_D
    xs = x.reshape(B * S, D)
    tables = _rope_tables(positions)
    kv = None
    for l in range(depth):
        w_wide = jnp.concatenate(
            [w_in[l][:, :qkvz], w_in[l][:, qkvz + 2 * GDN_HEADS:], w_branch_gate[l]], axis=1).astype(BF16)
        w_ab = jnp.pad(w_in[l][:, qkvz:qkvz + 2 * GDN_HEADS],
                       ((0, 0), (0, LANES - 2 * GDN_HEADS))).astype(BF16)
        wide, ab = _in_proj(xs, g_mix[l].reshape(1, D), w_wide, w_ab, tables)

        u, w, qd, kd, attn, egl = _gdn_prep(wide, ab, conv_w[l], _pad_lanes(a_log[l]),
                                            _pad_lanes(dt_bias[l]), B, S)
        o_a = _gdn_scan(u, w, qd, kd, attn, egl, wide, gdn_norm_g[l].reshape(1, GDN_D), B, S)

        lam_init = 0.8 - 0.6 * math.exp(-0.3 * l)
        o_b = _diff_attn(wide, lambda_q1[l].reshape(1, -1), lambda_k1[l].reshape(1, -1),
                         lambda_q2[l].reshape(1, -1), lambda_k2[l].reshape(1, -1),
                         diff_norm_g[l].reshape(1, -1), lam_init, B, S)

        xs = _merge(o_a, o_b, wide, xs, w_out_a[l].astype(BF16), w_out_b[l].astype(BF16),
                    w_o[l].astype(BF16))

        kv = _mem_kv(mem.reshape(B * M, D), g_mem[l].reshape(1, D), w_ckv[l].astype(BF16))
        xs = _cross(xs, g_cross[l].reshape(1, D), w_cq[l].astype(BF16), kv, w_co[l].astype(BF16), S, M)

        xs = _ffn(xs, g_ffn[l].reshape(1, D), w_ffn_in[l].astype(BF16), w_ffn_out[l].astype(BF16),
                  g_final.reshape(1, D))
    return xs.reshape(B, S, D)
```

```python
import functools
import math

import jax
import jax.numpy as jnp
from jax import lax
from jax.experimental import pallas as pl
from jax.experimental.pallas import tpu as pltpu

F32 = jnp.float32
BF16 = jnp.bfloat16

EPS = 1e-6
LANES = 128
GDN_HEADS = 8
GDN_D = 128
CONV_WIDTH = 4
CHUNK = 64
DIFF_HEADS = 8
DIFF_DH = 64
ROPE_DIM = DIFF_DH // 4
ROPE_THETA = 500000.0
X_HEADS = 4
X_DH = 128
NEG_BIG = -1e30
LOG2E = math.log2(math.e)

VMEM_LIMIT = 48 * 1024 * 1024

U_QA, U_KA, U_VA, U_Z, U_QB, U_KB, U_VB, U_GA, U_GB, U_END = 0, 8, 16, 24, 32, 40, 48, 56, 64, 72


def _dot(a, b):
    return jnp.dot(a, b, preferred_element_type=F32)


def _dot_nt(a, b):
    return lax.dot_general(a, b, (((1,), (1,)), ((), ())), preferred_element_type=F32)


def _mm(a, b):
    return _dot(a.astype(BF16), b.astype(BF16))


def _split3(a):
    hi = a.astype(BF16)
    r = a - hi.astype(F32)
    mid = r.astype(BF16)
    lo = (r - mid.astype(F32)).astype(BF16)
    return hi, mid, lo


def _dot_mask(mask_bf16, b):
    hi, mid, lo = _split3(b)
    return _dot(mask_bf16, hi) + (_dot(mask_bf16, mid) + _dot(mask_bf16, lo))


def _dot_rmask(b, mask_bf16):
    hi, mid, lo = _split3(b)
    return _dot(hi, mask_bf16) + (_dot(mid, mask_bf16) + _dot(lo, mask_bf16))


def _rms(x, g):
    ms = jnp.mean(x * x, axis=-1, keepdims=True)
    return x * lax.rsqrt(ms + EPS) * g


def _sigmoid(x):
    return 1.0 / (1.0 + jnp.exp(-x))


def _silu(x):
    return x * _sigmoid(x)


def _softplus(x):
    return jnp.maximum(x, 0.0) + jnp.log(1.0 + jnp.exp(-jnp.abs(x)))


def _rope_table_kernel(pos_ref, inv_ref, c_ref, s1_ref, s2_ref):
    ang = pos_ref[...] * inv_ref[...]
    cos = jnp.cos(ang)
    sin = jnp.sin(ang)
    lane = lax.broadcasted_iota(jnp.int32, ang.shape, 1) & (DIFF_DH - 1)
    half = ROPE_DIM // 2
    c_ref[...] = jnp.where(lane < ROPE_DIM, cos, 1.0)
    s1_ref[...] = jnp.where(lane < half, -sin, 0.0)
    s2_ref[...] = jnp.where((lane >= half) & (lane < ROPE_DIM), sin, 0.0)


def _rope_tables(positions, tm=1024):
    T = positions.size
    pos = positions.astype(F32).reshape(T, 1)
    half = ROPE_DIM // 2
    inv_freq = ROPE_THETA ** (-jnp.arange(0, ROPE_DIM, 2, dtype=F32) / ROPE_DIM)
    inv = jnp.tile(inv_freq, LANES // half).reshape(1, LANES)
    out = jax.ShapeDtypeStruct((T, LANES), F32)
    return pl.pallas_call(
        _rope_table_kernel,
        grid=(T // tm,),
        in_specs=[pl.BlockSpec((tm, 1), lambda i: (i, 0)),
                  pl.BlockSpec((1, LANES), lambda i: (0, 0))],
        out_specs=[pl.BlockSpec((tm, LANES), lambda i: (i, 0))] * 3,
        out_shape=[out, out, out],
        name="rope_tables",
    )(pos, inv)


def _in_proj_kernel(x_ref, g_ref, w_ref, wab_ref, c_ref, s1_ref, s2_ref, out_ref, ab_ref, h_ref,
                    *, tn):
    j = pl.program_id(1)

    @pl.when(j == 0)
    def _():
        hb = _rms(x_ref[...], g_ref[...]).astype(BF16)
        h_ref[...] = hb
        ab_ref[...] = _dot(hb, wab_ref[...])

    acc = _dot(h_ref[...], w_ref[...])
    upt = tn // LANES
    rope_lo, rope_hi = U_QB // upt, U_VB // upt
    q_hi = U_KB // upt
    gate_lo = U_GA // upt
    is_rope = (j >= rope_lo) & (j < rope_hi)
    is_gate = j >= gate_lo

    @pl.when(is_rope)
    def _():
        c, s1, s2 = c_ref[...], s1_ref[...], s2_ref[...]
        half = ROPE_DIM // 2
        scale = jnp.where(j < q_hi, DIFF_DH ** -0.5 * LOG2E, 1.0)
        for cc in range(upt):
            cs = slice(cc * LANES, (cc + 1) * LANES)
            a = acc[:, cs]
            y = a * c + pltpu.roll(a, LANES - half, 1) * s1 + pltpu.roll(a, half, 1) * s2
            out_ref[:, cs] = (y * scale).astype(out_ref.dtype)

    @pl.when(is_gate)
    def _():
        out_ref[...] = _sigmoid(acc).astype(out_ref.dtype)

    @pl.when(jnp.logical_not(is_rope | is_gate))
    def _():
        out_ref[...] = acc.astype(out_ref.dtype)


def _in_proj(x2d, g_mix, w_wide, w_ab, tables, tm=1024, tn=512):
    T, D = x2d.shape
    N = w_wide.shape[1]
    tm = min(tm, T)
    c, s1, s2 = tables
    return pl.pallas_call(
        functools.partial(_in_proj_kernel, tn=tn),
        grid=(T // tm, N // tn),
        in_specs=[pl.BlockSpec((tm, D), lambda i, j: (i, 0)),
                  pl.BlockSpec((1, D), lambda i, j: (0, 0)),
                  pl.BlockSpec((D, tn), lambda i, j: (0, j)),
                  pl.BlockSpec((D, LANES), lambda i, j: (0, 0)),
                  pl.BlockSpec((tm, LANES), lambda i, j: (i, 0)),
                  pl.BlockSpec((tm, LANES), lambda i, j: (i, 0)),
                  pl.BlockSpec((tm, LANES), lambda i, j: (i, 0))],
        out_specs=[pl.BlockSpec((tm, tn), lambda i, j: (i, j)),
                   pl.BlockSpec((tm, LANES), lambda i, j: (i, 0))],
        out_shape=[jax.ShapeDtypeStruct((T, N), BF16),
                   jax.ShapeDtypeStruct((T, LANES), F32)],
        scratch_shapes=[pltpu.VMEM((tm, D), BF16)],
        compiler_params=pltpu.CompilerParams(
            dimension_semantics=("parallel", "arbitrary"), vmem_limit_bytes=VMEM_LIMIT),
        name="in_proj",
    )(x2d, g_mix, w_wide, w_ab, c, s1, s2)


GDN_TS = 4 * CHUNK
SOLVE_BLOCK = 16
CHUNK_SHIFT = CHUNK.bit_length() - 1
SOLVE_SHIFT = SOLVE_BLOCK.bit_length() - 1
assert (1 << CHUNK_SHIFT) == CHUNK and (1 << SOLVE_SHIFT) == SOLVE_BLOCK and CHUNK // SOLVE_BLOCK == 4


def _gdn_prep_kernel(qkv_ref, ab_ref, cw_ref, alog_ref, dtb_ref,
                     u_ref, w_ref, qd_ref, kd_ref, attn_ref, egl_ref, cbuf_ref):
    TS, H, D = GDN_TS, GDN_HEADS, GDN_D
    s = pl.program_id(1)

    @pl.when(s == 0)
    def _():
        cbuf_ref[0:8, :] = jnp.zeros((8, 3 * H * D), F32)

    cbuf_ref[8:8 + TS, :] = qkv_ref[...].astype(F32)
    base = 8 - (CONV_WIDTH - 1)

    def conv_silu(c0):
        cs = slice(c0, c0 + D)
        acc = cbuf_ref[base:base + TS, cs] * cw_ref[0:1, cs]
        for t in range(1, CONV_WIDTH):
            acc = acc + cbuf_ref[base + t:base + t + TS, cs] * cw_ref[t:t + 1, cs]
        return _silu(acc)

    ri = lax.broadcasted_iota(jnp.int32, (TS, TS), 0)
    ci = lax.broadcasted_iota(jnp.int32, (TS, TS), 1)
    same = (ri >> CHUNK_SHIFT) == (ci >> CHUNK_SHIFT)
    incl = same & (ri >= ci)
    strict = same & (ri > ci)
    blk = (ri >> SOLVE_SHIFT) == (ci >> SOLVE_SHIFT)
    eye = jnp.where(ri == ci, 1.0, 0.0)

    ab = ab_ref[...]
    g_all = -jnp.exp(alog_ref[...]) * _softplus(ab + dtb_ref[...])
    b_all = _sigmoid(ab)
    gc_all = _dot_mask(jnp.where(incl, 1.0, 0.0).astype(BF16), g_all)
    gl_all = _dot_mask(jnp.where(same, 1.0, 0.0).astype(BF16), g_all)
    gc_t = _dot_rmask(g_all.T, jnp.where(same & (ri <= ci), 1.0, 0.0).astype(BF16))

    r8 = lax.broadcasted_iota(jnp.int32, (8, TS), 0)
    c8 = lax.broadcasted_iota(jnp.int32, (8, TS), 1)
    sel = jnp.where(c8 == r8 * CHUNK, 1.0, 0.0).astype(BF16)
    gl8 = _dot_mask(sel, gl_all)

    ld, lo, x, rhs = [], [], [], []
    for h in range(H):
        cs = slice(h * D, (h + 1) * D)
        q = conv_silu(h * D)
        k = conv_silu((H + h) * D)
        v = conv_silu((2 * H + h) * D)
        q = q * lax.rsqrt(jnp.sum(q * q, axis=-1, keepdims=True) + EPS) * (D ** -0.5)
        k = k * lax.rsqrt(jnp.sum(k * k, axis=-1, keepdims=True) + EPS)
        gc_col = gc_all[:, h:h + 1]
        gl_col = gl_all[:, h:h + 1]
        beta = b_all[:, H + h:H + h + 1]
        decay = jnp.exp(jnp.minimum(gc_col - gc_t[h:h + 1, :], 0.0))
        eg = jnp.exp(gc_col)
        kb = k * beta
        kbf = k.astype(BF16)
        lmat = jnp.where(strict, _dot_nt(kb.astype(BF16), kbf) * decay, 0.0)
        attn = jnp.where(incl, _dot_nt(q.astype(BF16), kbf) * decay, 0.0)
        attn_ref[:, h * TS:(h + 1) * TS] = attn.astype(attn_ref.dtype)
        qd_ref[:, cs] = (q * eg).astype(qd_ref.dtype)
        kd_ref[:, cs] = (k * jnp.exp(gl_col - gc_col)).astype(kd_ref.dtype)
        egl_ref[:, cs] = jnp.exp(jnp.broadcast_to(gl8[:, h:h + 1], (8, D)))
        ldh = jnp.where(blk, lmat, 0.0)
        ld.append(ldh.astype(BF16))
        lo.append((lmat - ldh).astype(BF16))
        x.append(eye - ldh)
        rhs.append(jnp.concatenate([v * beta, kb * eg], axis=1).astype(BF16))

    cbuf_ref[0:8, :] = cbuf_ref[TS:TS + 8, :]

    hs = range(H)
    p = [_dot(ld[h], ld[h]) for h in hs]
    for _ in range(2):
        pb = [p[h].astype(BF16) for h in hs]
        x = [x[h] + _mm(x[h], pb[h]) for h in hs]
        p = [_dot(pb[h], pb[h]) for h in hs]
    td = [(x[h] + _mm(x[h], p[h])) for h in hs]
    tdb = [td[h].astype(BF16) for h in hs]
    n = [_dot(tdb[h], lo[h]).astype(BF16) for h in hs]
    n2 = [_dot(n[h], n[h]) for h in hs]
    m1 = [td[h] + _mm(n2[h], tdb[h]) for h in hs]
    tinv = [m1[h] - _mm(n[h], m1[h]) for h in hs]
    for h in hs:
        cs = slice(h * D, (h + 1) * D)
        uw = _mm(tinv[h], rhs[h])
        u_ref[:, cs] = uw[:, :D].astype(u_ref.dtype)
        w_ref[:, cs] = uw[:, D:].astype(w_ref.dtype)


def _gdn_prep(wide, ab, conv_w, a_log, dt_bias, B, S):
    T = B * S
    TS = GDN_TS
    ns = S // TS
    HD = GDN_HEADS * GDN_D
    row = lambda b, s: b * ns + s
    vec = pl.BlockSpec((1, LANES), lambda b, s: (0, 0))
    tokw = pl.BlockSpec((TS, HD), lambda b, s: (row(b, s), 0))
    act = jax.ShapeDtypeStruct((T, HD), BF16)
    return pl.pallas_call(
        _gdn_prep_kernel,
        grid=(B, ns),
        in_specs=[pl.BlockSpec((TS, 3 * HD), lambda b, s: (row(b, s), 0)),
                  pl.BlockSpec((TS, LANES), lambda b, s: (row(b, s), 0)),
                  pl.BlockSpec((CONV_WIDTH, 3 * HD), lambda b, s: (0, 0)),
                  vec, vec],
        out_specs=[tokw, tokw, tokw, tokw,
                   pl.BlockSpec((TS, GDN_HEADS * TS), lambda b, s: (row(b, s), 0)),
                   pl.BlockSpec((8, HD), lambda b, s: (row(b, s), 0))],
        out_shape=[act, act, act, act,
                   jax.ShapeDtypeStruct((T, GDN_HEADS * TS), BF16),
                   jax.ShapeDtypeStruct((B * ns * 8, HD), F32)],
        scratch_shapes=[pltpu.VMEM((TS + 8, 3 * HD), F32)],
        compiler_params=pltpu.CompilerParams(
            dimension_semantics=("parallel", "arbitrary"), vmem_limit_bytes=VMEM_LIMIT),
        name="gdn_prep",
    )(wide, ab, conv_w, a_log, dt_bias)


def _gdn_scan_kernel(u_ref, w_ref, qd_ref, kd_ref, attn_ref, egl_ref, z_ref, gn_ref, o_ref,
                     state_ref, vnew_ref, oq_ref):
    TS = GDN_TS
    D = GDN_D
    s = pl.program_id(1)

    @pl.when(s == 0)
    def _():
        state_ref[...] = jnp.zeros(state_ref.shape, F32)

    for c in range(TS // CHUNK):
        r = slice(c * CHUNK, (c + 1) * CHUNK)
        for h in range(GDN_HEADS):
            cs = slice(h * D, (h + 1) * D)
            st = state_ref[h]
            wq = jnp.concatenate([w_ref[r, cs], qd_ref[r, cs]], axis=0)
            ws = _dot(wq, st.astype(BF16))
            v_new = (u_ref[r, cs].astype(F32) - ws[:CHUNK]).astype(BF16)
            vnew_ref[h, r, :] = v_new
            oq_ref[r, cs] = ws[CHUNK:]
            ktv = lax.dot_general(kd_ref[r, cs], v_new, (((0,), (0,)), ((), ())),
                                  preferred_element_type=F32)
            state_ref[h] = st * egl_ref[c:c + 1, cs] + ktv

    for h in range(GDN_HEADS):
        cs = slice(h * D, (h + 1) * D)
        o = oq_ref[:, cs] + _dot(attn_ref[:, h * TS:(h + 1) * TS], vnew_ref[h])
        o = _rms(o, gn_ref[...]) * _silu(z_ref[:, cs].astype(F32))
        o_ref[:, cs] = o.astype(o_ref.dtype)


def _gdn_scan(u, w, qd, kd, attn, egl, wide, gdn_norm_g, B, S):
    T = B * S
    TS = GDN_TS
    ns = S // TS
    HD = GDN_HEADS * GDN_D
    tokw = pl.BlockSpec((TS, HD), lambda b, s: (b * ns + s, 0))
    return pl.pallas_call(
        _gdn_scan_kernel,
        grid=(B, ns),
        in_specs=[tokw, tokw, tokw, tokw,
                  pl.BlockSpec((TS, GDN_HEADS * TS), lambda b, s: (b * ns + s, 0)),
                  pl.BlockSpec((8, HD), lambda b, s: (b * ns + s, 0)),
                  pl.BlockSpec((TS, HD), lambda b, s: (b * ns + s, U_Z * LANES // HD)),
                  pl.BlockSpec((1, GDN_D), lambda b, s: (0, 0))],
        out_specs=tokw,
        out_shape=jax.ShapeDtypeStruct((T, HD), BF16),
        scratch_shapes=[pltpu.VMEM((GDN_HEADS, GDN_D, GDN_D), F32),
                        pltpu.VMEM((GDN_HEADS, TS, GDN_D), BF16),
                        pltpu.VMEM((TS, HD), F32)],
        compiler_params=pltpu.CompilerParams(
            dimension_semantics=("parallel", "arbitrary"), vmem_limit_bytes=VMEM_LIMIT),
        name="gdn_scan",
    )(u, w, qd, kd, attn, egl, wide, gdn_norm_g)


def _diff_attn_kernel(q_ref, k_ref, v_ref, lq1_ref, lk1_ref, lq2_ref, lk2_ref, gn_ref, o_ref,
                      vext_ref, qz_ref, sa_ref, sb_ref, m_ref, acc_ref, *, tq, lam_init):
    i = pl.program_id(2)
    dv = 2 * DIFF_DH

    @pl.when(i == 0)
    def _():
        vext_ref[:, :dv] = v_ref[...]
        vext_ref[:, dv:] = jnp.ones((vext_ref.shape[0], dv), BF16)

    q = q_ref[...]
    lane = lax.broadcasted_iota(jnp.int32, q.shape, 1)
    zero = jnp.zeros_like(q)
    qz_ref[0:tq, :] = jnp.where(lane < DIFF_DH, q, zero)
    qz_ref[tq:2 * tq, :] = jnp.where(lane >= DIFF_DH, q, zero)
    m_ref[...] = jnp.full(m_ref.shape, NEG_BIG, F32)
    acc_ref[...] = jnp.zeros(acc_ref.shape, F32)

    def scores(j, s_ref):
        off = pl.multiple_of(j * tq, tq)
        s_ref[...] = _dot_nt(qz_ref[...], k_ref[pl.ds(off, tq), :])

    def consume(j, s_ref, masked):
        off = pl.multiple_of(j * tq, tq)
        sc = s_ref[...]
        if masked:
            rr = lax.broadcasted_iota(jnp.int32, sc.shape, 0) & (tq - 1)
            cc = lax.broadcasted_iota(jnp.int32, sc.shape, 1)
            sc = jnp.where(rr >= cc, sc, NEG_BIG)
        m_prev = m_ref[...]
        m_new = jnp.maximum(m_prev, jnp.max(sc, axis=-1, keepdims=True))
        p = jnp.exp2(sc - m_new)
        acc_ref[...] = (jnp.exp2(m_prev - m_new) * acc_ref[...]
                        + _dot(p.astype(BF16), vext_ref[pl.ds(off, tq), :]))
        m_ref[...] = m_new

    scores(0, sa_ref)

    def pair(jj, carry):
        j = 2 * jj
        scores(j + 1, sb_ref)
        consume(j, sa_ref, False)
        scores(j + 2, sa_ref)
        consume(j + 1, sb_ref, False)
        return carry

    lax.fori_loop(0, i // 2, pair, 0)

    @pl.when(i % 2 == 1)
    def _():
        scores(i, sb_ref)
        consume(i - 1, sa_ref, False)
        consume(i, sb_ref, True)

    @pl.when(i % 2 == 0)
    def _():
        consume(i, sa_ref, True)

    lam = (jnp.exp(jnp.sum(lq1_ref[...] * lk1_ref[...], axis=-1, keepdims=True))
           - jnp.exp(jnp.sum(lq2_ref[...] * lk2_ref[...], axis=-1, keepdims=True))
           + lam_init)
    a1 = acc_ref[0:tq, :]
    a2 = acc_ref[tq:2 * tq, :]
    o = a1[:, :dv] / a1[:, dv:] - lam * (a2[:, :dv] / a2[:, dv:])
    o = _rms(o, gn_ref[...]) * (1.0 - lam_init)
    o_ref[...] = o.astype(o_ref.dtype)


def _diff_attn(wide, lq1, lk1, lq2, lk2, diff_norm_g, lam_init, B, S, tq=512):
    T = B * S
    nq = S // tq
    dv = 2 * DIFF_DH
    lam_spec = pl.BlockSpec((1, DIFF_DH), lambda b, h, i: (0, 0))
    return pl.pallas_call(
        functools.partial(_diff_attn_kernel, tq=tq, lam_init=lam_init),
        grid=(B, DIFF_HEADS, nq),
        in_specs=[pl.BlockSpec((tq, LANES), lambda b, h, i: (b * nq + i, U_QB + h)),
                  pl.BlockSpec((S, LANES), lambda b, h, i: (b, U_KB + h)),
                  pl.BlockSpec((S, LANES), lambda b, h, i: (b, U_VB + h)),
                  lam_spec, lam_spec, lam_spec, lam_spec,
                  pl.BlockSpec((1, dv), lambda b, h, i: (0, 0))],
        out_specs=pl.BlockSpec((tq, LANES), lambda b, h, i: (b * nq + i, h)),
        out_shape=jax.ShapeDtypeStruct((T, DIFF_HEADS * dv), BF16),
        scratch_shapes=[pltpu.VMEM((S, 2 * dv), BF16),
                        pltpu.VMEM((2 * tq, LANES), BF16),
                        pltpu.VMEM((2 * tq, tq), F32),
                        pltpu.VMEM((2 * tq, tq), F32),
                        pltpu.VMEM((2 * tq, 1), F32),
                        pltpu.VMEM((2 * tq, 2 * dv), F32)],
        compiler_params=pltpu.CompilerParams(
            dimension_semantics=("parallel", "parallel", "arbitrary"),
            vmem_limit_bytes=VMEM_LIMIT),
        name="diff_attn",
    )(wide, wide, wide, lq1, lk1, lq2, lk2, diff_norm_g)


def _merge_kernel(oa_ref, ob_ref, ga_ref, gb_ref, x_ref, woa_ref, wob_ref, wo_ref, x1_ref):
    ya = _dot(oa_ref[...], woa_ref[...])
    yb = _dot(ob_ref[...], wob_ref[...])
    merged = ga_ref[...].astype(F32) * ya + gb_ref[...].astype(F32) * yb
    x1_ref[...] = x_ref[...] + _dot(merged.astype(BF16), wo_ref[...])


def _merge(o_a, o_b, wide, x2d, w_out_a, w_out_b, w_o, tm=512):
    T, D = x2d.shape
    tokd = pl.BlockSpec((tm, D), lambda i: (i, 0))
    wspec = pl.BlockSpec((D, D), lambda i: (0, 0))
    return pl.pallas_call(
        _merge_kernel,
        grid=(T // tm,),
        in_specs=[tokd, tokd,
                  pl.BlockSpec((tm, D), lambda i: (i, U_GA * LANES // D)),
                  pl.BlockSpec((tm, D), lambda i: (i, U_GB * LANES // D)),
                  tokd, wspec, wspec, wspec],
        out_specs=tokd,
        out_shape=jax.ShapeDtypeStruct((T, D), F32),
        compiler_params=pltpu.CompilerParams(
            dimension_semantics=("parallel",), vmem_limit_bytes=VMEM_LIMIT),
        name="merge",
    )(o_a, o_b, wide, wide, x2d, w_out_a, w_out_b, w_o)


def _mem_kv_kernel(m_ref, g_ref, w_ref, o_ref):
    o_ref[...] = _dot(_rms(m_ref[...], g_ref[...]).astype(BF16), w_ref[...]).astype(o_ref.dtype)


def _mem_kv(mem2d, g_mem, w_ckv):
    R, D = mem2d.shape
    N = w_ckv.shape[1]
    return pl.pallas_call(
        _mem_kv_kernel,
        grid=(1,),
        in_specs=[pl.BlockSpec((R, D), lambda i: (0, 0)),
                  pl.BlockSpec((1, D), lambda i: (0, 0)),
                  pl.BlockSpec((D, N), lambda i: (0, 0))],
        out_specs=pl.BlockSpec((R, N), lambda i: (0, 0)),
        out_shape=jax.ShapeDtypeStruct((R, N), BF16),
        compiler_params=pltpu.CompilerParams(vmem_limit_bytes=VMEM_LIMIT),
        name="mem_kv",
    )(mem2d, g_mem, w_ckv)


def _cross_kernel(x_ref, g_ref, wq_ref, kv_ref, wo_ref, o_ref):
    x = x_ref[...]
    hx = _rms(x, g_ref[...]).astype(BF16)
    qc = (_dot(hx, wq_ref[...]) * (X_DH ** -0.5)).astype(BF16)
    xw = X_HEADS * X_DH
    outs = []
    for hh in range(X_HEADS):
        cs = slice(hh * X_DH, (hh + 1) * X_DH)
        sc = _dot_nt(qc[:, cs], kv_ref[:, cs])
        p = jnp.exp(sc - jnp.max(sc, axis=-1, keepdims=True))
        p = p / jnp.sum(p, axis=-1, keepdims=True)
        outs.append(_dot(p.astype(BF16), kv_ref[:, xw + hh * X_DH:xw + (hh + 1) * X_DH]))
    oc = jnp.concatenate(outs, axis=1).astype(BF16)
    o_ref[...] = x + _dot(oc, wo_ref[...])


def _cross(x1, g_cross, w_cq, kv, w_co, S, M, tm=512):
    T, D = x1.shape
    xw = X_HEADS * X_DH
    per_b = S // tm
    tokd = pl.BlockSpec((tm, D), lambda i: (i, 0))
    return pl.pallas_call(
        _cross_kernel,
        grid=(T // tm,),
        in_specs=[tokd,
                  pl.BlockSpec((1, D), lambda i: (0, 0)),
                  pl.BlockSpec((D, xw), lambda i: (0, 0)),
                  pl.BlockSpec((M, 2 * xw), lambda i: (i // per_b, 0)),
                  pl.BlockSpec((xw, D), lambda i: (0, 0))],
        out_specs=tokd,
        out_shape=jax.ShapeDtypeStruct((T, D), F32),
        compiler_params=pltpu.CompilerParams(
            dimension_semantics=("parallel",), vmem_limit_bytes=VMEM_LIMIT),
        name="cross",
    )(x1, g_cross, w_cq, kv, w_co)


def _ffn_kernel(x_ref, g_ref, wi_ref, wo_ref, gf_ref, o_ref, *, tf):
    x = x_ref[...]
    hb = _rms(x, g_ref[...]).astype(BF16)
    F = wo_ref.shape[0]
    acc = x
    for f in range(F // tf):
        act = _silu(_dot(hb, wi_ref[:, f * tf:(f + 1) * tf])) * _dot(hb, wi_ref[:, F + f * tf:F + (f + 1) * tf])
        acc = acc + _dot(act.astype(BF16), wo_ref[f * tf:(f + 1) * tf, :])
    o_ref[...] = _rms(acc, gf_ref[...])


def _ffn(x2, g_ffn, w_ffn_in, w_ffn_out, g_final, tm=512, tf=256):
    T, D = x2.shape
    F = w_ffn_out.shape[0]
    tokd = pl.BlockSpec((tm, D), lambda i: (i, 0))
    vec = pl.BlockSpec((1, D), lambda i: (0, 0))
    resident = pl.Buffered(1)
    return pl.pallas_call(
        functools.partial(_ffn_kernel, tf=tf),
        grid=(T // tm,),
        in_specs=[tokd, vec,
                  pl.BlockSpec((D, 2 * F), lambda i: (0, 0), pipeline_mode=resident),
                  pl.BlockSpec((F, D), lambda i: (0, 0), pipeline_mode=resident),
                  vec],
        out_specs=tokd,
        out_shape=jax.ShapeDtypeStruct((T, D), F32),
        compiler_params=pltpu.CompilerParams(
            dimension_semantics=("parallel",), vmem_limit_bytes=VMEM_LIMIT),
        name="ffn",
    )(x2, g_ffn, w_ffn_in, w_ffn_out, g_final)


def _pad_lanes(v):
    v = v.reshape(1, -1).astype(F32)
    return jnp.pad(v, ((0, 0), (0, LANES - v.shape[1])))


def kernel(x, mem, positions, g_mix, w_in, conv_w, a_log, dt_bias, gdn_norm_g, lambda_q1, lambda_k1, lambda_q2, lambda_k2, diff_norm_g, w_branch_gate, w_out_a, w_out_b, w_o, g_cross, g_mem, w_cq, w_ckv, w_co, g_ffn, w_ffn_in, w_ffn_out, g_final):
    B, S, D = x.shape
    M = mem.shape[1]
    depth = w_in.shape[0]
    assert depth == 1, "the final rmsnorm is fused into the (single) layer's ffn call"
    qkvz = 4 * GDN_HEADS * GDN_D
    xs = x.reshape(B * S, D)
    tables = _rope_tables(positions)
    kv = None
    for l in range(depth):
        w_wide = jnp.concatenate(
            [w_in[l][:, :qkvz], w_in[l][:, qkvz + 2 * GDN_HEADS:], w_branch_gate[l]], axis=1).astype(BF16)
        w_ab = jnp.pad(w_in[l][:, qkvz:qkvz + 2 * GDN_HEADS],
                       ((0, 0), (0, LANES - 2 * GDN_HEADS))).astype(BF16)
        wide, ab = _in_proj(xs, g_mix[l].reshape(1, D), w_wide, w_ab, tables)

        u, w, qd, kd, attn, egl = _gdn_prep(wide, ab, conv_w[l], _pad_lanes(a_log[l]),
                                            _pad_lanes(dt_bias[l]), B, S)
        o_a = _gdn_scan(u, w, qd, kd, attn, egl, wide, gdn_norm_g[l].reshape(1, GDN_D), B, S)

        lam_init = 0.8 - 0.6 * math.exp(-0.3 * l)
        o_b = _diff_attn(wide, lambda_q1[l].reshape(1, -1), lambda_k1[l].reshape(1, -1),
                         lambda_q2[l].reshape(1, -1), lambda_k2[l].reshape(1, -1),
                         diff_norm_g[l].reshape(1, -1), lam_init, B, S)

        xs = _merge(o_a, o_b, wide, xs, w_out_a[l].astype(BF16), w_out_b[l].astype(BF16),
                    w_o[l].astype(BF16))

        kv = _mem_kv(mem.reshape(B * M, D), g_mem[l].reshape(1, D), w_ckv[l].astype(BF16))
        xs = _cross(xs, g_cross[l].reshape(1, D), w_cq[l].astype(BF16), kv, w_co[l].astype(BF16), S, M)

        xs = _ffn(xs, g_ffn[l].reshape(1, D), w_ffn_in[l].astype(BF16), w_ffn_out[l].astype(BF16),
                  g_final.reshape(1, D))
    return xs.reshape(B, S, D)
```

```python
import functools
import math

import jax
import jax.numpy as jnp
from jax import lax
from jax.experimental import pallas as pl
from jax.experimental.pallas import tpu as pltpu

F32 = jnp.float32
BF16 = jnp.bfloat16

EPS = 1e-6
LANES = 128
GDN_HEADS = 8
GDN_D = 128
CONV_WIDTH = 4
CHUNK = 64
DIFF_HEADS = 8
DIFF_DH = 64
ROPE_DIM = DIFF_DH // 4
ROPE_THETA = 500000.0
X_HEADS = 4
X_DH = 128
NEG_BIG = -1e30
LOG2E = math.log2(math.e)

VMEM_LIMIT = 48 * 1024 * 1024

P_QA, P_KA, P_VA, P_Z, P_VB, P_GA, P_GB, P_END = 0, 8, 16, 24, 32, 40, 48, 56
R_Q, R_K = 0, 8


def _dot(a, b):
    return jnp.dot(a, b, preferred_element_type=F32)


def _dot_nt(a, b):
    return lax.dot_general(a, b, (((1,), (1,)), ((), ())), preferred_element_type=F32)


def _mm(a, b):
    return _dot(a.astype(BF16), b.astype(BF16))


def _split3(a):
    hi = a.astype(BF16)
    r = a - hi.astype(F32)
    mid = r.astype(BF16)
    lo = (r - mid.astype(F32)).astype(BF16)
    return hi, mid, lo


def _dot_mask(mask_bf16, b):
    hi, mid, lo = _split3(b)
    return _dot(mask_bf16, hi) + (_dot(mask_bf16, mid) + _dot(mask_bf16, lo))


def _dot_rmask(b, mask_bf16):
    hi, mid, lo = _split3(b)
    return _dot(hi, mask_bf16) + (_dot(mid, mask_bf16) + _dot(lo, mask_bf16))


def _rms(x, g):
    ms = jnp.mean(x * x, axis=-1, keepdims=True)
    return x * lax.rsqrt(ms + EPS) * g


def _sigmoid(x):
    return 1.0 / (1.0 + jnp.exp(-x))


def _silu(x):
    return x * _sigmoid(x)


def _softplus(x):
    return jnp.maximum(x, 0.0) + jnp.log(1.0 + jnp.exp(-jnp.abs(x)))


ROPE_HALF = ROPE_DIM // 2
HALF_LANES = LANES // 2


def _head_lane_source():
    src = [0] * LANES
    for m in range(2):
        for d in range(DIFF_DH):
            if d < ROPE_HALF:
                lane = m * ROPE_HALF + d
            elif d < ROPE_DIM:
                lane = HALF_LANES + m * ROPE_HALF + (d - ROPE_HALF)
            else:
                lane = (ROPE_DIM if m == 0 else HALF_LANES + ROPE_DIM) + (d - ROPE_DIM)
            src[lane] = m * DIFF_DH + d
    return src


def _is_map1_lane(lane):
    return (lane < ROPE_HALF) | ((lane >= ROPE_DIM) & (lane < HALF_LANES + ROPE_HALF))


def _rope_table_kernel(pos_ref, inv_ref, c_ref, sg_ref):
    ang = pos_ref[...] * inv_ref[...]
    cos = jnp.cos(ang)
    sin = jnp.sin(ang)
    lane = lax.broadcasted_iota(jnp.int32, ang.shape, 1)
    first = lane < ROPE_DIM
    second = (lane >= HALF_LANES) & (lane < HALF_LANES + ROPE_DIM)
    c_ref[...] = jnp.where(first | second, cos, 1.0)
    sg_ref[...] = jnp.where(first, -sin, jnp.where(second, sin, 0.0))


def _rope_tables(positions, tm=1024):
    T = positions.size
    pos = positions.astype(F32).reshape(T, 1)
    inv_freq = ROPE_THETA ** (-jnp.arange(0, ROPE_DIM, 2, dtype=F32) / ROPE_DIM)
    inv = jnp.tile(inv_freq, LANES // ROPE_HALF).reshape(1, LANES)
    out = jax.ShapeDtypeStruct((T, LANES), F32)
    return pl.pallas_call(
        _rope_table_kernel,
        grid=(T // tm,),
        in_specs=[pl.BlockSpec((tm, 1), lambda i: (i, 0)),
                  pl.BlockSpec((1, LANES), lambda i: (0, 0))],
        out_specs=[pl.BlockSpec((tm, LANES), lambda i: (i, 0))] * 2,
        out_shape=[out, out],
        name="rope_tables",
    )(pos, inv)


PROJ_SUB = 256


def _in_proj_kernel(x_ref, g_ref, w_ref, wab_ref, out_ref, ab_ref, h_ref):
    @pl.when(pl.program_id(1) == 0)
    def _():
        hb = _rms(x_ref[...], g_ref[...]).astype(BF16)
        h_ref[...] = hb
        ab_ref[...] = _dot(hb, wab_ref[...])

    hb = h_ref[...]
    for c0 in range(0, out_ref.shape[1], PROJ_SUB):
        cs = slice(c0, c0 + PROJ_SUB)
        out_ref[:, cs] = _dot(hb, w_ref[:, cs]).astype(out_ref.dtype)


def _in_proj(x2d, g_mix, w_plain, w_ab, tm=1024, tn=1024):
    T, D = x2d.shape
    N = w_plain.shape[1]
    tm = min(tm, T)
    return pl.pallas_call(
        _in_proj_kernel,
        grid=(T // tm, N // tn),
        in_specs=[pl.BlockSpec((tm, D), lambda i, j: (i, 0)),
                  pl.BlockSpec((1, D), lambda i, j: (0, 0)),
                  pl.BlockSpec((D, tn), lambda i, j: (0, j)),
                  pl.BlockSpec((D, LANES), lambda i, j: (0, 0))],
        out_specs=[pl.BlockSpec((tm, tn), lambda i, j: (i, j)),
                   pl.BlockSpec((tm, LANES), lambda i, j: (i, 0)),
                   pl.BlockSpec((tm, D), lambda i, j: (i, 0))],
        out_shape=[jax.ShapeDtypeStruct((T, N), BF16),
                   jax.ShapeDtypeStruct((T, LANES), F32),
                   jax.ShapeDtypeStruct((T, D), BF16)],
        compiler_params=pltpu.CompilerParams(
            dimension_semantics=("parallel", "arbitrary"), vmem_limit_bytes=VMEM_LIMIT),
        name="in_proj",
    )(x2d, g_mix, w_plain, w_ab)


def _rope_proj_kernel(h_ref, w_ref, c_ref, sg_ref, out_ref):
    scale = jnp.where(pl.program_id(1) == 0, DIFF_DH ** -0.5 * LOG2E, 1.0)
    c = c_ref[...] * scale
    sg = sg_ref[...] * scale
    hb = h_ref[...]
    for c0 in range(0, out_ref.shape[1], PROJ_SUB):
        acc = _dot(hb, w_ref[:, c0:c0 + PROJ_SUB])
        for l0 in range(0, PROJ_SUB, LANES):
            a = acc[:, l0:l0 + LANES]
            y = a * c + pltpu.roll(a, HALF_LANES, 1) * sg
            out_ref[:, c0 + l0:c0 + l0 + LANES] = y.astype(out_ref.dtype)


def _rope_proj(h, w_qk, tables, tm=1024):
    T, D = h.shape
    N = w_qk.shape[1]
    tn = N // 2
    tm = min(tm, T)
    return pl.pallas_call(
        _rope_proj_kernel,
        grid=(T // tm, 2),
        in_specs=[pl.BlockSpec((tm, D), lambda i, j: (i, 0)),
                  pl.BlockSpec((D, tn), lambda i, j: (0, j)),
                  pl.BlockSpec((tm, LANES), lambda i, j: (i, 0)),
                  pl.BlockSpec((tm, LANES), lambda i, j: (i, 0))],
        out_specs=pl.BlockSpec((tm, tn), lambda i, j: (i, j)),
        out_shape=jax.ShapeDtypeStruct((T, N), BF16),
        compiler_params=pltpu.CompilerParams(
            dimension_semantics=("parallel", "parallel"), vmem_limit_bytes=VMEM_LIMIT),
        name="rope_proj",
    )(h, w_qk, *tables)


GDN_TS = 4 * CHUNK
SOLVE_BLOCK = 16
CHUNK_SHIFT = CHUNK.bit_length() - 1
SOLVE_SHIFT = SOLVE_BLOCK.bit_length() - 1
assert (1 << CHUNK_SHIFT) == CHUNK and (1 << SOLVE_SHIFT) == SOLVE_BLOCK and CHUNK // SOLVE_BLOCK == 4


def _gdn_prep_kernel(qkv_ref, ab_ref, cw_ref, alog_ref, dtb_ref,
                     u_ref, w_ref, qd_ref, kd_ref, attn_ref, egl_ref, cbuf_ref):
    TS, H, D = GDN_TS, GDN_HEADS, GDN_D
    s = pl.program_id(1)

    @pl.when(s == 0)
    def _():
        cbuf_ref[0:8, :] = jnp.zeros((8, 3 * H * D), F32)

    cbuf_ref[8:8 + TS, :] = qkv_ref[...].astype(F32)
    base = 8 - (CONV_WIDTH - 1)

    def conv_silu(c0):
        cs = slice(c0, c0 + D)
        acc = cbuf_ref[base:base + TS, cs] * cw_ref[0:1, cs]
        for t in range(1, CONV_WIDTH):
            acc = acc + cbuf_ref[base + t:base + t + TS, cs] * cw_ref[t:t + 1, cs]
        return _silu(acc)

    ri = lax.broadcasted_iota(jnp.int32, (TS, TS), 0)
    ci = lax.broadcasted_iota(jnp.int32, (TS, TS), 1)
    same = (ri >> CHUNK_SHIFT) == (ci >> CHUNK_SHIFT)
    incl = same & (ri >= ci)
    strict = same & (ri > ci)
    blk = (ri >> SOLVE_SHIFT) == (ci >> SOLVE_SHIFT)
    eye = jnp.where(ri == ci, 1.0, 0.0)

    ab = ab_ref[...]
    g_all = -jnp.exp(alog_ref[...]) * _softplus(ab + dtb_ref[...])
    b_all = _sigmoid(ab)
    gc_all = _dot_mask(jnp.where(incl, 1.0, 0.0).astype(BF16), g_all)
    gl_all = _dot_mask(jnp.where(same, 1.0, 0.0).astype(BF16), g_all)
    gc_t = _dot_rmask(g_all.T, jnp.where(same & (ri <= ci), 1.0, 0.0).astype(BF16))

    r8 = lax.broadcasted_iota(jnp.int32, (8, TS), 0)
    c8 = lax.broadcasted_iota(jnp.int32, (8, TS), 1)
    sel = jnp.where(c8 == r8 * CHUNK, 1.0, 0.0).astype(BF16)
    gl8 = _dot_mask(sel, gl_all)

    ld, lo, x, rhs = [], [], [], []
    for h in range(H):
        cs = slice(h * D, (h + 1) * D)
        q = conv_silu(h * D)
        k = conv_silu((H + h) * D)
        v = conv_silu((2 * H + h) * D)
        q = q * lax.rsqrt(jnp.sum(q * q, axis=-1, keepdims=True) + EPS) * (D ** -0.5)
        k = k * lax.rsqrt(jnp.sum(k * k, axis=-1, keepdims=True) + EPS)
        gc_col = gc_all[:, h:h + 1]
        gl_col = gl_all[:, h:h + 1]
        beta = b_all[:, H + h:H + h + 1]
        decay = jnp.exp(jnp.minimum(gc_col - gc_t[h:h + 1, :], 0.0))
        eg = jnp.exp(gc_col)
        kb = k * beta
        kbf = k.astype(BF16)
        lmat = jnp.where(strict, _dot_nt(kb.astype(BF16), kbf) * decay, 0.0)
        attn = jnp.where(incl, _dot_nt(q.astype(BF16), kbf) * decay, 0.0)
        attn_ref[:, h * TS:(h + 1) * TS] = attn.astype(attn_ref.dtype)
        qd_ref[:, cs] = (q * eg).astype(qd_ref.dtype)
        kd_ref[:, cs] = (k * jnp.exp(gl_col - gc_col)).astype(kd_ref.dtype)
        egl_ref[:, cs] = jnp.exp(jnp.broadcast_to(gl8[:, h:h + 1], (8, D)))
        ldh = jnp.where(blk, lmat, 0.0)
        ld.append(ldh.astype(BF16))
        lo.append((lmat - ldh).astype(BF16))
        x.append(eye - ldh)
        rhs.append(jnp.concatenate([v * beta, kb * eg], axis=1).astype(BF16))

    cbuf_ref[0:8, :] = cbuf_ref[TS:TS + 8, :]

    hs = range(H)
    p = [_dot(ld[h], ld[h]) for h in hs]
    for _ in range(2):
        pb = [p[h].astype(BF16) for h in hs]
        x = [x[h] + _mm(x[h], pb[h]) for h in hs]
        p = [_dot(pb[h], pb[h]) for h in hs]
    td = [(x[h] + _mm(x[h], p[h])) for h in hs]
    tdb = [td[h].astype(BF16) for h in hs]
    n = [_dot(tdb[h], lo[h]).astype(BF16) for h in hs]
    n2 = [_dot(n[h], n[h]) for h in hs]
    m1 = [td[h] + _mm(n2[h], tdb[h]) for h in hs]
    tinv = [m1[h] - _mm(n[h], m1[h]) for h in hs]
    for h in hs:
        cs = slice(h * D, (h + 1) * D)
        uw = _mm(tinv[h], rhs[h])
        u_ref[:, cs] = uw[:, :D].astype(u_ref.dtype)
        w_ref[:, cs] = uw[:, D:].astype(w_ref.dtype)


def _gdn_prep(plain, ab, conv_w, a_log, dt_bias, B, S):
    T = B * S
    TS = GDN_TS
    ns = S // TS
    HD = GDN_HEADS * GDN_D
    row = lambda b, s: b * ns + s
    vec = pl.BlockSpec((1, LANES), lambda b, s: (0, 0))
    tokw = pl.BlockSpec((TS, HD), lambda b, s: (row(b, s), 0))
    act = jax.ShapeDtypeStruct((T, HD), BF16)
    return pl.pallas_call(
        _gdn_prep_kernel,
        grid=(B, ns),
        in_specs=[pl.BlockSpec((TS, 3 * HD), lambda b, s: (row(b, s), 0)),
                  pl.BlockSpec((TS, LANES), lambda b, s: (row(b, s), 0)),
                  pl.BlockSpec((CONV_WIDTH, 3 * HD), lambda b, s: (0, 0)),
                  vec, vec],
        out_specs=[tokw, tokw, tokw, tokw,
                   pl.BlockSpec((TS, GDN_HEADS * TS), lambda b, s: (row(b, s), 0)),
                   pl.BlockSpec((8, HD), lambda b, s: (row(b, s), 0))],
        out_shape=[act, act, act, act,
                   jax.ShapeDtypeStruct((T, GDN_HEADS * TS), BF16),
                   jax.ShapeDtypeStruct((B * ns * 8, HD), F32)],
        scratch_shapes=[pltpu.VMEM((TS + 8, 3 * HD), F32)],
        compiler_params=pltpu.CompilerParams(
            dimension_semantics=("parallel", "arbitrary"), vmem_limit_bytes=VMEM_LIMIT),
        name="gdn_prep",
    )(plain, ab, conv_w, a_log, dt_bias)


def _gdn_scan_kernel(u_ref, w_ref, qd_ref, kd_ref, attn_ref, egl_ref, z_ref, gn_ref, o_ref,
                     state_ref, vnew_ref, oq_ref):
    TS = GDN_TS
    D = GDN_D
    s = pl.program_id(1)

    @pl.when(s == 0)
    def _():
        state_ref[...] = jnp.zeros(state_ref.shape, F32)

    for c in range(TS // CHUNK):
        r = slice(c * CHUNK, (c + 1) * CHUNK)
        for h in range(GDN_HEADS):
            cs = slice(h * D, (h + 1) * D)
            st = state_ref[h]
            wq = jnp.concatenate([w_ref[r, cs], qd_ref[r, cs]], axis=0)
            ws = _dot(wq, st.astype(BF16))
            v_new = (u_ref[r, cs].astype(F32) - ws[:CHUNK]).astype(BF16)
            vnew_ref[h, r, :] = v_new
            oq_ref[r, cs] = ws[CHUNK:]
            ktv = lax.dot_general(kd_ref[r, cs], v_new, (((0,), (0,)), ((), ())),
                                  preferred_element_type=F32)
            state_ref[h] = st * egl_ref[c:c + 1, cs] + ktv

    for h in range(GDN_HEADS):
        cs = slice(h * D, (h + 1) * D)
        o = oq_ref[:, cs] + _dot(attn_ref[:, h * TS:(h + 1) * TS], vnew_ref[h])
        o = _rms(o, gn_ref[...]) * _silu(z_ref[:, cs].astype(F32))
        o_ref[:, cs] = o.astype(o_ref.dtype)


def _gdn_scan(u, w, qd, kd, attn, egl, plain, gdn_norm_g, B, S):
    T = B * S
    TS = GDN_TS
    ns = S // TS
    HD = GDN_HEADS * GDN_D
    tokw = pl.BlockSpec((TS, HD), lambda b, s: (b * ns + s, 0))
    return pl.pallas_call(
        _gdn_scan_kernel,
        grid=(B, ns),
        in_specs=[tokw, tokw, tokw, tokw,
                  pl.BlockSpec((TS, GDN_HEADS * TS), lambda b, s: (b * ns + s, 0)),
                  pl.BlockSpec((8, HD), lambda b, s: (b * ns + s, 0)),
                  pl.BlockSpec((TS, HD), lambda b, s: (b * ns + s, P_Z * LANES // HD)),
                  pl.BlockSpec((1, GDN_D), lambda b, s: (0, 0))],
        out_specs=tokw,
        out_shape=jax.ShapeDtypeStruct((T, HD), BF16),
        scratch_shapes=[pltpu.VMEM((GDN_HEADS, GDN_D, GDN_D), F32),
                        pltpu.VMEM((GDN_HEADS, TS, GDN_D), BF16),
                        pltpu.VMEM((TS, HD), F32)],
        compiler_params=pltpu.CompilerParams(
            dimension_semantics=("parallel", "arbitrary"), vmem_limit_bytes=VMEM_LIMIT),
        name="gdn_scan",
    )(u, w, qd, kd, attn, egl, plain, gdn_norm_g)


def _diff_attn_kernel(q_ref, qn_ref, k_ref, v_ref, lq1_ref, lk1_ref, lq2_ref, lk2_ref, gn_ref, o_ref,
                      vext_ref, qz_ref, qnz_ref, sa_ref, sb_ref, sc_ref, m_ref, acc_ref,
                      *, tq, nq, lam_init):
    i = pl.program_id(2)
    dv = 2 * DIFF_DH

    def stack(src_ref, dst_ref):
        q = src_ref[...]
        map1 = _is_map1_lane(lax.broadcasted_iota(jnp.int32, q.shape, 1))
        zero = jnp.zeros_like(q)
        dst_ref[0:tq, :] = jnp.where(map1, q, zero)
        dst_ref[tq:2 * tq, :] = jnp.where(map1, zero, q)

    def scores(j, s_ref, qsrc_ref):
        off = pl.multiple_of(j * tq, tq)
        s_ref[...] = _dot_nt(qsrc_ref[...], k_ref[pl.ds(off, tq), :])

    def consume(j, s_ref, masked):
        off = pl.multiple_of(j * tq, tq)
        sc = s_ref[...]
        if masked:
            rr = lax.broadcasted_iota(jnp.int32, sc.shape, 0) & (tq - 1)
            cc = lax.broadcasted_iota(jnp.int32, sc.shape, 1)
            sc = jnp.where(rr >= cc, sc, NEG_BIG)
        m_prev = m_ref[...]
        m_new = jnp.maximum(m_prev, jnp.max(sc, axis=-1, keepdims=True))
        p = jnp.exp2(sc - m_new)
        acc_ref[...] = (jnp.exp2(m_prev - m_new) * acc_ref[...]
                        + _dot(p.astype(BF16), vext_ref[pl.ds(off, tq), :]))
        m_ref[...] = m_new

    def next_diagonal():
        stack(qn_ref, qnz_ref)
        scores(jnp.minimum(i + 1, nq - 1), sc_ref, qnz_ref)

    stack(q_ref, qz_ref)
    m_ref[...] = jnp.full(m_ref.shape, NEG_BIG, F32)
    acc_ref[...] = jnp.zeros(acc_ref.shape, F32)

    @pl.when(i == 0)
    def _():
        vext_ref[:, :dv] = v_ref[...]
        vext_ref[:, dv:] = jnp.ones((vext_ref.shape[0], dv), BF16)
        scores(0, sc_ref, qz_ref)
        consume(0, sc_ref, True)
        next_diagonal()

    @pl.when(i > 0)
    def _():
        scores(0, sa_ref, qz_ref)
        consume(i, sc_ref, True)

        def pair(p, carry):
            j = 2 * p
            scores(j + 1, sb_ref, qz_ref)
            consume(j, sa_ref, False)
            scores(j + 2, sa_ref, qz_ref)
            consume(j + 1, sb_ref, False)
            return carry

        lax.fori_loop(0, (i - 1) // 2, pair, 0)

        @pl.when(i % 2 == 1)
        def _():
            next_diagonal()
            consume(i - 1, sa_ref, False)

        @pl.when(i % 2 == 0)
        def _():
            scores(i - 1, sb_ref, qz_ref)
            consume(i - 2, sa_ref, False)
            next_diagonal()
            consume(i - 1, sb_ref, False)

    lam = (jnp.exp(jnp.sum(lq1_ref[...] * lk1_ref[...], axis=-1, keepdims=True))
           - jnp.exp(jnp.sum(lq2_ref[...] * lk2_ref[...], axis=-1, keepdims=True))
           + lam_init)
    a1 = acc_ref[0:tq, :]
    a2 = acc_ref[tq:2 * tq, :]
    o = a1[:, :dv] / a1[:, dv:] - lam * (a2[:, :dv] / a2[:, dv:])
    o = _rms(o, gn_ref[...]) * (1.0 - lam_init)
    o_ref[...] = o.astype(o_ref.dtype)


def _diff_attn(rope, plain, lq1, lk1, lq2, lk2, diff_norm_g, lam_init, B, S, tq=512):
    T = B * S
    nq = S // tq
    dv = 2 * DIFF_DH
    lam_spec = pl.BlockSpec((1, DIFF_DH), lambda b, h, i: (0, 0))
    score_buf = pltpu.VMEM((2 * tq, tq), F32)
    stacked_q = pltpu.VMEM((2 * tq, LANES), BF16)
    return pl.pallas_call(
        functools.partial(_diff_attn_kernel, tq=tq, nq=nq, lam_init=lam_init),
        grid=(B, DIFF_HEADS, nq),
        in_specs=[pl.BlockSpec((tq, LANES), lambda b, h, i: (b * nq + i, R_Q + h)),
                  pl.BlockSpec((tq, LANES),
                               lambda b, h, i: (b * nq + jnp.minimum(i + 1, nq - 1), R_Q + h)),
                  pl.BlockSpec((S, LANES), lambda b, h, i: (b, R_K + h)),
                  pl.BlockSpec((S, LANES), lambda b, h, i: (b, P_VB + h)),
                  lam_spec, lam_spec, lam_spec, lam_spec,
                  pl.BlockSpec((1, dv), lambda b, h, i: (0, 0))],
        out_specs=pl.BlockSpec((tq, LANES), lambda b, h, i: (b * nq + i, h)),
        out_shape=jax.ShapeDtypeStruct((T, DIFF_HEADS * dv), BF16),
        scratch_shapes=[pltpu.VMEM((S, 2 * dv), BF16),
                        stacked_q, stacked_q,
                        score_buf, score_buf, score_buf,
                        pltpu.VMEM((2 * tq, 1), F32),
                        pltpu.VMEM((2 * tq, 2 * dv), F32)],
        compiler_params=pltpu.CompilerParams(
            dimension_semantics=("parallel", "parallel", "arbitrary"),
            vmem_limit_bytes=VMEM_LIMIT),
        name="diff_attn",
    )(rope, rope, rope, plain, lq1, lk1, lq2, lk2, diff_norm_g)


def _merge_kernel(oa_ref, ob_ref, ga_ref, gb_ref, x_ref, woa_ref, wob_ref, wo_ref, x1_ref):
    ya = _dot(oa_ref[...], woa_ref[...])
    yb = _dot(ob_ref[...], wob_ref[...])
    merged = _sigmoid(ga_ref[...].astype(F32)) * ya + _sigmoid(gb_ref[...].astype(F32)) * yb
    x1_ref[...] = x_ref[...] + _dot(merged.astype(BF16), wo_ref[...])


def _merge(o_a, o_b, plain, x2d, w_out_a, w_out_b, w_o, tm=512):
    T, D = x2d.shape
    tokd = pl.BlockSpec((tm, D), lambda i: (i, 0))
    wspec = pl.BlockSpec((D, D), lambda i: (0, 0))
    return pl.pallas_call(
        _merge_kernel,
        grid=(T // tm,),
        in_specs=[tokd, tokd,
                  pl.BlockSpec((tm, D), lambda i: (i, P_GA * LANES // D)),
                  pl.BlockSpec((tm, D), lambda i: (i, P_GB * LANES // D)),
                  tokd, wspec, wspec, wspec],
        out_specs=tokd,
        out_shape=jax.ShapeDtypeStruct((T, D), F32),
        compiler_params=pltpu.CompilerParams(
            dimension_semantics=("parallel",), vmem_limit_bytes=VMEM_LIMIT),
        name="merge",
    )(o_a, o_b, plain, plain, x2d, w_out_a, w_out_b, w_o)


def _mem_kv_kernel(m_ref, g_ref, w_ref, o_ref):
    o_ref[...] = _dot(_rms(m_ref[...], g_ref[...]).astype(BF16), w_ref[...]).astype(o_ref.dtype)


def _mem_kv(mem2d, g_mem, w_ckv):
    R, D = mem2d.shape
    N = w_ckv.shape[1]
    return pl.pallas_call(
        _mem_kv_kernel,
        grid=(1,),
        in_specs=[pl.BlockSpec((R, D), lambda i: (0, 0)),
                  pl.BlockSpec((1, D), lambda i: (0, 0)),
                  pl.BlockSpec((D, N), lambda i: (0, 0))],
        out_specs=pl.BlockSpec((R, N), lambda i: (0, 0)),
        out_shape=jax.ShapeDtypeStruct((R, N), BF16),
        compiler_params=pltpu.CompilerParams(vmem_limit_bytes=VMEM_LIMIT),
        name="mem_kv",
    )(mem2d, g_mem, w_ckv)


def _cross_kernel(x_ref, g_ref, wq_ref, kv_ref, wo_ref, o_ref):
    x = x_ref[...]
    hx = _rms(x, g_ref[...]).astype(BF16)
    qc = (_dot(hx, wq_ref[...]) * (X_DH ** -0.5)).astype(BF16)
    xw = X_HEADS * X_DH
    outs = []
    for hh in range(X_HEADS):
        cs = slice(hh * X_DH, (hh + 1) * X_DH)
        sc = _dot_nt(qc[:, cs], kv_ref[:, cs])
        p = jnp.exp(sc - jnp.max(sc, axis=-1, keepdims=True))
        p = p / jnp.sum(p, axis=-1, keepdims=True)
        outs.append(_dot(p.astype(BF16), kv_ref[:, xw + hh * X_DH:xw + (hh + 1) * X_DH]))
    oc = jnp.concatenate(outs, axis=1).astype(BF16)
    o_ref[...] = x + _dot(oc, wo_ref[...])


def _cross(x1, g_cross, w_cq, kv, w_co, S, M, tm=512):
    T, D = x1.shape
    xw = X_HEADS * X_DH
    per_b = S // tm
    tokd = pl.BlockSpec((tm, D), lambda i: (i, 0))
    return pl.pallas_call(
        _cross_kernel,
        grid=(T // tm,),
        in_specs=[tokd,
                  pl.BlockSpec((1, D), lambda i: (0, 0)),
                  pl.BlockSpec((D, xw), lambda i: (0, 0)),
                  pl.BlockSpec((M, 2 * xw), lambda i: (i // per_b, 0)),
                  pl.BlockSpec((xw, D), lambda i: (0, 0))],
        out_specs=tokd,
        out_shape=jax.ShapeDtypeStruct((T, D), F32),
        compiler_params=pltpu.CompilerParams(
            dimension_semantics=("parallel",), vmem_limit_bytes=VMEM_LIMIT),
        name="cross",
    )(x1, g_cross, w_cq, kv, w_co)


def _ffn_kernel(x_ref, g_ref, wi_ref, wo_ref, gf_ref, o_ref, *, tf):
    x = x_ref[...]
    hb = _rms(x, g_ref[...]).astype(BF16)
    F = wo_ref.shape[0]
    acc = x
    for f in range(F // tf):
        act = _silu(_dot(hb, wi_ref[:, f * tf:(f + 1) * tf])) * _dot(hb, wi_ref[:, F + f * tf:F + (f + 1) * tf])
        acc = acc + _dot(act.astype(BF16), wo_ref[f * tf:(f + 1) * tf, :])
    o_ref[...] = _rms(acc, gf_ref[...])


def _ffn(x2, g_ffn, w_ffn_in, w_ffn_out, g_final, tm=512, tf=256):
    T, D = x2.shape
    F = w_ffn_out.shape[0]
    tokd = pl.BlockSpec((tm, D), lambda i: (i, 0))
    vec = pl.BlockSpec((1, D), lambda i: (0, 0))
    resident = pl.Buffered(1)
    return pl.pallas_call(
        functools.partial(_ffn_kernel, tf=tf),
        grid=(T // tm,),
        in_specs=[tokd, vec,
                  pl.BlockSpec((D, 2 * F), lambda i: (0, 0), pipeline_mode=resident),
                  pl.BlockSpec((F, D), lambda i: (0, 0), pipeline_mode=resident),
                  vec],
        out_specs=tokd,
        out_shape=jax.ShapeDtypeStruct((T, D), F32),
        compiler_params=pltpu.CompilerParams(
            dimension_semantics=("parallel",), vmem_limit_bytes=VMEM_LIMIT),
        name="ffn",
    )(x2, g_ffn, w_ffn_in, w_ffn_out, g_final)


def _pad_lanes(v):
    v = v.reshape(1, -1).astype(F32)
    return jnp.pad(v, ((0, 0), (0, LANES - v.shape[1])))


def kernel(x, mem, positions, g_mix, w_in, conv_w, a_log, dt_bias, gdn_norm_g, lambda_q1, lambda_k1, lambda_q2, lambda_k2, diff_norm_g, w_branch_gate, w_out_a, w_out_b, w_o, g_cross, g_mem, w_cq, w_ckv, w_co, g_ffn, w_ffn_in, w_ffn_out, g_final):
    B, S, D = x.shape
    M = mem.shape[1]
    depth = w_in.shape[0]
    assert depth == 1, "the final rmsnorm is fused into the (single) layer's ffn call"
    qkvz = 4 * GDN_HEADS * GDN_D
    xs = x.reshape(B * S, D)
    tables = _rope_tables(positions)
    kv = None
    for l in range(depth):
        ab_end = qkvz + 2 * GDN_HEADS
        qk_end = ab_end + 2 * DIFF_HEADS * 2 * DIFF_DH
        w_plain = jnp.concatenate(
            [w_in[l][:, :qkvz], w_in[l][:, qk_end:], w_branch_gate[l]], axis=1).astype(BF16)
        w_ab = jnp.pad(w_in[l][:, qkvz:ab_end], ((0, 0), (0, LANES - 2 * GDN_HEADS))).astype(BF16)
        plain, ab, h = _in_proj(xs, g_mix[l].reshape(1, D), w_plain, w_ab)
        src = jnp.asarray([hh * LANES + c for hh in range(2 * DIFF_HEADS) for c in _head_lane_source()],
                          dtype=jnp.int32)
        w_qk = jnp.take(w_in[l][:, ab_end:qk_end], src, axis=1).astype(BF16)
        rope = _rope_proj(h, w_qk, tables)

        u, w, qd, kd, attn, egl = _gdn_prep(plain, ab, conv_w[l], _pad_lanes(a_log[l]),
                                            _pad_lanes(dt_bias[l]), B, S)
        o_a = _gdn_scan(u, w, qd, kd, attn, egl, plain, gdn_norm_g[l].reshape(1, GDN_D), B, S)

        lam_init = 0.8 - 0.6 * math.exp(-0.3 * l)
        o_b = _diff_attn(rope, plain, lambda_q1[l].reshape(1, -1), lambda_k1[l].reshape(1, -1),
                         lambda_q2[l].reshape(1, -1), lambda_k2[l].reshape(1, -1),
                         diff_norm_g[l].reshape(1, -1), lam_init, B, S)

        xs = _merge(o_a, o_b, plain, xs,w_out_a[l].astype(BF16), w_out_b[l].astype(BF16),
                    w_o[l].astype(BF16))

        kv = _mem_kv(mem.reshape(B * M, D), g_mem[l].reshape(1, D), w_ckv[l].astype(BF16))
        xs = _cross(xs, g_cross[l].reshape(1, D), w_cq[l].astype(BF16), kv, w_co[l].astype(BF16), S, M)

        xs = _ffn(xs, g_ffn[l].reshape(1, D), w_ffn_in[l].astype(BF16), w_ffn_out[l].astype(BF16),
                  g_final.reshape(1, D))
    return xs.reshape(B, S, D)
```

```python
import functools
import math

import jax
import jax.numpy as jnp
from jax import lax
from jax.experimental import pallas as pl
from jax.experimental.pallas import tpu as pltpu

F32 = jnp.float32
BF16 = jnp.bfloat16

EPS = 1e-6
LANES = 128
GDN_HEADS = 8
GDN_D = 128
CONV_WIDTH = 4
CHUNK = 64
DIFF_HEADS = 8
DIFF_DH = 64
ROPE_DIM = DIFF_DH // 4
ROPE_THETA = 500000.0
X_HEADS = 4
X_DH = 128
NEG_BIG = -1e30
LOG2E = math.log2(math.e)

VMEM_LIMIT = 48 * 1024 * 1024

P_QA, P_KA, P_VA, P_Z, P_VB, P_GA, P_GB, P_END = 0, 8, 16, 24, 32, 40, 48, 56
R_Q, R_K = 0, 8


def _dot(a, b):
    return jnp.dot(a, b, preferred_element_type=F32)


def _dot_nt(a, b):
    return lax.dot_general(a, b, (((1,), (1,)), ((), ())), preferred_element_type=F32)


def _mm(a, b):
    return _dot(a.astype(BF16), b.astype(BF16))


def _split3(a):
    hi = a.astype(BF16)
    r = a - hi.astype(F32)
    mid = r.astype(BF16)
    lo = (r - mid.astype(F32)).astype(BF16)
    return hi, mid, lo


def _dot_mask(mask_bf16, b):
    hi, mid, lo = _split3(b)
    return _dot(mask_bf16, hi) + (_dot(mask_bf16, mid) + _dot(mask_bf16, lo))


def _dot_rmask(b, mask_bf16):
    hi, mid, lo = _split3(b)
    return _dot(hi, mask_bf16) + (_dot(mid, mask_bf16) + _dot(lo, mask_bf16))


def _rms(x, g):
    ms = jnp.mean(x * x, axis=-1, keepdims=True)
    return x * lax.rsqrt(ms + EPS) * g


def _sigmoid(x):
    return 1.0 / (1.0 + jnp.exp(-x))


def _silu(x):
    return x * _sigmoid(x)


def _softplus(x):
    return jnp.maximum(x, 0.0) + jnp.log(1.0 + jnp.exp(-jnp.abs(x)))


ROPE_HALF = ROPE_DIM // 2
HALF_LANES = LANES // 2


def _head_lane_source():
    src = [0] * LANES
    for m in range(2):
        for d in range(DIFF_DH):
            if d < ROPE_HALF:
                lane = m * ROPE_HALF + d
            elif d < ROPE_DIM:
                lane = HALF_LANES + m * ROPE_HALF + (d - ROPE_HALF)
            else:
                lane = (ROPE_DIM if m == 0 else HALF_LANES + ROPE_DIM) + (d - ROPE_DIM)
            src[lane] = m * DIFF_DH + d
    return src


def _is_map1_lane(lane):
    return (lane < ROPE_HALF) | ((lane >= ROPE_DIM) & (lane < HALF_LANES + ROPE_HALF))


def _rope_table_kernel(pos_ref, inv_ref, c_ref, sg_ref):
    ang = pos_ref[...] * inv_ref[...]
    cos = jnp.cos(ang)
    sin = jnp.sin(ang)
    lane = lax.broadcasted_iota(jnp.int32, ang.shape, 1)
    first = lane < ROPE_DIM
    second = (lane >= HALF_LANES) & (lane < HALF_LANES + ROPE_DIM)
    c_ref[...] = jnp.where(first | second, cos, 1.0)
    sg_ref[...] = jnp.where(first, -sin, jnp.where(second, sin, 0.0))


def _rope_tables(positions, tm=1024):
    T = positions.size
    pos = positions.astype(F32).reshape(T, 1)
    inv_freq = ROPE_THETA ** (-jnp.arange(0, ROPE_DIM, 2, dtype=F32) / ROPE_DIM)
    inv = jnp.tile(inv_freq, LANES // ROPE_HALF).reshape(1, LANES)
    out = jax.ShapeDtypeStruct((T, LANES), F32)
    return pl.pallas_call(
        _rope_table_kernel,
        grid=(T // tm,),
        in_specs=[pl.BlockSpec((tm, 1), lambda i: (i, 0)),
                  pl.BlockSpec((1, LANES), lambda i: (0, 0))],
        out_specs=[pl.BlockSpec((tm, LANES), lambda i: (i, 0))] * 2,
        out_shape=[out, out],
        name="rope_tables",
    )(pos, inv)


PROJ_SUB = 256


def _in_proj_kernel(x_ref, g_ref, w_ref, wab_ref, out_ref, ab_ref, h_ref):
    @pl.when(pl.program_id(1) == 0)
    def _():
        hb = _rms(x_ref[...], g_ref[...]).astype(BF16)
        h_ref[...] = hb
        ab_ref[...] = _dot(hb, wab_ref[...])

    hb = h_ref[...]
    for c0 in range(0, out_ref.shape[1], PROJ_SUB):
        cs = slice(c0, c0 + PROJ_SUB)
        out_ref[:, cs] = _dot(hb, w_ref[:, cs]).astype(out_ref.dtype)


def _in_proj(x2d, g_mix, w_plain, w_ab, tm=1024, tn=1024):
    T, D = x2d.shape
    N = w_plain.shape[1]
    tm = min(tm, T)
    return pl.pallas_call(
        _in_proj_kernel,
        grid=(T // tm, N // tn),
        in_specs=[pl.BlockSpec((tm, D), lambda i, j: (i, 0)),
                  pl.BlockSpec((1, D), lambda i, j: (0, 0)),
                  pl.BlockSpec((D, tn), lambda i, j: (0, j)),
                  pl.BlockSpec((D, LANES), lambda i, j: (0, 0))],
        out_specs=[pl.BlockSpec((tm, tn), lambda i, j: (i, j)),
                   pl.BlockSpec((tm, LANES), lambda i, j: (i, 0)),
                   pl.BlockSpec((tm, D), lambda i, j: (i, 0))],
        out_shape=[jax.ShapeDtypeStruct((T, N), BF16),
                   jax.ShapeDtypeStruct((T, LANES), F32),
                   jax.ShapeDtypeStruct((T, D), BF16)],
        compiler_params=pltpu.CompilerParams(
            dimension_semantics=("parallel", "arbitrary"), vmem_limit_bytes=VMEM_LIMIT),
        name="in_proj",
    )(x2d, g_mix, w_plain, w_ab)


def _rope_proj_kernel(h_ref, w_ref, c_ref, sg_ref, out_ref):
    scale = jnp.where(pl.program_id(1) == 0, DIFF_DH ** -0.5 * LOG2E, 1.0)
    c = c_ref[...] * scale
    sg = sg_ref[...] * scale
    hb = h_ref[...]
    for c0 in range(0, out_ref.shape[1], PROJ_SUB):
        acc = _dot(hb, w_ref[:, c0:c0 + PROJ_SUB])
        for l0 in range(0, PROJ_SUB, LANES):
            a = acc[:, l0:l0 + LANES]
            y = a * c + pltpu.roll(a, HALF_LANES, 1) * sg
            out_ref[:, c0 + l0:c0 + l0 + LANES] = y.astype(out_ref.dtype)


def _rope_proj(h, w_qk, tables, tm=1024):
    T, D = h.shape
    N = w_qk.shape[1]
    tn = N // 2
    tm = min(tm, T)
    return pl.pallas_call(
        _rope_proj_kernel,
        grid=(T // tm, 2),
        in_specs=[pl.BlockSpec((tm, D), lambda i, j: (i, 0)),
                  pl.BlockSpec((D, tn), lambda i, j: (0, j)),
                  pl.BlockSpec((tm, LANES), lambda i, j: (i, 0)),
                  pl.BlockSpec((tm, LANES), lambda i, j: (i, 0))],
        out_specs=pl.BlockSpec((tm, tn), lambda i, j: (i, j)),
        out_shape=jax.ShapeDtypeStruct((T, N), BF16),
        compiler_params=pltpu.CompilerParams(
            dimension_semantics=("parallel", "parallel"), vmem_limit_bytes=VMEM_LIMIT),
        name="rope_proj",
    )(h, w_qk, *tables)


GDN_TS = 4 * CHUNK
SOLVE_BLOCK = 16
CHUNK_SHIFT = CHUNK.bit_length() - 1
SOLVE_SHIFT = SOLVE_BLOCK.bit_length() - 1
assert (1 << CHUNK_SHIFT) == CHUNK and (1 << SOLVE_SHIFT) == SOLVE_BLOCK and CHUNK // SOLVE_BLOCK == 4


def _gdn_prep_kernel(qkv_ref, ab_ref, cw_ref, alog_ref, dtb_ref,
                     u_ref, w_ref, qd_ref, kd_ref, attn_ref, egl_ref, cbuf_ref):
    TS, H, D = GDN_TS, GDN_HEADS, GDN_D
    s = pl.program_id(1)

    @pl.when(s == 0)
    def _():
        cbuf_ref[0:8, :] = jnp.zeros((8, 3 * H * D), F32)

    cbuf_ref[8:8 + TS, :] = qkv_ref[...].astype(F32)
    base = 8 - (CONV_WIDTH - 1)

    def conv_silu(c0):
        cs = slice(c0, c0 + D)
        acc = cbuf_ref[base:base + TS, cs] * cw_ref[0:1, cs]
        for t in range(1, CONV_WIDTH):
            acc = acc + cbuf_ref[base + t:base + t + TS, cs] * cw_ref[t:t + 1, cs]
        return _silu(acc)

    ri = lax.broadcasted_iota(jnp.int32, (TS, TS), 0)
    ci = lax.broadcasted_iota(jnp.int32, (TS, TS), 1)
    same = (ri >> CHUNK_SHIFT) == (ci >> CHUNK_SHIFT)
    incl = same & (ri >= ci)
    strict = same & (ri > ci)
    blk = (ri >> SOLVE_SHIFT) == (ci >> SOLVE_SHIFT)
    eye = jnp.where(ri == ci, 1.0, 0.0)

    ab = ab_ref[...]
    g_all = -jnp.exp(alog_ref[...]) * _softplus(ab + dtb_ref[...])
    b_all = _sigmoid(ab)
    gc_all = _dot_mask(jnp.where(incl, 1.0, 0.0).astype(BF16), g_all)
    gl_all = _dot_mask(jnp.where(same, 1.0, 0.0).astype(BF16), g_all)
    gc_t = _dot_rmask(g_all.T, jnp.where(same & (ri <= ci), 1.0, 0.0).astype(BF16))

    r8 = lax.broadcasted_iota(jnp.int32, (8, TS), 0)
    c8 = lax.broadcasted_iota(jnp.int32, (8, TS), 1)
    sel = jnp.where(c8 == r8 * CHUNK, 1.0, 0.0).astype(BF16)
    gl8 = _dot_mask(sel, gl_all)

    ld, lo, x, rhs = [], [], [], []
    for h in range(H):
        cs = slice(h * D, (h + 1) * D)
        q = conv_silu(h * D)
        k = conv_silu((H + h) * D)
        v = conv_silu((2 * H + h) * D)
        q = q * lax.rsqrt(jnp.sum(q * q, axis=-1, keepdims=True) + EPS) * (D ** -0.5)
        k = k * lax.rsqrt(jnp.sum(k * k, axis=-1, keepdims=True) + EPS)
        gc_col = gc_all[:, h:h + 1]
        gl_col = gl_all[:, h:h + 1]
        beta = b_all[:, H + h:H + h + 1]
        decay = jnp.exp(jnp.minimum(gc_col - gc_t[h:h + 1, :], 0.0))
        eg = jnp.exp(gc_col)
        kb = k * beta
        kbf = k.astype(BF16)
        lmat = jnp.where(strict, _dot_nt(kb.astype(BF16), kbf) * decay, 0.0)
        attn = jnp.where(incl, _dot_nt(q.astype(BF16), kbf) * decay, 0.0)
        attn_ref[:, h * TS:(h + 1) * TS] = attn.astype(attn_ref.dtype)
        qd_ref[:, cs] = (q * eg).astype(qd_ref.dtype)
        kd_ref[:, cs] = (k * jnp.exp(gl_col - gc_col)).astype(kd_ref.dtype)
        egl_ref[:, cs] = jnp.exp(jnp.broadcast_to(gl8[:, h:h + 1], (8, D)))
        ldh = jnp.where(blk, lmat, 0.0)
        ld.append(ldh.astype(BF16))
        lo.append((lmat - ldh).astype(BF16))
        x.append(eye - ldh)
        rhs.append(jnp.concatenate([v * beta, kb * eg], axis=1).astype(BF16))

    cbuf_ref[0:8, :] = cbuf_ref[TS:TS + 8, :]

    hs = range(H)
    p = [_dot(ld[h], ld[h]) for h in hs]
    for _ in range(2):
        pb = [p[h].astype(BF16) for h in hs]
        x = [x[h] + _mm(x[h], pb[h]) for h in hs]
        p = [_dot(pb[h], pb[h]) for h in hs]
    td = [(x[h] + _mm(x[h], p[h])) for h in hs]
    tdb = [td[h].astype(BF16) for h in hs]
    n = [_dot(tdb[h], lo[h]).astype(BF16) for h in hs]
    n2 = [_dot(n[h], n[h]) for h in hs]
    m1 = [td[h] + _mm(n2[h], tdb[h]) for h in hs]
    tinv = [m1[h] - _mm(n[h], m1[h]) for h in hs]
    for h in hs:
        cs = slice(h * D, (h + 1) * D)
        uw = _mm(tinv[h], rhs[h])
        u_ref[:, cs] = uw[:, :D].astype(u_ref.dtype)
        w_ref[:, cs] = uw[:, D:].astype(w_ref.dtype)


def _gdn_prep(plain, ab, conv_w, a_log, dt_bias, B, S):
    T = B * S
    TS = GDN_TS
    ns = S // TS
    HD = GDN_HEADS * GDN_D
    row = lambda b, s: b * ns + s
    vec = pl.BlockSpec((1, LANES), lambda b, s: (0, 0))
    tokw = pl.BlockSpec((TS, HD), lambda b, s: (row(b, s), 0))
    act = jax.ShapeDtypeStruct((T, HD), BF16)
    return pl.pallas_call(
        _gdn_prep_kernel,
        grid=(B, ns),
        in_specs=[pl.BlockSpec((TS, 3 * HD), lambda b, s: (row(b, s), 0)),
                  pl.BlockSpec((TS, LANES), lambda b, s: (row(b, s), 0)),
                  pl.BlockSpec((CONV_WIDTH, 3 * HD), lambda b, s: (0, 0)),
                  vec, vec],
        out_specs=[tokw, tokw, tokw, tokw,
                   pl.BlockSpec((TS, GDN_HEADS * TS), lambda b, s: (row(b, s), 0)),
                   pl.BlockSpec((8, HD), lambda b, s: (row(b, s), 0))],
        out_shape=[act, act, act, act,
                   jax.ShapeDtypeStruct((T, GDN_HEADS * TS), BF16),
                   jax.ShapeDtypeStruct((B * ns * 8, HD), F32)],
        scratch_shapes=[pltpu.VMEM((TS + 8, 3 * HD), F32)],
        compiler_params=pltpu.CompilerParams(
            dimension_semantics=("parallel", "arbitrary"), vmem_limit_bytes=VMEM_LIMIT),
        name="gdn_prep",
    )(plain, ab, conv_w, a_log, dt_bias)


def _gdn_scan_kernel(u_ref, w_ref, qd_ref, kd_ref, attn_ref, egl_ref, z_ref, gn_ref, o_ref,
                     state_ref, vnew_ref, oq_ref):
    TS = GDN_TS
    D = GDN_D
    s = pl.program_id(1)

    @pl.when(s == 0)
    def _():
        state_ref[...] = jnp.zeros(state_ref.shape, F32)

    for c in range(TS // CHUNK):
        r = slice(c * CHUNK, (c + 1) * CHUNK)
        for h in range(GDN_HEADS):
            cs = slice(h * D, (h + 1) * D)
            st = state_ref[h]
            wq = jnp.concatenate([w_ref[r, cs], qd_ref[r, cs]], axis=0)
            ws = _dot(wq, st.astype(BF16))
            v_new = (u_ref[r, cs].astype(F32) - ws[:CHUNK]).astype(BF16)
            vnew_ref[h, r, :] = v_new
            oq_ref[r, cs] = ws[CHUNK:]
            ktv = lax.dot_general(kd_ref[r, cs], v_new, (((0,), (0,)), ((), ())),
                                  preferred_element_type=F32)
            state_ref[h] = st * egl_ref[c:c + 1, cs] + ktv

    for h in range(GDN_HEADS):
        cs = slice(h * D, (h + 1) * D)
        o = oq_ref[:, cs] + _dot(attn_ref[:, h * TS:(h + 1) * TS], vnew_ref[h])
        o = _rms(o, gn_ref[...]) * _silu(z_ref[:, cs].astype(F32))
        o_ref[:, cs] = o.astype(o_ref.dtype)


def _gdn_scan(u, w, qd, kd, attn, egl, plain, gdn_norm_g, B, S):
    T = B * S
    TS = GDN_TS
    ns = S // TS
    HD = GDN_HEADS * GDN_D
    tokw = pl.BlockSpec((TS, HD), lambda b, s: (b * ns + s, 0))
    return pl.pallas_call(
        _gdn_scan_kernel,
        grid=(B, ns),
        in_specs=[tokw, tokw, tokw, tokw,
                  pl.BlockSpec((TS, GDN_HEADS * TS), lambda b, s: (b * ns + s, 0)),
                  pl.BlockSpec((8, HD), lambda b, s: (b * ns + s, 0)),
                  pl.BlockSpec((TS, HD), lambda b, s: (b * ns + s, P_Z * LANES // HD)),
                  pl.BlockSpec((1, GDN_D), lambda b, s: (0, 0))],
        out_specs=tokw,
        out_shape=jax.ShapeDtypeStruct((T, HD), BF16),
        scratch_shapes=[pltpu.VMEM((GDN_HEADS, GDN_D, GDN_D), F32),
                        pltpu.VMEM((GDN_HEADS, TS, GDN_D), BF16),
                        pltpu.VMEM((TS, HD), F32)],
        compiler_params=pltpu.CompilerParams(
            dimension_semantics=("parallel", "arbitrary"), vmem_limit_bytes=VMEM_LIMIT),
        name="gdn_scan",
    )(u, w, qd, kd, attn, egl, plain, gdn_norm_g)


def _diff_attn_kernel(q_ref, qn_ref, k_ref, v_ref, lq1_ref, lk1_ref, lq2_ref, lk2_ref, gn_ref, o_ref,
                      vext_ref, qz_ref, qnz_ref, sa_ref, sb_ref, sc_ref, sh_ref, m_ref, acc_ref,
                      *, tq, nq, lam_init):
    i = pl.program_id(2)
    dv = 2 * DIFF_DH
    tk = tq // 2

    def stack(src_ref, dst_ref):
        q = src_ref[...]
        map1 = _is_map1_lane(lax.broadcasted_iota(jnp.int32, q.shape, 1))
        zero = jnp.zeros_like(q)
        q1 = jnp.where(map1, q, zero)
        q2 = jnp.where(map1, zero, q)
        for half in range(2):
            rows = slice(half * tk, (half + 1) * tk)
            dst_ref[2 * half * tk:(2 * half + 1) * tk, :] = q1[rows]
            dst_ref[(2 * half + 1) * tk:(2 * half + 2) * tk, :] = q2[rows]

    def scores(j, s_ref, qsrc_ref, r0=0):
        off = pl.multiple_of(j * tk, tk)
        s_ref[...] = _dot_nt(qsrc_ref[r0:, :], k_ref[pl.ds(off, tk), :])

    def consume(j, s_ref, diag=None, r0=0):
        off = pl.multiple_of(j * tk, tk)
        sc = s_ref[...]
        if diag is not None:
            rr = lax.broadcasted_iota(jnp.int32, sc.shape, 0)
            cc = lax.broadcasted_iota(jnp.int32, sc.shape, 1)
            keep = (rr & (tk - 1)) >= cc
            if diag == "D1":
                keep = keep | (rr >= tq)
            sc = jnp.where(keep, sc, NEG_BIG)
        rows = slice(r0, 2 * tq)
        m_prev = m_ref[rows, :]
        m_new = jnp.maximum(m_prev, jnp.max(sc, axis=-1, keepdims=True))
        p = jnp.concatenate([jnp.exp2(sc[:, c0:c0 + LANES] - m_new) for c0 in range(0, tk, LANES)],
                            axis=1)
        alpha = jnp.exp2(m_prev - m_new)
        pv = _dot(p.astype(BF16), vext_ref[pl.ds(off, tk), :])
        for c0 in range(0, 2 * dv, LANES):
            acc_ref[rows, c0:c0 + LANES] = alpha * acc_ref[rows, c0:c0 + LANES] + pv[:, c0:c0 + LANES]
        m_ref[rows, :] = m_new

    def next_d1():
        nxt = jnp.minimum(i + 1, nq - 1)
        stack(qn_ref, qnz_ref)
        scores(2 * nxt, sc_ref, qnz_ref)

    stack(q_ref, qz_ref)
    m_ref[...] = jnp.full(m_ref.shape, NEG_BIG, F32)
    acc_ref[...] = jnp.zeros(acc_ref.shape, F32)

    @pl.when(i == 0)
    def _():
        vext_ref[:, :dv] = v_ref[...]
        vext_ref[:, dv:] = jnp.ones((vext_ref.shape[0], dv), BF16)
        scores(0, sc_ref, qz_ref)

    scores(2 * i + 1, sh_ref, qz_ref, r0=tq)
    consume(2 * i, sc_ref, diag="D1")

    @pl.when(i == 0)
    def _():
        next_d1()
        consume(1, sh_ref, diag="D2", r0=tq)

    @pl.when(i > 0)
    def _():
        scores(0, sa_ref, qz_ref)
        consume(2 * i + 1, sh_ref, diag="D2", r0=tq)

        def pair(p, carry):
            j = 2 * p
            scores(j + 1, sb_ref, qz_ref)
            consume(j, sa_ref)
            scores(j + 2, sa_ref, qz_ref)
            consume(j + 1, sb_ref)
            return carry

        lax.fori_loop(0, i - 1, pair, 0)
        scores(2 * i - 1, sb_ref, qz_ref)
        consume(2 * i - 2, sa_ref)
        next_d1()
        consume(2 * i - 1, sb_ref)

    lam = (jnp.exp(jnp.sum(lq1_ref[...] * lk1_ref[...], axis=-1, keepdims=True))
           - jnp.exp(jnp.sum(lq2_ref[...] * lk2_ref[...], axis=-1, keepdims=True))
           + lam_init)
    for half in range(2):
        a1 = acc_ref[2 * half * tk:(2 * half + 1) * tk, :]
        a2 = acc_ref[(2 * half + 1) * tk:(2 * half + 2) * tk, :]
        o = a1[:, :dv] / a1[:, dv:] - lam * (a2[:, :dv] / a2[:, dv:])
        o = _rms(o, gn_ref[...]) * (1.0 - lam_init)
        o_ref[half * tk:(half + 1) * tk, :] = o.astype(o_ref.dtype)


def _diff_attn(rope, plain, lq1, lk1, lq2, lk2, diff_norm_g, lam_init, B, S, tq=1024):
    T = B * S
    nq = S // tq
    dv = 2 * DIFF_DH
    lam_spec = pl.BlockSpec((1, DIFF_DH), lambda b, h, i: (0, 0))
    score_buf = pltpu.VMEM((2 * tq, tq // 2), F32)
    stacked_q = pltpu.VMEM((2 * tq, LANES), BF16)
    return pl.pallas_call(
        functools.partial(_diff_attn_kernel, tq=tq, nq=nq, lam_init=lam_init),
        grid=(B, DIFF_HEADS, nq),
        in_specs=[pl.BlockSpec((tq, LANES), lambda b, h, i: (b * nq + i, R_Q + h)),
                  pl.BlockSpec((tq, LANES),
                               lambda b, h, i: (b * nq + jnp.minimum(i + 1, nq - 1), R_Q + h)),
                  pl.BlockSpec((S, LANES), lambda b, h, i: (b, R_K + h)),
                  pl.BlockSpec((S, LANES), lambda b, h, i: (b, P_VB + h)),
                  lam_spec, lam_spec, lam_spec, lam_spec,
                  pl.BlockSpec((1, dv), lambda b, h, i: (0, 0))],
        out_specs=pl.BlockSpec((tq, LANES), lambda b, h, i: (b * nq + i, h)),
        out_shape=jax.ShapeDtypeStruct((T, DIFF_HEADS * dv), BF16),
        scratch_shapes=[pltpu.VMEM((S, 2 * dv), BF16),
                        stacked_q, stacked_q,
                        score_buf, score_buf, score_buf,
                        pltpu.VMEM((tq, tq // 2), F32),
                        pltpu.VMEM((2 * tq, LANES), F32),
                        pltpu.VMEM((2 * tq, 2 * dv), F32)],
        compiler_params=pltpu.CompilerParams(
            dimension_semantics=("parallel", "parallel", "arbitrary"),
            vmem_limit_bytes=VMEM_LIMIT),
        name="diff_attn",
    )(rope, rope, rope, plain, lq1, lk1, lq2, lk2, diff_norm_g)


def _merge_kernel(oa_ref, ob_ref, ga_ref, gb_ref, x_ref, woa_ref, wob_ref, wo_ref, x1_ref):
    ya = _dot(oa_ref[...], woa_ref[...])
    yb = _dot(ob_ref[...], wob_ref[...])
    merged = _sigmoid(ga_ref[...].astype(F32)) * ya + _sigmoid(gb_ref[...].astype(F32)) * yb
    x1_ref[...] = x_ref[...] + _dot(merged.astype(BF16), wo_ref[...])


def _merge(o_a, o_b, plain, x2d, w_out_a, w_out_b, w_o, tm=512):
    T, D = x2d.shape
    tokd = pl.BlockSpec((tm, D), lambda i: (i, 0))
    wspec = pl.BlockSpec((D, D), lambda i: (0, 0))
    return pl.pallas_call(
        _merge_kernel,
        grid=(T // tm,),
        in_specs=[tokd, tokd,
                  pl.BlockSpec((tm, D), lambda i: (i, P_GA * LANES // D)),
                  pl.BlockSpec((tm, D), lambda i: (i, P_GB * LANES // D)),
                  tokd, wspec, wspec, wspec],
        out_specs=tokd,
        out_shape=jax.ShapeDtypeStruct((T, D), F32),
        compiler_params=pltpu.CompilerParams(
            dimension_semantics=("parallel",), vmem_limit_bytes=VMEM_LIMIT),
        name="merge",
    )(o_a, o_b, plain, plain, x2d, w_out_a, w_out_b, w_o)


def _mem_kv_kernel(m_ref, g_ref, w_ref, o_ref):
    o_ref[...] = _dot(_rms(m_ref[...], g_ref[...]).astype(BF16), w_ref[...]).astype(o_ref.dtype)


def _mem_kv(mem2d, g_mem, w_ckv):
    R, D = mem2d.shape
    N = w_ckv.shape[1]
    return pl.pallas_call(
        _mem_kv_kernel,
        grid=(1,),
        in_specs=[pl.BlockSpec((R, D), lambda i: (0, 0)),
                  pl.BlockSpec((1, D), lambda i: (0, 0)),
                  pl.BlockSpec((D, N), lambda i: (0, 0))],
        out_specs=pl.BlockSpec((R, N), lambda i: (0, 0)),
        out_shape=jax.ShapeDtypeStruct((R, N), BF16),
        compiler_params=pltpu.CompilerParams(vmem_limit_bytes=VMEM_LIMIT),
        name="mem_kv",
    )(mem2d, g_mem, w_ckv)


def _cross_kernel(x_ref, g_ref, wq_ref, kv_ref, wo_ref, o_ref):
    x = x_ref[...]
    hx = _rms(x, g_ref[...]).astype(BF16)
    qc = (_dot(hx, wq_ref[...]) * (X_DH ** -0.5)).astype(BF16)
    xw = X_HEADS * X_DH
    outs = []
    for hh in range(X_HEADS):
        cs = slice(hh * X_DH, (hh + 1) * X_DH)
        sc = _dot_nt(qc[:, cs], kv_ref[:, cs])
        p = jnp.exp(sc - jnp.max(sc, axis=-1, keepdims=True))
        p = p / jnp.sum(p, axis=-1, keepdims=True)
        outs.append(_dot(p.astype(BF16), kv_ref[:, xw + hh * X_DH:xw + (hh + 1) * X_DH]))
    oc = jnp.concatenate(outs, axis=1).astype(BF16)
    o_ref[...] = x + _dot(oc, wo_ref[...])


def _cross(x1, g_cross, w_cq, kv, w_co, S, M, tm=512):
    T, D = x1.shape
    xw = X_HEADS * X_DH
    per_b = S // tm
    tokd = pl.BlockSpec((tm, D), lambda i: (i, 0))
    return pl.pallas_call(
        _cross_kernel,
        grid=(T // tm,),
        in_specs=[tokd,
                  pl.BlockSpec((1, D), lambda i: (0, 0)),
                  pl.BlockSpec((D, xw), lambda i: (0, 0)),
                  pl.BlockSpec((M, 2 * xw), lambda i: (i // per_b, 0)),
                  pl.BlockSpec((xw, D), lambda i: (0, 0))],
        out_specs=tokd,
        out_shape=jax.ShapeDtypeStruct((T, D), F32),
        compiler_params=pltpu.CompilerParams(
            dimension_semantics=("parallel",), vmem_limit_bytes=VMEM_LIMIT),
        name="cross",
    )(x1, g_cross, w_cq, kv, w_co)


def _ffn_kernel(x_ref, g_ref, wi_ref, wo_ref, gf_ref, o_ref, *, tf):
    x = x_ref[...]
    hb = _rms(x, g_ref[...]).astype(BF16)
    F = wo_ref.shape[0]
    acc = x
    for f in range(F // tf):
        act = _silu(_dot(hb, wi_ref[:, f * tf:(f + 1) * tf])) * _dot(hb, wi_ref[:, F + f * tf:F + (f + 1) * tf])
        acc = acc + _dot(act.astype(BF16), wo_ref[f * tf:(f + 1) * tf, :])
    o_ref[...] = _rms(acc, gf_ref[...])


def _ffn(x2, g_ffn, w_ffn_in, w_ffn_out, g_final, tm=512, tf=256):
    T, D = x2.shape
    F = w_ffn_out.shape[0]
    tokd = pl.BlockSpec((tm, D), lambda i: (i, 0))
    vec = pl.BlockSpec((1, D), lambda i: (0, 0))
    resident = pl.Buffered(1)
    return pl.pallas_call(
        functools.partial(_ffn_kernel, tf=tf),
        grid=(T // tm,),
        in_specs=[tokd, vec,
                  pl.BlockSpec((D, 2 * F), lambda i: (0, 0), pipeline_mode=resident),
                  pl.BlockSpec((F, D), lambda i: (0, 0), pipeline_mode=resident),
                  vec],
        out_specs=tokd,
        out_shape=jax.ShapeDtypeStruct((T, D), F32),
        compiler_params=pltpu.CompilerParams(
            dimension_semantics=("parallel",), vmem_limit_bytes=VMEM_LIMIT),
        name="ffn",
    )(x2, g_ffn, w_ffn_in, w_ffn_out, g_final)


def _pad_lanes(v):
    v = v.reshape(1, -1).astype(F32)
    return jnp.pad(v, ((0, 0), (0, LANES - v.shape[1])))


def kernel(x, mem, positions, g_mix, w_in, conv_w, a_log, dt_bias, gdn_norm_g, lambda_q1, lambda_k1, lambda_q2, lambda_k2, diff_norm_g, w_branch_gate, w_out_a, w_out_b, w_o, g_cross, g_mem, w_cq, w_ckv, w_co, g_ffn, w_ffn_in, w_ffn_out, g_final):
    B, S, D = x.shape
    M = mem.shape[1]
    depth = w_in.shape[0]
    assert depth == 1, "the final rmsnorm is fused into the (single) layer's ffn call"
    qkvz = 4 * GDN_HEADS * GDN_D
    xs = x.reshape(B * S, D)
    tables = _rope_tables(positions)
    kv = None
    for l in range(depth):
        ab_end = qkvz + 2 * GDN_HEADS
        qk_end = ab_end + 2 * DIFF_HEADS * 2 * DIFF_DH
        w_plain = jnp.concatenate(
            [w_in[l][:, :qkvz], w_in[l][:, qk_end:], w_branch_gate[l]], axis=1).astype(BF16)
        w_ab = jnp.pad(w_in[l][:, qkvz:ab_end], ((0, 0), (0, LANES - 2 * GDN_HEADS))).astype(BF16)
        plain, ab, h = _in_proj(xs, g_mix[l].reshape(1, D), w_plain, w_ab)
        src = jnp.asarray([hh * LANES + c for hh in range(2 * DIFF_HEADS) for c in _head_lane_source()],
                          dtype=jnp.int32)
        w_qk = jnp.take(w_in[l][:, ab_end:qk_end], src, axis=1).astype(BF16)
        rope = _rope_proj(h, w_qk, tables)

        u, w, qd, kd, attn, egl = _gdn_prep(plain, ab, conv_w[l], _pad_lanes(a_log[l]),
                                            _pad_lanes(dt_bias[l]), B, S)
        o_a = _gdn_scan(u, w, qd, kd, attn, egl, plain, gdn_norm_g[l].reshape(1, GDN_D), B, S)

        lam_init = 0.8 - 0.6 * math.exp(-0.3 * l)
        o_b = _diff_attn(rope, plain, lambda_q1[l].reshape(1, -1), lambda_k1[l].reshape(1, -1),
                         lambda_q2[l].reshape(1, -1), lambda_k2[l].reshape(1, -1),
                         diff_norm_g[l].reshape(1, -1), lam_init, B, S)

        xs = _merge(o_a, o_b, plain, xs,w_out_a[l].astype(BF16), w_out_b[l].astype(BF16),
                    w_o[l].astype(BF16))

        kv = _mem_kv(mem.reshape(B * M, D), g_mem[l].reshape(1, D), w_ckv[l].astype(BF16))
        xs = _cross(xs, g_cross[l].reshape(1, D), w_cq[l].astype(BF16), kv, w_co[l].astype(BF16), S, M)

        xs = _ffn(xs, g_ffn[l].reshape(1, D), w_ffn_in[l].astype(BF16), w_ffn_out[l].astype(BF16),
                  g_final.reshape(1, D))
    return xs.reshape(B, S, D)
```

```python
import functools
import math

import jax
import jax.numpy as jnp
from jax import lax
from jax.experimental import pallas as pl
from jax.experimental.pallas import tpu as pltpu

F32 = jnp.float32
BF16 = jnp.bfloat16

EPS = 1e-6
LANES = 128
GDN_HEADS = 8
GDN_D = 128
CONV_WIDTH = 4
CHUNK = 64
DIFF_HEADS = 8
DIFF_DH = 64
ROPE_DIM = DIFF_DH // 4
ROPE_THETA = 500000.0
X_HEADS = 4
X_DH = 128
NEG_BIG = -1e30
LOG2E = math.log2(math.e)

VMEM_LIMIT = 48 * 1024 * 1024

P_QA, P_KA, P_VA, P_Z, P_VB, P_GA, P_GB, P_END = 0, 8, 16, 24, 32, 40, 48, 56
R_Q, R_K = 0, 8


def _dot(a, b):
    return jnp.dot(a, b, preferred_element_type=F32)


def _dot_nt(a, b):
    return lax.dot_general(a, b, (((1,), (1,)), ((), ())), preferred_element_type=F32)


def _mm(a, b):
    return _dot(a.astype(BF16), b.astype(BF16))


def _split3(a):
    hi = a.astype(BF16)
    r = a - hi.astype(F32)
    mid = r.astype(BF16)
    lo = (r - mid.astype(F32)).astype(BF16)
    return hi, mid, lo


def _dot_mask(mask_bf16, b):
    hi, mid, lo = _split3(b)
    return _dot(mask_bf16, hi) + (_dot(mask_bf16, mid) + _dot(mask_bf16, lo))


def _dot_rmask(b, mask_bf16):
    hi, mid, lo = _split3(b)
    return _dot(hi, mask_bf16) + (_dot(mid, mask_bf16) + _dot(lo, mask_bf16))


def _rms(x, g):
    ms = jnp.mean(x * x, axis=-1, keepdims=True)
    return x * lax.rsqrt(ms + EPS) * g


def _sigmoid(x):
    return 0.5 * jnp.tanh(0.5 * x) + 0.5


def _silu(x):
    half = 0.5 * x
    return half * jnp.tanh(half) + half


def _softplus(x):
    return jnp.maximum(x, 0.0) + jnp.log(1.0 + jnp.exp(-jnp.abs(x)))


ROPE_HALF = ROPE_DIM // 2
HALF_LANES = LANES // 2


def _head_lane_source():
    src = [0] * LANES
    for m in range(2):
        for d in range(DIFF_DH):
            if d < ROPE_HALF:
                lane = m * ROPE_HALF + d
            elif d < ROPE_DIM:
                lane = HALF_LANES + m * ROPE_HALF + (d - ROPE_HALF)
            else:
                lane = (ROPE_DIM if m == 0 else HALF_LANES + ROPE_DIM) + (d - ROPE_DIM)
            src[lane] = m * DIFF_DH + d
    return src


def _is_map1_lane(lane):
    return (lane < ROPE_HALF) | ((lane >= ROPE_DIM) & (lane < HALF_LANES + ROPE_HALF))


def _rope_table_kernel(pos_ref, inv_ref, c_ref, sg_ref):
    ang = pos_ref[...] * inv_ref[...]
    cos = jnp.cos(ang)
    sin = jnp.sin(ang)
    lane = lax.broadcasted_iota(jnp.int32, ang.shape, 1)
    first = lane < ROPE_DIM
    second = (lane >= HALF_LANES) & (lane < HALF_LANES + ROPE_DIM)
    c_ref[...] = jnp.where(first | second, cos, 1.0)
    sg_ref[...] = jnp.where(first, -sin, jnp.where(second, sin, 0.0))


def _rope_tables(positions, tm=1024):
    T = positions.size
    pos = positions.astype(F32).reshape(T, 1)
    inv_freq = ROPE_THETA ** (-jnp.arange(0, ROPE_DIM, 2, dtype=F32) / ROPE_DIM)
    inv = jnp.tile(inv_freq, LANES // ROPE_HALF).reshape(1, LANES)
    out = jax.ShapeDtypeStruct((T, LANES), F32)
    return pl.pallas_call(
        _rope_table_kernel,
        grid=(T // tm,),
        in_specs=[pl.BlockSpec((tm, 1), lambda i: (i, 0)),
                  pl.BlockSpec((1, LANES), lambda i: (0, 0))],
        out_specs=[pl.BlockSpec((tm, LANES), lambda i: (i, 0))] * 2,
        out_shape=[out, out],
        name="rope_tables",
    )(pos, inv)


PROJ_SUB = 256


def _in_proj_kernel(x_ref, g_ref, w_ref, wab_ref, out_ref, ab_ref, h_ref):
    @pl.when(pl.program_id(1) == 0)
    def _():
        hb = _rms(x_ref[...], g_ref[...]).astype(BF16)
        h_ref[...] = hb
        ab_ref[...] = _dot(hb, wab_ref[...])

    hb = h_ref[...]
    for c0 in range(0, out_ref.shape[1], PROJ_SUB):
        cs = slice(c0, c0 + PROJ_SUB)
        out_ref[:, cs] = _dot(hb, w_ref[:, cs]).astype(out_ref.dtype)


def _in_proj(x2d, g_mix, w_plain, w_ab, tm=1024, tn=1024):
    T, D = x2d.shape
    N = w_plain.shape[1]
    tm = min(tm, T)
    return pl.pallas_call(
        _in_proj_kernel,
        grid=(T // tm, N // tn),
        in_specs=[pl.BlockSpec((tm, D), lambda i, j: (i, 0)),
                  pl.BlockSpec((1, D), lambda i, j: (0, 0)),
                  pl.BlockSpec((D, tn), lambda i, j: (0, j)),
                  pl.BlockSpec((D, LANES), lambda i, j: (0, 0))],
        out_specs=[pl.BlockSpec((tm, tn), lambda i, j: (i, j)),
                   pl.BlockSpec((tm, LANES), lambda i, j: (i, 0)),
                   pl.BlockSpec((tm, D), lambda i, j: (i, 0))],
        out_shape=[jax.ShapeDtypeStruct((T, N), BF16),
                   jax.ShapeDtypeStruct((T, LANES), F32),
                   jax.ShapeDtypeStruct((T, D), BF16)],
        compiler_params=pltpu.CompilerParams(
            dimension_semantics=("parallel", "arbitrary"), vmem_limit_bytes=VMEM_LIMIT),
        name="in_proj",
    )(x2d, g_mix, w_plain, w_ab)


def _rope_proj_kernel(h_ref, w_ref, c_ref, sg_ref, out_ref):
    scale = jnp.where(pl.program_id(1) == 0, DIFF_DH ** -0.5 * LOG2E, 1.0)
    c = c_ref[...] * scale
    sg = sg_ref[...] * scale
    hb = h_ref[...]
    for c0 in range(0, out_ref.shape[1], PROJ_SUB):
        acc = _dot(hb, w_ref[:, c0:c0 + PROJ_SUB])
        for l0 in range(0, PROJ_SUB, LANES):
            a = acc[:, l0:l0 + LANES]
            y = a * c + pltpu.roll(a, HALF_LANES, 1) * sg
            out_ref[:, c0 + l0:c0 + l0 + LANES] = y.astype(out_ref.dtype)


def _rope_proj(h, w_qk, tables, tm=1024):
    T, D = h.shape
    N = w_qk.shape[1]
    tn = N // 2
    tm = min(tm, T)
    return pl.pallas_call(
        _rope_proj_kernel,
        grid=(T // tm, 2),
        in_specs=[pl.BlockSpec((tm, D), lambda i, j: (i, 0)),
                  pl.BlockSpec((D, tn), lambda i, j: (0, j)),
                  pl.BlockSpec((tm, LANES), lambda i, j: (i, 0)),
                  pl.BlockSpec((tm, LANES), lambda i, j: (i, 0))],
        out_specs=pl.BlockSpec((tm, tn), lambda i, j: (i, j)),
        out_shape=jax.ShapeDtypeStruct((T, N), BF16),
        compiler_params=pltpu.CompilerParams(
            dimension_semantics=("parallel", "parallel"), vmem_limit_bytes=VMEM_LIMIT),
        name="rope_proj",
    )(h, w_qk, *tables)


GDN_TS = 4 * CHUNK
SOLVE_BLOCK = 16
CHUNK_SHIFT = CHUNK.bit_length() - 1
SOLVE_SHIFT = SOLVE_BLOCK.bit_length() - 1
assert (1 << CHUNK_SHIFT) == CHUNK and (1 << SOLVE_SHIFT) == SOLVE_BLOCK and CHUNK // SOLVE_BLOCK == 4


def _gdn_prep_kernel(qkv_ref, ab_ref, cw_ref, alog_ref, dtb_ref,
                     u_ref, w_ref, qd_ref, kd_ref, attn_ref, egl_ref, halo_ref):
    TS, H, D = GDN_TS, GDN_HEADS, GDN_D
    ones = jnp.ones((D, D), BF16)

    @pl.when(pl.program_id(1) == 0)
    def _():
        halo_ref[...] = jnp.zeros(halo_ref.shape, F32)

    def conv_silu(c0):
        cs = slice(c0, c0 + D)
        ext = jnp.concatenate([halo_ref[:, cs], qkv_ref[:, cs].astype(F32)], axis=0)
        z = ext * cw_ref[0:1, cs]
        for t in range(1, CONV_WIDTH):
            z = pltpu.roll(z, 1, 0) + ext * cw_ref[t:t + 1, cs]
        return _silu(z[8:])

    def l2_normalised(x):
        return x * lax.rsqrt(_dot((x * x).astype(BF16), ones) + EPS)

    ri = lax.broadcasted_iota(jnp.int32, (TS, TS), 0)
    ci = lax.broadcasted_iota(jnp.int32, (TS, TS), 1)
    same = (ri >> CHUNK_SHIFT) == (ci >> CHUNK_SHIFT)
    incl = same & (ri >= ci)
    strict = same & (ri > ci)
    blk = (ri >> SOLVE_SHIFT) == (ci >> SOLVE_SHIFT)
    eye = jnp.where(ri == ci, 1.0, 0.0)

    ab = ab_ref[...]
    g_all = -jnp.exp(alog_ref[...]) * _softplus(ab + dtb_ref[...])
    b_all = _sigmoid(ab)
    gc_all = _dot_mask(jnp.where(incl, 1.0, 0.0).astype(BF16), g_all)
    gl_all = _dot_mask(jnp.where(same, 1.0, 0.0).astype(BF16), g_all)
    gc_t = _dot_rmask(g_all.T, jnp.where(same & (ri <= ci), 1.0, 0.0).astype(BF16))

    r8 = lax.broadcasted_iota(jnp.int32, (8, TS), 0)
    c8 = lax.broadcasted_iota(jnp.int32, (8, TS), 1)
    sel = jnp.where(c8 == r8 * CHUNK, 1.0, 0.0).astype(BF16)
    gl8 = _dot_mask(sel, gl_all)

    ld, lo, x, rhs = [], [], [], []
    for h in range(H):
        cs = slice(h * D, (h + 1) * D)
        q = l2_normalised(conv_silu(h * D)) * (D ** -0.5)
        k = l2_normalised(conv_silu((H + h) * D))
        v = conv_silu((2 * H + h) * D)
        gc_col = gc_all[:, h:h + 1]
        gl_col = gl_all[:, h:h + 1]
        beta = b_all[:, H + h:H + h + 1]
        decay = jnp.exp(jnp.minimum(gc_col - gc_t[h:h + 1, :], 0.0))
        eg = jnp.exp(gc_col)
        kb = k * beta
        kbf = k.astype(BF16)
        lmat = jnp.where(strict, _dot_nt(kb.astype(BF16), kbf) * decay, 0.0)
        attn = jnp.where(incl, _dot_nt(q.astype(BF16), kbf) * decay, 0.0)
        attn_ref[:, h * TS:(h + 1) * TS] = attn.astype(attn_ref.dtype)
        qd_ref[:, cs] = (q * eg).astype(qd_ref.dtype)
        kd_ref[:, cs] = (k * jnp.exp(gl_col - gc_col)).astype(kd_ref.dtype)
        egl_ref[:, cs] = jnp.exp(jnp.broadcast_to(gl8[:, h:h + 1], (8, D)))
        ldh = jnp.where(blk, lmat, 0.0)
        ld.append(ldh.astype(BF16))
        lo.append((lmat - ldh).astype(BF16))
        x.append(eye - ldh)
        rhs.append(jnp.concatenate([v * beta, kb * eg], axis=1).astype(BF16))

    halo_ref[...] = qkv_ref[TS - 8:TS, :].astype(F32)

    hs = range(H)
    p = [_dot(ld[h], ld[h]) for h in hs]
    for _ in range(2):
        pb = [p[h].astype(BF16) for h in hs]
        x = [x[h] + _mm(x[h], pb[h]) for h in hs]
        p = [_dot(pb[h], pb[h]) for h in hs]
    td = [(x[h] + _mm(x[h], p[h])) for h in hs]
    tdb = [td[h].astype(BF16) for h in hs]
    n = [_dot(tdb[h], lo[h]).astype(BF16) for h in hs]
    n2 = [_dot(n[h], n[h]) for h in hs]
    m1 = [td[h] + _mm(n2[h], tdb[h]) for h in hs]
    tinv = [m1[h] - _mm(n[h], m1[h]) for h in hs]
    for h in hs:
        cs = slice(h * D, (h + 1) * D)
        uw = _mm(tinv[h], rhs[h])
        u_ref[:, cs] = uw[:, :D].astype(u_ref.dtype)
        w_ref[:, cs] = uw[:, D:].astype(w_ref.dtype)


def _gdn_prep(plain, ab, conv_w, a_log, dt_bias, B, S):
    T = B * S
    TS = GDN_TS
    ns = S // TS
    HD = GDN_HEADS * GDN_D
    row = lambda b, s: b * ns + s
    vec = pl.BlockSpec((1, LANES), lambda b, s: (0, 0))
    tokw = pl.BlockSpec((TS, HD), lambda b, s: (row(b, s), 0))
    act = jax.ShapeDtypeStruct((T, HD), BF16)
    return pl.pallas_call(
        _gdn_prep_kernel,
        grid=(B, ns),
        in_specs=[pl.BlockSpec((TS, 3 * HD), lambda b, s: (row(b, s), 0)),
                  pl.BlockSpec((TS, LANES), lambda b, s: (row(b, s), 0)),
                  pl.BlockSpec((CONV_WIDTH, 3 * HD), lambda b, s: (0, 0)),
                  vec, vec],
        out_specs=[tokw, tokw, tokw, tokw,
                   pl.BlockSpec((TS, GDN_HEADS * TS), lambda b, s: (row(b, s), 0)),
                   pl.BlockSpec((8, HD), lambda b, s: (row(b, s), 0))],
        out_shape=[act, act, act, act,
                   jax.ShapeDtypeStruct((T, GDN_HEADS * TS), BF16),
                   jax.ShapeDtypeStruct((B * ns * 8, HD), F32)],
        scratch_shapes=[pltpu.VMEM((8, 3 * HD), F32)],
        compiler_params=pltpu.CompilerParams(
            dimension_semantics=("parallel", "arbitrary"), vmem_limit_bytes=VMEM_LIMIT),
        name="gdn_prep",
    )(plain, ab, conv_w, a_log, dt_bias)


def _gdn_scan_kernel(u_ref, w_ref, qd_ref, kd_ref, attn_ref, egl_ref, z_ref, gn_ref, o_ref,
                     state_ref, vnew_ref, oq_ref):
    TS = GDN_TS
    D = GDN_D
    s = pl.program_id(1)

    @pl.when(s == 0)
    def _():
        state_ref[...] = jnp.zeros(state_ref.shape, F32)

    for c in range(TS // CHUNK):
        r = slice(c * CHUNK, (c + 1) * CHUNK)
        for h in range(GDN_HEADS):
            cs = slice(h * D, (h + 1) * D)
            st = state_ref[h]
            wq = jnp.concatenate([w_ref[r, cs], qd_ref[r, cs]], axis=0)
            ws = _dot(wq, st.astype(BF16))
            v_new = (u_ref[r, cs].astype(F32) - ws[:CHUNK]).astype(BF16)
            vnew_ref[h, r, :] = v_new
            oq_ref[r, cs] = ws[CHUNK:]
            ktv = lax.dot_general(kd_ref[r, cs], v_new, (((0,), (0,)), ((), ())),
                                  preferred_element_type=F32)
            state_ref[h] = st * egl_ref[c:c + 1, cs] + ktv

    for h in range(GDN_HEADS):
        cs = slice(h * D, (h + 1) * D)
        o = oq_ref[:, cs] + _dot(attn_ref[:, h * TS:(h + 1) * TS], vnew_ref[h])
        o = _rms(o, gn_ref[...]) * _silu(z_ref[:, cs].astype(F32))
        o_ref[:, cs] = o.astype(o_ref.dtype)


def _gdn_scan(u, w, qd, kd, attn, egl, plain, gdn_norm_g, B, S):
    T = B * S
    TS = GDN_TS
    ns = S // TS
    HD = GDN_HEADS * GDN_D
    tokw = pl.BlockSpec((TS, HD), lambda b, s: (b * ns + s, 0))
    return pl.pallas_call(
        _gdn_scan_kernel,
        grid=(B, ns),
        in_specs=[tokw, tokw, tokw, tokw,
                  pl.BlockSpec((TS, GDN_HEADS * TS), lambda b, s: (b * ns + s, 0)),
                  pl.BlockSpec((8, HD), lambda b, s: (b * ns + s, 0)),
                  pl.BlockSpec((TS, HD), lambda b, s: (b * ns + s, P_Z * LANES // HD)),
                  pl.BlockSpec((1, GDN_D), lambda b, s: (0, 0))],
        out_specs=tokw,
        out_shape=jax.ShapeDtypeStruct((T, HD), BF16),
        scratch_shapes=[pltpu.VMEM((GDN_HEADS, GDN_D, GDN_D), F32),
                        pltpu.VMEM((GDN_HEADS, TS, GDN_D), BF16),
                        pltpu.VMEM((TS, HD), F32)],
        compiler_params=pltpu.CompilerParams(
            dimension_semantics=("parallel", "arbitrary"), vmem_limit_bytes=VMEM_LIMIT),
        name="gdn_scan",
    )(u, w, qd, kd, attn, egl, plain, gdn_norm_g)


def _diff_attn_kernel(q_ref, qn_ref, k_ref, v_ref, lq1_ref, lk1_ref, lq2_ref, lk2_ref, gn_ref, o_ref,
                      vext_ref, qz_ref, qnz_ref, sa_ref, sb_ref, sc_ref, sh_ref, m_ref, acc_ref,
                      *, tq, nq, lam_init):
    i = pl.program_id(2)
    dv = 2 * DIFF_DH
    tk = tq // 2

    def stack(src_ref, dst_ref):
        q = src_ref[...]
        map1 = _is_map1_lane(lax.broadcasted_iota(jnp.int32, q.shape, 1))
        zero = jnp.zeros_like(q)
        q1 = jnp.where(map1, q, zero)
        q2 = jnp.where(map1, zero, q)
        for half in range(2):
            rows = slice(half * tk, (half + 1) * tk)
            dst_ref[2 * half * tk:(2 * half + 1) * tk, :] = q1[rows]
            dst_ref[(2 * half + 1) * tk:(2 * half + 2) * tk, :] = q2[rows]

    def scores(j, s_ref, qsrc_ref, r0=0):
        off = pl.multiple_of(j * tk, tk)
        s_ref[...] = _dot_nt(qsrc_ref[r0:, :], k_ref[pl.ds(off, tk), :])

    def consume(j, s_ref, diag=None, r0=0):
        off = pl.multiple_of(j * tk, tk)
        sc = s_ref[...]
        if diag is not None:
            rr = lax.broadcasted_iota(jnp.int32, sc.shape, 0)
            cc = lax.broadcasted_iota(jnp.int32, sc.shape, 1)
            keep = (rr & (tk - 1)) >= cc
            if diag == "D1":
                keep = keep | (rr >= tq)
            sc = jnp.where(keep, sc, NEG_BIG)
        rows = slice(r0, 2 * tq)
        m_prev = m_ref[rows, :]
        m_new = jnp.maximum(m_prev, jnp.max(sc, axis=-1, keepdims=True))
        p = jnp.concatenate([jnp.exp2(sc[:, c0:c0 + LANES] - m_new) for c0 in range(0, tk, LANES)],
                            axis=1)
        alpha = jnp.exp2(m_prev - m_new)
        pv = _dot(p.astype(BF16), vext_ref[pl.ds(off, tk), :])
        for c0 in range(0, 2 * dv, LANES):
            acc_ref[rows, c0:c0 + LANES] = alpha * acc_ref[rows, c0:c0 + LANES] + pv[:, c0:c0 + LANES]
        m_ref[rows, :] = m_new

    def next_d1():
        nxt = jnp.minimum(i + 1, nq - 1)
        stack(qn_ref, qnz_ref)
        scores(2 * nxt, sc_ref, qnz_ref)

    stack(q_ref, qz_ref)
    m_ref[...] = jnp.full(m_ref.shape, NEG_BIG, F32)
    acc_ref[...] = jnp.zeros(acc_ref.shape, F32)

    @pl.when(i == 0)
    def _():
        vext_ref[:, :dv] = v_ref[...]
        vext_ref[:, dv:] = jnp.ones((vext_ref.shape[0], dv), BF16)
        scores(0, sc_ref, qz_ref)

    scores(2 * i + 1, sh_ref, qz_ref, r0=tq)
    consume(2 * i, sc_ref, diag="D1")

    @pl.when(i == 0)
    def _():
        next_d1()
        consume(1, sh_ref, diag="D2", r0=tq)

    @pl.when(i > 0)
    def _():
        scores(0, sa_ref, qz_ref)
        consume(2 * i + 1, sh_ref, diag="D2", r0=tq)

        def pair(p, carry):
            j = 2 * p
            scores(j + 1, sb_ref, qz_ref)
            consume(j, sa_ref)
            scores(j + 2, sa_ref, qz_ref)
            consume(j + 1, sb_ref)
            return carry

        lax.fori_loop(0, i - 1, pair, 0)
        scores(2 * i - 1, sb_ref, qz_ref)
        consume(2 * i - 2, sa_ref)
        next_d1()
        consume(2 * i - 1, sb_ref)

    lam = (jnp.exp(jnp.sum(lq1_ref[...] * lk1_ref[...], axis=-1, keepdims=True))
           - jnp.exp(jnp.sum(lq2_ref[...] * lk2_ref[...], axis=-1, keepdims=True))
           + lam_init)
    for half in range(2):
        a1 = acc_ref[2 * half * tk:(2 * half + 1) * tk, :]
        a2 = acc_ref[(2 * half + 1) * tk:(2 * half + 2) * tk, :]
        o = a1[:, :dv] / a1[:, dv:] - lam * (a2[:, :dv] / a2[:, dv:])
        o = _rms(o, gn_ref[...]) * (1.0 - lam_init)
        o_ref[half * tk:(half + 1) * tk, :] = o.astype(o_ref.dtype)


def _diff_attn(rope, plain, lq1, lk1, lq2, lk2, diff_norm_g, lam_init, B, S, tq=1024):
    T = B * S
    nq = S // tq
    dv = 2 * DIFF_DH
    lam_spec = pl.BlockSpec((1, DIFF_DH), lambda b, h, i: (0, 0))
    score_buf = pltpu.VMEM((2 * tq, tq // 2), F32)
    stacked_q = pltpu.VMEM((2 * tq, LANES), BF16)
    return pl.pallas_call(
        functools.partial(_diff_attn_kernel, tq=tq, nq=nq, lam_init=lam_init),
        grid=(B, DIFF_HEADS, nq),
        in_specs=[pl.BlockSpec((tq, LANES), lambda b, h, i: (b * nq + i, R_Q + h)),
                  pl.BlockSpec((tq, LANES),
                               lambda b, h, i: (b * nq + jnp.minimum(i + 1, nq - 1), R_Q + h)),
                  pl.BlockSpec((S, LANES), lambda b, h, i: (b, R_K + h)),
                  pl.BlockSpec((S, LANES), lambda b, h, i: (b, P_VB + h)),
                  lam_spec, lam_spec, lam_spec, lam_spec,
                  pl.BlockSpec((1, dv), lambda b, h, i: (0, 0))],
        out_specs=pl.BlockSpec((tq, LANES), lambda b, h, i: (b * nq + i, h)),
        out_shape=jax.ShapeDtypeStruct((T, DIFF_HEADS * dv), BF16),
        scratch_shapes=[pltpu.VMEM((S, 2 * dv), BF16),
                        stacked_q, stacked_q,
                        score_buf, score_buf, score_buf,
                        pltpu.VMEM((tq, tq // 2), F32),
                        pltpu.VMEM((2 * tq, LANES), F32),
                        pltpu.VMEM((2 * tq, 2 * dv), F32)],
        compiler_params=pltpu.CompilerParams(
            dimension_semantics=("parallel", "parallel", "arbitrary"),
            vmem_limit_bytes=VMEM_LIMIT),
        name="diff_attn",
    )(rope, rope, rope, plain, lq1, lk1, lq2, lk2, diff_norm_g)


def _merge_kernel(oa_ref, ob_ref, ga_ref, gb_ref, x_ref, woa_ref, wob_ref, wo_ref, x1_ref):
    ya = _dot(oa_ref[...], woa_ref[...])
    yb = _dot(ob_ref[...], wob_ref[...])
    merged = _sigmoid(ga_ref[...].astype(F32)) * ya + _sigmoid(gb_ref[...].astype(F32)) * yb
    x1_ref[...] = x_ref[...] + _dot(merged.astype(BF16), wo_ref[...])


def _merge(o_a, o_b, plain, x2d, w_out_a, w_out_b, w_o, tm=512):
    T, D = x2d.shape
    tokd = pl.BlockSpec((tm, D), lambda i: (i, 0))
    wspec = pl.BlockSpec((D, D), lambda i: (0, 0))
    return pl.pallas_call(
        _merge_kernel,
        grid=(T // tm,),
        in_specs=[tokd, tokd,
                  pl.BlockSpec((tm, D), lambda i: (i, P_GA * LANES // D)),
                  pl.BlockSpec((tm, D), lambda i: (i, P_GB * LANES // D)),
                  tokd, wspec, wspec, wspec],
        out_specs=tokd,
        out_shape=jax.ShapeDtypeStruct((T, D), F32),
        compiler_params=pltpu.CompilerParams(
            dimension_semantics=("parallel",), vmem_limit_bytes=VMEM_LIMIT),
        name="merge",
    )(o_a, o_b, plain, plain, x2d, w_out_a, w_out_b, w_o)


def _mem_kv_kernel(m_ref, g_ref, w_ref, o_ref):
    o_ref[...] = _dot(_rms(m_ref[...], g_ref[...]).astype(BF16), w_ref[...]).astype(o_ref.dtype)


def _mem_kv(mem2d, g_mem, w_ckv):
    R, D = mem2d.shape
    N = w_ckv.shape[1]
    return pl.pallas_call(
        _mem_kv_kernel,
        grid=(1,),
        in_specs=[pl.BlockSpec((R, D), lambda i: (0, 0)),
                  pl.BlockSpec((1, D), lambda i: (0, 0)),
                  pl.BlockSpec((D, N), lambda i: (0, 0))],
        out_specs=pl.BlockSpec((R, N), lambda i: (0, 0)),
        out_shape=jax.ShapeDtypeStruct((R, N), BF16),
        compiler_params=pltpu.CompilerParams(vmem_limit_bytes=VMEM_LIMIT),
        name="mem_kv",
    )(mem2d, g_mem, w_ckv)


def _cross_kernel(x_ref, g_ref, wq_ref, kv_ref, wo_ref, o_ref):
    x = x_ref[...]
    hx = _rms(x, g_ref[...]).astype(BF16)
    qc = (_dot(hx, wq_ref[...]) * (X_DH ** -0.5)).astype(BF16)
    xw = X_HEADS * X_DH
    outs = []
    for hh in range(X_HEADS):
        cs = slice(hh * X_DH, (hh + 1) * X_DH)
        sc = _dot_nt(qc[:, cs], kv_ref[:, cs])
        p = jnp.exp(sc - jnp.max(sc, axis=-1, keepdims=True))
        p = p / jnp.sum(p, axis=-1, keepdims=True)
        outs.append(_dot(p.astype(BF16), kv_ref[:, xw + hh * X_DH:xw + (hh + 1) * X_DH]))
    oc = jnp.concatenate(outs, axis=1).astype(BF16)
    o_ref[...] = x + _dot(oc, wo_ref[...])


def _cross(x1, g_cross, w_cq, kv, w_co, S, M, tm=512):
    T, D = x1.shape
    xw = X_HEADS * X_DH
    per_b = S // tm
    tokd = pl.BlockSpec((tm, D), lambda i: (i, 0))
    return pl.pallas_call(
        _cross_kernel,
        grid=(T // tm,),
        in_specs=[tokd,
                  pl.BlockSpec((1, D), lambda i: (0, 0)),
                  pl.BlockSpec((D, xw), lambda i: (0, 0)),
                  pl.BlockSpec((M, 2 * xw), lambda i: (i // per_b, 0)),
                  pl.BlockSpec((xw, D), lambda i: (0, 0))],
        out_specs=tokd,
        out_shape=jax.ShapeDtypeStruct((T, D), F32),
        compiler_params=pltpu.CompilerParams(
            dimension_semantics=("parallel",), vmem_limit_bytes=VMEM_LIMIT),
        name="cross",
    )(x1, g_cross, w_cq, kv, w_co)


def _ffn_kernel(x_ref, g_ref, wi_ref, wo_ref, gf_ref, o_ref, *, tf):
    x = x_ref[...]
    hb = _rms(x, g_ref[...]).astype(BF16)
    F = wo_ref.shape[0]
    acc = x
    for f in range(F // tf):
        act = _silu(_dot(hb, wi_ref[:, f * tf:(f + 1) * tf])) * _dot(hb, wi_ref[:, F + f * tf:F + (f + 1) * tf])
        acc = acc + _dot(act.astype(BF16), wo_ref[f * tf:(f + 1) * tf, :])
    o_ref[...] = _rms(acc, gf_ref[...])


def _ffn(x2, g_ffn, w_ffn_in, w_ffn_out, g_final, tm=512, tf=256):
    T, D = x2.shape
    F = w_ffn_out.shape[0]
    tokd = pl.BlockSpec((tm, D), lambda i: (i, 0))
    vec = pl.BlockSpec((1, D), lambda i: (0, 0))
    resident = pl.Buffered(1)
    return pl.pallas_call(
        functools.partial(_ffn_kernel, tf=tf),
        grid=(T // tm,),
        in_specs=[tokd, vec,
                  pl.BlockSpec((D, 2 * F), lambda i: (0, 0), pipeline_mode=resident),
                  pl.BlockSpec((F, D), lambda i: (0, 0), pipeline_mode=resident),
                  vec],
        out_specs=tokd,
        out_shape=jax.ShapeDtypeStruct((T, D), F32),
        compiler_params=pltpu.CompilerParams(
            dimension_semantics=("parallel",), vmem_limit_bytes=VMEM_LIMIT),
        name="ffn",
    )(x2, g_ffn, w_ffn_in, w_ffn_out, g_final)


def _pad_lanes(v):
    v = v.reshape(1, -1).astype(F32)
    return jnp.pad(v, ((0, 0), (0, LANES - v.shape[1])))


def kernel(x, mem, positions, g_mix, w_in, conv_w, a_log, dt_bias, gdn_norm_g, lambda_q1, lambda_k1, lambda_q2, lambda_k2, diff_norm_g, w_branch_gate, w_out_a, w_out_b, w_o, g_cross, g_mem, w_cq, w_ckv, w_co, g_ffn, w_ffn_in, w_ffn_out, g_final):
    B, S, D = x.shape
    M = mem.shape[1]
    depth = w_in.shape[0]
    assert depth == 1, "the final rmsnorm is fused into the (single) layer's ffn call"
    qkvz = 4 * GDN_HEADS * GDN_D
    xs = x.reshape(B * S, D)
    tables = _rope_tables(positions)
    kv = None
    for l in range(depth):
        ab_end = qkvz + 2 * GDN_HEADS
        qk_end = ab_end + 2 * DIFF_HEADS * 2 * DIFF_DH
        w_plain = jnp.concatenate(
            [w_in[l][:, :qkvz], w_in[l][:, qk_end:], w_branch_gate[l]], axis=1).astype(BF16)
        w_ab = jnp.pad(w_in[l][:, qkvz:ab_end], ((0, 0), (0, LANES - 2 * GDN_HEADS))).astype(BF16)
        plain, ab, h = _in_proj(xs, g_mix[l].reshape(1, D), w_plain, w_ab)
        src = jnp.asarray([hh * LANES + c for hh in range(2 * DIFF_HEADS) for c in _head_lane_source()],
                          dtype=jnp.int32)
        w_qk = jnp.take(w_in[l][:, ab_end:qk_end], src, axis=1).astype(BF16)
        rope = _rope_proj(h, w_qk, tables)

        u, w, qd, kd, attn, egl = _gdn_prep(plain, ab, conv_w[l], _pad_lanes(a_log[l]),
                                            _pad_lanes(dt_bias[l]), B, S)
        o_a = _gdn_scan(u, w, qd, kd, attn, egl, plain, gdn_norm_g[l].reshape(1, GDN_D), B, S)

        lam_init = 0.8 - 0.6 * math.exp(-0.3 * l)
        o_b = _diff_attn(rope, plain, lambda_q1[l].reshape(1, -1), lambda_k1[l].reshape(1, -1),
                         lambda_q2[l].reshape(1, -1), lambda_k2[l].reshape(1, -1),
                         diff_norm_g[l].reshape(1, -1), lam_init, B, S)

        xs = _merge(o_a, o_b, plain, xs,w_out_a[l].astype(BF16), w_out_b[l].astype(BF16),
                    w_o[l].astype(BF16))

        kv = _mem_kv(mem.reshape(B * M, D), g_mem[l].reshape(1, D), w_ckv[l].astype(BF16))
        xs = _cross(xs, g_cross[l].reshape(1, D), w_cq[l].astype(BF16), kv, w_co[l].astype(BF16), S, M)

        xs = _ffn(xs, g_ffn[l].reshape(1, D), w_ffn_in[l].astype(BF16), w_ffn_out[l].astype(BF16),
                  g_final.reshape(1, D))
    return xs.reshape(B, S, D)
```

```python
import functools
import math

import jax
import jax.numpy as jnp
from jax import lax
from jax.experimental import pallas as pl
from jax.experimental.pallas import tpu as pltpu

F32 = jnp.float32
BF16 = jnp.bfloat16

EPS = 1e-6
LANES = 128
GDN_HEADS = 8
GDN_D = 128
CONV_WIDTH = 4
CHUNK = 64
DIFF_HEADS = 8
DIFF_DH = 64
ROPE_DIM = DIFF_DH // 4
ROPE_THETA = 500000.0
X_HEADS = 4
X_DH = 128
NEG_BIG = -1e30
LOG2E = math.log2(math.e)

VMEM_LIMIT = 48 * 1024 * 1024

P_QA, P_KA, P_VA, P_Z, P_VB, P_GA, P_GB, P_END = 0, 8, 16, 24, 32, 40, 48, 56
R_Q, R_K = 0, 8


def _dot(a, b):
    return jnp.dot(a, b, preferred_element_type=F32)


def _dot_nt(a, b):
    return lax.dot_general(a, b, (((1,), (1,)), ((), ())), preferred_element_type=F32)


def _mm(a, b):
    return _dot(a.astype(BF16), b.astype(BF16))


def _split3(a):
    hi = a.astype(BF16)
    r = a - hi.astype(F32)
    mid = r.astype(BF16)
    lo = (r - mid.astype(F32)).astype(BF16)
    return hi, mid, lo


def _dot_mask(mask_bf16, b):
    hi, mid, lo = _split3(b)
    return _dot(mask_bf16, hi) + (_dot(mask_bf16, mid) + _dot(mask_bf16, lo))


def _dot_rmask(b, mask_bf16):
    hi, mid, lo = _split3(b)
    return _dot(hi, mask_bf16) + (_dot(mid, mask_bf16) + _dot(lo, mask_bf16))


def _rms(x, g):
    ms = jnp.mean(x * x, axis=-1, keepdims=True)
    return x * lax.rsqrt(ms + EPS) * g


def _sigmoid(x):
    return 0.5 * jnp.tanh(0.5 * x) + 0.5


def _silu(x):
    half = 0.5 * x
    return half * jnp.tanh(half) + half


def _softplus(x):
    return jnp.maximum(x, 0.0) + jnp.log(1.0 + jnp.exp(-jnp.abs(x)))


ROPE_HALF = ROPE_DIM // 2
HALF_LANES = LANES // 2


def _head_lane_source():
    src = [0] * LANES
    for m in range(2):
        for d in range(DIFF_DH):
            if d < ROPE_HALF:
                lane = m * ROPE_HALF + d
            elif d < ROPE_DIM:
                lane = HALF_LANES + m * ROPE_HALF + (d - ROPE_HALF)
            else:
                lane = (ROPE_DIM if m == 0 else HALF_LANES + ROPE_DIM) + (d - ROPE_DIM)
            src[lane] = m * DIFF_DH + d
    return src


def _is_map1_lane(lane):
    return (lane < ROPE_HALF) | ((lane >= ROPE_DIM) & (lane < HALF_LANES + ROPE_HALF))


def _rope_table_kernel(pos_ref, inv_ref, c_ref, sg_ref):
    ang = pos_ref[...] * inv_ref[...]
    cos = jnp.cos(ang)
    sin = jnp.sin(ang)
    lane = lax.broadcasted_iota(jnp.int32, ang.shape, 1)
    first = lane < ROPE_DIM
    second = (lane >= HALF_LANES) & (lane < HALF_LANES + ROPE_DIM)
    c_ref[...] = jnp.where(first | second, cos, 1.0)
    sg_ref[...] = jnp.where(first, -sin, jnp.where(second, sin, 0.0))


def _rope_tables(positions, tm=1024):
    T = positions.size
    pos = positions.astype(F32).reshape(T, 1)
    inv_freq = ROPE_THETA ** (-jnp.arange(0, ROPE_DIM, 2, dtype=F32) / ROPE_DIM)
    inv = jnp.tile(inv_freq, LANES // ROPE_HALF).reshape(1, LANES)
    out = jax.ShapeDtypeStruct((T, LANES), F32)
    return pl.pallas_call(
        _rope_table_kernel,
        grid=(T // tm,),
        in_specs=[pl.BlockSpec((tm, 1), lambda i: (i, 0)),
                  pl.BlockSpec((1, LANES), lambda i: (0, 0))],
        out_specs=[pl.BlockSpec((tm, LANES), lambda i: (i, 0))] * 2,
        out_shape=[out, out],
        name="rope_tables",
    )(pos, inv)


PROJ_SUB = 256


def _in_proj_kernel(x_ref, g_ref, w_ref, wab_ref, out_ref, ab_ref, h_ref):
    @pl.when(pl.program_id(1) == 0)
    def _():
        hb = _rms(x_ref[...], g_ref[...]).astype(BF16)
        h_ref[...] = hb
        ab_ref[...] = _dot(hb, wab_ref[...])

    hb = h_ref[...]
    for c0 in range(0, out_ref.shape[1], PROJ_SUB):
        cs = slice(c0, c0 + PROJ_SUB)
        out_ref[:, cs] = _dot(hb, w_ref[:, cs]).astype(out_ref.dtype)


def _in_proj(x2d, g_mix, w_plain, w_ab, tm=1024, tn=1792):
    T, D = x2d.shape
    N = w_plain.shape[1]
    tm = min(tm, T)
    return pl.pallas_call(
        _in_proj_kernel,
        grid=(T // tm, N // tn),
        in_specs=[pl.BlockSpec((tm, D), lambda i, j: (i, 0)),
                  pl.BlockSpec((1, D), lambda i, j: (0, 0)),
                  pl.BlockSpec((D, tn), lambda i, j: (0, j)),
                  pl.BlockSpec((D, LANES), lambda i, j: (0, 0))],
        out_specs=[pl.BlockSpec((tm, tn), lambda i, j: (i, j)),
                   pl.BlockSpec((tm, LANES), lambda i, j: (i, 0)),
                   pl.BlockSpec((tm, D), lambda i, j: (i, 0))],
        out_shape=[jax.ShapeDtypeStruct((T, N), BF16),
                   jax.ShapeDtypeStruct((T, LANES), F32),
                   jax.ShapeDtypeStruct((T, D), BF16)],
        compiler_params=pltpu.CompilerParams(
            dimension_semantics=("parallel", "arbitrary"), vmem_limit_bytes=VMEM_LIMIT),
        name="in_proj",
    )(x2d, g_mix, w_plain, w_ab)


def _rope_proj_kernel(h_ref, w_ref, c_ref, sg_ref, out_ref):
    scale = jnp.where(pl.program_id(1) == 0, DIFF_DH ** -0.5 * LOG2E, 1.0)
    c = c_ref[...] * scale
    sg = sg_ref[...] * scale
    hb = h_ref[...]
    for c0 in range(0, out_ref.shape[1], PROJ_SUB):
        acc = _dot(hb, w_ref[:, c0:c0 + PROJ_SUB])
        for l0 in range(0, PROJ_SUB, LANES):
            a = acc[:, l0:l0 + LANES]
            y = a * c + pltpu.roll(a, HALF_LANES, 1) * sg
            out_ref[:, c0 + l0:c0 + l0 + LANES] = y.astype(out_ref.dtype)


def _rope_proj(h, w_qk, tables, tm=1024):
    T, D = h.shape
    N = w_qk.shape[1]
    tn = N // 2
    tm = min(tm, T)
    return pl.pallas_call(
        _rope_proj_kernel,
        grid=(T // tm, 2),
        in_specs=[pl.BlockSpec((tm, D), lambda i, j: (i, 0)),
                  pl.BlockSpec((D, tn), lambda i, j: (0, j)),
                  pl.BlockSpec((tm, LANES), lambda i, j: (i, 0)),
                  pl.BlockSpec((tm, LANES), lambda i, j: (i, 0))],
        out_specs=pl.BlockSpec((tm, tn), lambda i, j: (i, j)),
        out_shape=jax.ShapeDtypeStruct((T, N), BF16),
        compiler_params=pltpu.CompilerParams(
            dimension_semantics=("parallel", "parallel"), vmem_limit_bytes=VMEM_LIMIT),
        name="rope_proj",
    )(h, w_qk, *tables)


GDN_TS = 4 * CHUNK
SOLVE_BLOCK = 16
CHUNK_SHIFT = CHUNK.bit_length() - 1
SOLVE_SHIFT = SOLVE_BLOCK.bit_length() - 1
assert (1 << CHUNK_SHIFT) == CHUNK and (1 << SOLVE_SHIFT) == SOLVE_BLOCK and CHUNK // SOLVE_BLOCK == 4


def _gdn_prep_kernel(qkv_ref, ab_ref, cw_ref, alog_ref, dtb_ref,
                     u_ref, w_ref, qd_ref, kd_ref, attn_ref, egl_ref, halo_ref):
    TS, H, D = GDN_TS, GDN_HEADS, GDN_D
    ones = jnp.ones((D, D), BF16)

    @pl.when(pl.program_id(1) == 0)
    def _():
        halo_ref[...] = jnp.zeros(halo_ref.shape, F32)

    def conv_silu(c0):
        cs = slice(c0, c0 + D)
        ext = jnp.concatenate([halo_ref[:, cs], qkv_ref[:, cs].astype(F32)], axis=0)
        z = ext * cw_ref[0:1, cs]
        for t in range(1, CONV_WIDTH):
            z = pltpu.roll(z, 1, 0) + ext * cw_ref[t:t + 1, cs]
        return _silu(z[8:])

    def l2_normalised(x):
        return x * lax.rsqrt(_dot((x * x).astype(BF16), ones) + EPS)

    ri = lax.broadcasted_iota(jnp.int32, (TS, TS), 0)
    ci = lax.broadcasted_iota(jnp.int32, (TS, TS), 1)
    same = (ri >> CHUNK_SHIFT) == (ci >> CHUNK_SHIFT)
    incl = same & (ri >= ci)
    strict = same & (ri > ci)
    blk = (ri >> SOLVE_SHIFT) == (ci >> SOLVE_SHIFT)
    eye = jnp.where(ri == ci, 1.0, 0.0)

    ab = ab_ref[...]
    g_all = -jnp.exp(alog_ref[...]) * _softplus(ab + dtb_ref[...])
    b_all = _sigmoid(ab)
    gc_all = _dot_mask(jnp.where(incl, 1.0, 0.0).astype(BF16), g_all)
    gl_all = _dot_mask(jnp.where(same, 1.0, 0.0).astype(BF16), g_all)
    gc_t = _dot_rmask(g_all.T, jnp.where(same & (ri <= ci), 1.0, 0.0).astype(BF16))

    r8 = lax.broadcasted_iota(jnp.int32, (8, TS), 0)
    c8 = lax.broadcasted_iota(jnp.int32, (8, TS), 1)
    sel = jnp.where(c8 == r8 * CHUNK, 1.0, 0.0).astype(BF16)
    gl8 = _dot_mask(sel, gl_all)

    ld, lo, x, rhs = [], [], [], []
    for h in range(H):
        cs = slice(h * D, (h + 1) * D)
        q = l2_normalised(conv_silu(h * D)) * (D ** -0.5)
        k = l2_normalised(conv_silu((H + h) * D))
        v = conv_silu((2 * H + h) * D)
        gc_col = gc_all[:, h:h + 1]
        gl_col = gl_all[:, h:h + 1]
        beta = b_all[:, H + h:H + h + 1]
        decay = jnp.exp(jnp.minimum(gc_col - gc_t[h:h + 1, :], 0.0))
        eg = jnp.exp(gc_col)
        kb = k * beta
        kbf = k.astype(BF16)
        lmat = jnp.where(strict, _dot_nt(kb.astype(BF16), kbf) * decay, 0.0)
        attn = jnp.where(incl, _dot_nt(q.astype(BF16), kbf) * decay, 0.0)
        attn_ref[:, h * TS:(h + 1) * TS] = attn.astype(attn_ref.dtype)
        qd_ref[:, cs] = (q * eg).astype(qd_ref.dtype)
        kd_ref[:, cs] = (k * jnp.exp(gl_col - gc_col)).astype(kd_ref.dtype)
        egl_ref[:, cs] = jnp.exp(jnp.broadcast_to(gl8[:, h:h + 1], (8, D)))
        ldh = jnp.where(blk, lmat, 0.0)
        ld.append(ldh.astype(BF16))
        lo.append((lmat - ldh).astype(BF16))
        x.append(eye - ldh)
        rhs.append(jnp.concatenate([v * beta, kb * eg], axis=1).astype(BF16))

    halo_ref[...] = qkv_ref[TS - 8:TS, :].astype(F32)

    hs = range(H)
    p = [_dot(ld[h], ld[h]) for h in hs]
    for _ in range(2):
        pb = [p[h].astype(BF16) for h in hs]
        x = [x[h] + _mm(x[h], pb[h]) for h in hs]
        p = [_dot(pb[h], pb[h]) for h in hs]
    td = [(x[h] + _mm(x[h], p[h])) for h in hs]
    tdb = [td[h].astype(BF16) for h in hs]
    n = [_dot(tdb[h], lo[h]).astype(BF16) for h in hs]
    n2 = [_dot(n[h], n[h]) for h in hs]
    m1 = [td[h] + _mm(n2[h], tdb[h]) for h in hs]
    tinv = [m1[h] - _mm(n[h], m1[h]) for h in hs]
    for h in hs:
        cs = slice(h * D, (h + 1) * D)
        uw = _mm(tinv[h], rhs[h])
        u_ref[:, cs] = uw[:, :D].astype(u_ref.dtype)
        w_ref[:, cs] = uw[:, D:].astype(w_ref.dtype)


def _gdn_prep(plain, ab, conv_w, a_log, dt_bias, B, S):
    T = B * S
    TS = GDN_TS
    ns = S // TS
    HD = GDN_HEADS * GDN_D
    row = lambda b, s: b * ns + s
    vec = pl.BlockSpec((1, LANES), lambda b, s: (0, 0))
    tokw = pl.BlockSpec((TS, HD), lambda b, s: (row(b, s), 0))
    act = jax.ShapeDtypeStruct((T, HD), BF16)
    return pl.pallas_call(
        _gdn_prep_kernel,
        grid=(B, ns),
        in_specs=[pl.BlockSpec((TS, 3 * HD), lambda b, s: (row(b, s), 0)),
                  pl.BlockSpec((TS, LANES), lambda b, s: (row(b, s), 0)),
                  pl.BlockSpec((CONV_WIDTH, 3 * HD), lambda b, s: (0, 0)),
                  vec, vec],
        out_specs=[tokw, tokw, tokw, tokw,
                   pl.BlockSpec((TS, GDN_HEADS * TS), lambda b, s: (row(b, s), 0)),
                   pl.BlockSpec((8, HD), lambda b, s: (row(b, s), 0))],
        out_shape=[act, act, act, act,
                   jax.ShapeDtypeStruct((T, GDN_HEADS * TS), BF16),
                   jax.ShapeDtypeStruct((B * ns * 8, HD), F32)],
        scratch_shapes=[pltpu.VMEM((8, 3 * HD), F32)],
        compiler_params=pltpu.CompilerParams(
            dimension_semantics=("parallel", "arbitrary"), vmem_limit_bytes=VMEM_LIMIT),
        name="gdn_prep",
    )(plain, ab, conv_w, a_log, dt_bias)


def _gdn_scan_kernel(u_ref, w_ref, qd_ref, kd_ref, attn_ref, egl_ref, z_ref, gn_ref, o_ref,
                     state_ref, vnew_ref, oq_ref):
    TS = GDN_TS
    D = GDN_D
    nb = u_ref.shape[0]
    s = pl.program_id(1)

    @pl.when(s == 0)
    def _():
        state_ref[...] = jnp.zeros(state_ref.shape, F32)

    for c in range(TS // CHUNK):
        r = slice(c * CHUNK, (c + 1) * CHUNK)
        for b in range(nb):
            for h in range(GDN_HEADS):
                cs = slice(h * D, (h + 1) * D)
                st = state_ref[b, h]
                wq = jnp.concatenate([w_ref[b, r, cs], qd_ref[b, r, cs]], axis=0)
                ws = _dot(wq, st.astype(BF16))
                v_new = (u_ref[b, r, cs].astype(F32) - ws[:CHUNK]).astype(BF16)
                vnew_ref[b, h, r, :] = v_new
                oq_ref[b, r, cs] = ws[CHUNK:]
                ktv = lax.dot_general(kd_ref[b, r, cs], v_new, (((0,), (0,)), ((), ())),
                                      preferred_element_type=F32)
                state_ref[b, h] = st * egl_ref[b, c:c + 1, cs] + ktv

    for b in range(nb):
        for h in range(GDN_HEADS):
            cs = slice(h * D, (h + 1) * D)
            o = oq_ref[b, :, cs] + _dot(attn_ref[b, :, h * TS:(h + 1) * TS], vnew_ref[b, h])
            o = _rms(o, gn_ref[...]) * _silu(z_ref[b, :, cs].astype(F32))
            o_ref[b, :, cs] = o.astype(o_ref.dtype)


def _gdn_scan(u, w, qd, kd, attn, egl, plain, gdn_norm_g, B, S, nb=4):
    TS = GDN_TS
    ns = S // TS
    HD = GDN_HEADS * GDN_D
    nb = min(nb, B)
    assert B % nb == 0
    seq = lambda a: a.reshape(B, S, a.shape[-1])
    tokw = pl.BlockSpec((nb, TS, HD), lambda b, s: (b, s, 0))
    out = pl.pallas_call(
        _gdn_scan_kernel,
        grid=(B // nb, ns),
        in_specs=[tokw, tokw, tokw, tokw,
                  pl.BlockSpec((nb, TS, GDN_HEADS * TS), lambda b, s: (b, s, 0)),
                  pl.BlockSpec((nb, 8, HD), lambda b, s: (b, s, 0)),
                  pl.BlockSpec((nb, TS, HD), lambda b, s: (b, s, P_Z * LANES // HD)),
                  pl.BlockSpec((1, GDN_D), lambda b, s: (0, 0))],
        out_specs=tokw,
        out_shape=jax.ShapeDtypeStruct((B, S, HD), BF16),
        scratch_shapes=[pltpu.VMEM((nb, GDN_HEADS, GDN_D, GDN_D), F32),
                        pltpu.VMEM((nb, GDN_HEADS, TS, GDN_D), BF16),
                        pltpu.VMEM((nb, TS, HD), F32)],
        compiler_params=pltpu.CompilerParams(
            dimension_semantics=("parallel", "arbitrary"), vmem_limit_bytes=VMEM_LIMIT),
        name="gdn_scan",
    )(seq(u), seq(w), seq(qd), seq(kd), seq(attn), egl.reshape(B, ns * 8, HD), seq(plain), gdn_norm_g)
    return out.reshape(B * S, HD)


def _diff_attn_kernel(q_ref, qn_ref, k_ref, v_ref, lq1_ref, lk1_ref, lq2_ref, lk2_ref, gn_ref, o_ref,
                      vext_ref, qz_ref, qnz_ref, sa_ref, sb_ref, sc_ref, sh_ref, m_ref, acc_ref,
                      *, tq, nq, lam_init):
    i = pl.program_id(2)
    dv = 2 * DIFF_DH
    tk = tq // 2

    def stack(src_ref, dst_ref):
        q = src_ref[...]
        map1 = _is_map1_lane(lax.broadcasted_iota(jnp.int32, q.shape, 1))
        zero = jnp.zeros_like(q)
        q1 = jnp.where(map1, q, zero)
        q2 = jnp.where(map1, zero, q)
        for half in range(2):
            rows = slice(half * tk, (half + 1) * tk)
            dst_ref[2 * half * tk:(2 * half + 1) * tk, :] = q1[rows]
            dst_ref[(2 * half + 1) * tk:(2 * half + 2) * tk, :] = q2[rows]

    def scores(j, s_ref, qsrc_ref, r0=0):
        off = pl.multiple_of(j * tk, tk)
        s_ref[...] = _dot_nt(qsrc_ref[r0:, :], k_ref[pl.ds(off, tk), :])

    def consume(j, s_ref, diag=None, r0=0):
        off = pl.multiple_of(j * tk, tk)
        sc = s_ref[...]
        if diag is not None:
            rr = lax.broadcasted_iota(jnp.int32, sc.shape, 0)
            cc = lax.broadcasted_iota(jnp.int32, sc.shape, 1)
            keep = (rr & (tk - 1)) >= cc
            if diag == "D1":
                keep = keep | (rr >= tq)
            sc = jnp.where(keep, sc, NEG_BIG)
        rows = slice(r0, 2 * tq)
        m_prev = m_ref[rows, :]
        m_new = jnp.maximum(m_prev, jnp.max(sc, axis=-1, keepdims=True))
        p = jnp.concatenate([jnp.exp2(sc[:, c0:c0 + LANES] - m_new) for c0 in range(0, tk, LANES)],
                            axis=1)
        alpha = jnp.exp2(m_prev - m_new)
        pv = _dot(p.astype(BF16), vext_ref[pl.ds(off, tk), :])
        for c0 in range(0, 2 * dv, LANES):
            acc_ref[rows, c0:c0 + LANES] = alpha * acc_ref[rows, c0:c0 + LANES] + pv[:, c0:c0 + LANES]
        m_ref[rows, :] = m_new

    def next_d1():
        nxt = jnp.minimum(i + 1, nq - 1)
        stack(qn_ref, qnz_ref)
        scores(2 * nxt, sc_ref, qnz_ref)

    stack(q_ref, qz_ref)
    m_ref[...] = jnp.full(m_ref.shape, NEG_BIG, F32)
    acc_ref[...] = jnp.zeros(acc_ref.shape, F32)

    @pl.when(i == 0)
    def _():
        vext_ref[:, :dv] = v_ref[...]
        vext_ref[:, dv:] = jnp.ones((vext_ref.shape[0], dv), BF16)
        scores(0, sc_ref, qz_ref)

    scores(2 * i + 1, sh_ref, qz_ref, r0=tq)
    consume(2 * i, sc_ref, diag="D1")

    @pl.when(i == 0)
    def _():
        next_d1()
        consume(1, sh_ref, diag="D2", r0=tq)

    @pl.when(i > 0)
    def _():
        scores(0, sa_ref, qz_ref)
        consume(2 * i + 1, sh_ref, diag="D2", r0=tq)

        def pair(p, carry):
            j = 2 * p
            scores(j + 1, sb_ref, qz_ref)
            consume(j, sa_ref)
            scores(j + 2, sa_ref, qz_ref)
            consume(j + 1, sb_ref)
            return carry

        lax.fori_loop(0, i - 1, pair, 0)
        scores(2 * i - 1, sb_ref, qz_ref)
        consume(2 * i - 2, sa_ref)
        next_d1()
        consume(2 * i - 1, sb_ref)

    lam = (jnp.exp(jnp.sum(lq1_ref[...] * lk1_ref[...], axis=-1, keepdims=True))
           - jnp.exp(jnp.sum(lq2_ref[...] * lk2_ref[...], axis=-1, keepdims=True))
           + lam_init)
    for half in range(2):
        a1 = acc_ref[2 * half * tk:(2 * half + 1) * tk, :]
        a2 = acc_ref[(2 * half + 1) * tk:(2 * half + 2) * tk, :]
        o = a1[:, :dv] / a1[:, dv:] - lam * (a2[:, :dv] / a2[:, dv:])
        o = _rms(o, gn_ref[...]) * (1.0 - lam_init)
        o_ref[half * tk:(half + 1) * tk, :] = o.astype(o_ref.dtype)


def _diff_attn(rope, plain, lq1, lk1, lq2, lk2, diff_norm_g, lam_init, B, S, tq=1024):
    T = B * S
    nq = S // tq
    dv = 2 * DIFF_DH
    lam_spec = pl.BlockSpec((1, DIFF_DH), lambda b, h, i: (0, 0))
    score_buf = pltpu.VMEM((2 * tq, tq // 2), F32)
    stacked_q = pltpu.VMEM((2 * tq, LANES), BF16)
    return pl.pallas_call(
        functools.partial(_diff_attn_kernel, tq=tq, nq=nq, lam_init=lam_init),
        grid=(B, DIFF_HEADS, nq),
        in_specs=[pl.BlockSpec((tq, LANES), lambda b, h, i: (b * nq + i, R_Q + h)),
                  pl.BlockSpec((tq, LANES),
                               lambda b, h, i: (b * nq + jnp.minimum(i + 1, nq - 1), R_Q + h)),
                  pl.BlockSpec((S, LANES), lambda b, h, i: (b, R_K + h)),
                  pl.BlockSpec((S, LANES), lambda b, h, i: (b, P_VB + h)),
                  lam_spec, lam_spec, lam_spec, lam_spec,
                  pl.BlockSpec((1, dv), lambda b, h, i: (0, 0))],
        out_specs=pl.BlockSpec((tq, LANES), lambda b, h, i: (b * nq + i, h)),
        out_shape=jax.ShapeDtypeStruct((T, DIFF_HEADS * dv), BF16),
        scratch_shapes=[pltpu.VMEM((S, 2 * dv), BF16),
                        stacked_q, stacked_q,
                        score_buf, score_buf, score_buf,
                        pltpu.VMEM((tq, tq // 2), F32),
                        pltpu.VMEM((2 * tq, LANES), F32),
                        pltpu.VMEM((2 * tq, 2 * dv), F32)],
        compiler_params=pltpu.CompilerParams(
            dimension_semantics=("parallel", "parallel", "arbitrary"),
            vmem_limit_bytes=VMEM_LIMIT),
        name="diff_attn",
    )(rope, rope, rope, plain, lq1, lk1, lq2, lk2, diff_norm_g)


def _merge_kernel(oa_ref, ob_ref, ga_ref, gb_ref, x_ref, woa_ref, wob_ref, wo_ref, x1_ref):
    ya = _dot(oa_ref[...], woa_ref[...])
    yb = _dot(ob_ref[...], wob_ref[...])
    merged = _sigmoid(ga_ref[...].astype(F32)) * ya + _sigmoid(gb_ref[...].astype(F32)) * yb
    x1_ref[...] = x_ref[...] + _dot(merged.astype(BF16), wo_ref[...])


def _merge(o_a, o_b, plain, x2d, w_out_a, w_out_b, w_o, tm=512):
    T, D = x2d.shape
    tokd = pl.BlockSpec((tm, D), lambda i: (i, 0))
    wspec = pl.BlockSpec((D, D), lambda i: (0, 0))
    return pl.pallas_call(
        _merge_kernel,
        grid=(T // tm,),
        in_specs=[tokd, tokd,
                  pl.BlockSpec((tm, D), lambda i: (i, P_GA * LANES // D)),
                  pl.BlockSpec((tm, D), lambda i: (i, P_GB * LANES // D)),
                  tokd, wspec, wspec, wspec],
        out_specs=tokd,
        out_shape=jax.ShapeDtypeStruct((T, D), F32),
        compiler_params=pltpu.CompilerParams(
            dimension_semantics=("parallel",), vmem_limit_bytes=VMEM_LIMIT),
        name="merge",
    )(o_a, o_b, plain, plain, x2d, w_out_a, w_out_b, w_o)


def _mem_kv_kernel(m_ref, g_ref, w_ref, o_ref):
    o_ref[...] = _dot(_rms(m_ref[...], g_ref[...]).astype(BF16), w_ref[...]).astype(o_ref.dtype)


def _mem_kv(mem2d, g_mem, w_ckv):
    R, D = mem2d.shape
    N = w_ckv.shape[1]
    return pl.pallas_call(
        _mem_kv_kernel,
        grid=(1,),
        in_specs=[pl.BlockSpec((R, D), lambda i: (0, 0)),
                  pl.BlockSpec((1, D), lambda i: (0, 0)),
                  pl.BlockSpec((D, N), lambda i: (0, 0))],
        out_specs=pl.BlockSpec((R, N), lambda i: (0, 0)),
        out_shape=jax.ShapeDtypeStruct((R, N), BF16),
        compiler_params=pltpu.CompilerParams(vmem_limit_bytes=VMEM_LIMIT),
        name="mem_kv",
    )(mem2d, g_mem, w_ckv)


def _cross_kernel(x_ref, g_ref, wq_ref, kv_ref, wo_ref, o_ref):
    x = x_ref[...]
    hx = _rms(x, g_ref[...]).astype(BF16)
    qc = (_dot(hx, wq_ref[...]) * (X_DH ** -0.5)).astype(BF16)
    xw = X_HEADS * X_DH
    outs = []
    for hh in range(X_HEADS):
        cs = slice(hh * X_DH, (hh + 1) * X_DH)
        sc = _dot_nt(qc[:, cs], kv_ref[:, cs])
        p = jnp.exp(sc - jnp.max(sc, axis=-1, keepdims=True))
        p = p / jnp.sum(p, axis=-1, keepdims=True)
        outs.append(_dot(p.astype(BF16), kv_ref[:, xw + hh * X_DH:xw + (hh + 1) * X_DH]))
    oc = jnp.concatenate(outs, axis=1).astype(BF16)
    o_ref[...] = x + _dot(oc, wo_ref[...])


def _cross(x1, g_cross, w_cq, kv, w_co, S, M, tm=512):
    T, D = x1.shape
    xw = X_HEADS * X_DH
    per_b = S // tm
    tokd = pl.BlockSpec((tm, D), lambda i: (i, 0))
    return pl.pallas_call(
        _cross_kernel,
        grid=(T // tm,),
        in_specs=[tokd,
                  pl.BlockSpec((1, D), lambda i: (0, 0)),
                  pl.BlockSpec((D, xw), lambda i: (0, 0)),
                  pl.BlockSpec((M, 2 * xw), lambda i: (i // per_b, 0)),
                  pl.BlockSpec((xw, D), lambda i: (0, 0))],
        out_specs=tokd,
        out_shape=jax.ShapeDtypeStruct((T, D), F32),
        compiler_params=pltpu.CompilerParams(
            dimension_semantics=("parallel",), vmem_limit_bytes=VMEM_LIMIT),
        name="cross",
    )(x1, g_cross, w_cq, kv, w_co)


def _ffn_kernel(x_ref, g_ref, wi_ref, wo_ref, gf_ref, o_ref, *, tf):
    x = x_ref[...]
    hb = _rms(x, g_ref[...]).astype(BF16)
    F = wo_ref.shape[0]
    acc = x
    for f in range(F // tf):
        act = _silu(_dot(hb, wi_ref[:, f * tf:(f + 1) * tf])) * _dot(hb, wi_ref[:, F + f * tf:F + (f + 1) * tf])
        acc = acc + _dot(act.astype(BF16), wo_ref[f * tf:(f + 1) * tf, :])
    o_ref[...] = _rms(acc, gf_ref[...])


def _ffn(x2, g_ffn, w_ffn_in, w_ffn_out, g_final, tm=512, tf=256):
    T, D = x2.shape
    F = w_ffn_out.shape[0]
    tokd = pl.BlockSpec((tm, D), lambda i: (i, 0))
    vec = pl.BlockSpec((1, D), lambda i: (0, 0))
    resident = pl.Buffered(1)
    return pl.pallas_call(
        functools.partial(_ffn_kernel, tf=tf),
        grid=(T // tm,),
        in_specs=[tokd, vec,
                  pl.BlockSpec((D, 2 * F), lambda i: (0, 0), pipeline_mode=resident),
                  pl.BlockSpec((F, D), lambda i: (0, 0), pipeline_mode=resident),
                  vec],
        out_specs=tokd,
        out_shape=jax.ShapeDtypeStruct((T, D), F32),
        compiler_params=pltpu.CompilerParams(
            dimension_semantics=("parallel",), vmem_limit_bytes=VMEM_LIMIT),
        name="ffn",
    )(x2, g_ffn, w_ffn_in, w_ffn_out, g_final)


def _pad_lanes(v):
    v = v.reshape(1, -1).astype(F32)
    return jnp.pad(v, ((0, 0), (0, LANES - v.shape[1])))


def kernel(x, mem, positions, g_mix, w_in, conv_w, a_log, dt_bias, gdn_norm_g, lambda_q1, lambda_k1, lambda_q2, lambda_k2, diff_norm_g, w_branch_gate, w_out_a, w_out_b, w_o, g_cross, g_mem, w_cq, w_ckv, w_co, g_ffn, w_ffn_in, w_ffn_out, g_final):
    B, S, D = x.shape
    M = mem.shape[1]
    depth = w_in.shape[0]
    assert depth == 1, "the final rmsnorm is fused into the (single) layer's ffn call"
    qkvz = 4 * GDN_HEADS * GDN_D
    xs = x.reshape(B * S, D)
    tables = _rope_tables(positions)
    kv = None
    for l in range(depth):
        ab_end = qkvz + 2 * GDN_HEADS
        qk_end = ab_end + 2 * DIFF_HEADS * 2 * DIFF_DH
        w_plain = jnp.concatenate(
            [w_in[l][:, :qkvz], w_in[l][:, qk_end:], w_branch_gate[l]], axis=1).astype(BF16)
        w_ab = jnp.pad(w_in[l][:, qkvz:ab_end], ((0, 0), (0, LANES - 2 * GDN_HEADS))).astype(BF16)
        plain, ab, h = _in_proj(xs, g_mix[l].reshape(1, D), w_plain, w_ab)
        src = jnp.asarray([hh * LANES + c for hh in range(2 * DIFF_HEADS) for c in _head_lane_source()],
                          dtype=jnp.int32)
        w_qk = jnp.take(w_in[l][:, ab_end:qk_end], src, axis=1).astype(BF16)
        rope = _rope_proj(h, w_qk, tables)

        u, w, qd, kd, attn, egl = _gdn_prep(plain, ab, conv_w[l], _pad_lanes(a_log[l]),
                                            _pad_lanes(dt_bias[l]), B, S)
        o_a = _gdn_scan(u, w, qd, kd, attn, egl, plain, gdn_norm_g[l].reshape(1, GDN_D), B, S)

        lam_init = 0.8 - 0.6 * math.exp(-0.3 * l)
        o_b = _diff_attn(rope, plain, lambda_q1[l].reshape(1, -1), lambda_k1[l].reshape(1, -1),
                         lambda_q2[l].reshape(1, -1), lambda_k2[l].reshape(1, -1),
                         diff_norm_g[l].reshape(1, -1), lam_init, B, S)

        xs = _merge(o_a, o_b, plain, xs,w_out_a[l].astype(BF16), w_out_b[l].astype(BF16),
                    w_o[l].astype(BF16))

        kv = _mem_kv(mem.reshape(B * M, D), g_mem[l].reshape(1, D), w_ckv[l].astype(BF16))
        xs = _cross(xs, g_cross[l].reshape(1, D), w_cq[l].astype(BF16), kv, w_co[l].astype(BF16), S, M)

        xs = _ffn(xs, g_ffn[l].reshape(1, D), w_ffn_in[l].astype(BF16), w_ffn_out[l].astype(BF16),
                  g_final.reshape(1, D))
    return xs.reshape(B, S, D)
```

```python
import functools
import math

import jax
import jax.numpy as jnp
from jax import lax
from jax.experimental import pallas as pl
from jax.experimental.pallas import tpu as pltpu

F32 = jnp.float32
BF16 = jnp.bfloat16

EPS = 1e-6
LANES = 128
GDN_HEADS = 8
GDN_D = 128
CONV_WIDTH = 4
CHUNK = 64
DIFF_HEADS = 8
DIFF_DH = 64
ROPE_DIM = DIFF_DH // 4
ROPE_THETA = 500000.0
X_HEADS = 4
X_DH = 128
NEG_BIG = -1e30
LOG2E = math.log2(math.e)

VMEM_LIMIT = 48 * 1024 * 1024

P_QA, P_KA, P_VA, P_Z, P_VB, P_GA, P_GB, P_END = 0, 8, 16, 24, 32, 40, 48, 56
R_Q, R_K = 0, 8


def _dot(a, b):
    return jnp.dot(a, b, preferred_element_type=F32)


def _dot_nt(a, b):
    return lax.dot_general(a, b, (((1,), (1,)), ((), ())), preferred_element_type=F32)


def _mm(a, b):
    return _dot(a.astype(BF16), b.astype(BF16))


def _split3(a):
    hi = a.astype(BF16)
    r = a - hi.astype(F32)
    mid = r.astype(BF16)
    lo = (r - mid.astype(F32)).astype(BF16)
    return hi, mid, lo


def _dot_mask(mask_bf16, b):
    hi, mid, lo = _split3(b)
    return _dot(mask_bf16, hi) + (_dot(mask_bf16, mid) + _dot(mask_bf16, lo))


def _dot_rmask(b, mask_bf16):
    hi, mid, lo = _split3(b)
    return _dot(hi, mask_bf16) + (_dot(mid, mask_bf16) + _dot(lo, mask_bf16))


def _rms(x, g):
    ms = jnp.mean(x * x, axis=-1, keepdims=True)
    return x * lax.rsqrt(ms + EPS) * g


def _sigmoid(x):
    return 0.5 * jnp.tanh(0.5 * x) + 0.5


def _silu(x):
    half = 0.5 * x
    return half * jnp.tanh(half) + half


def _softplus(x):
    return jnp.maximum(x, 0.0) + jnp.log(1.0 + jnp.exp(-jnp.abs(x)))


ROPE_HALF = ROPE_DIM // 2
HALF_LANES = LANES // 2


def _head_lane_source():
    src = [0] * LANES
    for m in range(2):
        for d in range(DIFF_DH):
            if d < ROPE_HALF:
                lane = m * ROPE_HALF + d
            elif d < ROPE_DIM:
                lane = HALF_LANES + m * ROPE_HALF + (d - ROPE_HALF)
            else:
                lane = (ROPE_DIM if m == 0 else HALF_LANES + ROPE_DIM) + (d - ROPE_DIM)
            src[lane] = m * DIFF_DH + d
    return src


def _is_map1_lane(lane):
    return (lane < ROPE_HALF) | ((lane >= ROPE_DIM) & (lane < HALF_LANES + ROPE_HALF))


ROPE_GROUPS = LANES // ROPE_HALF


def _rope_table_kernel(pos_ref, inv_ref, c_ref, sg_ref):
    ang = pos_ref[...] * inv_ref[...]
    cos = jnp.cos(ang)
    sin = jnp.sin(ang)
    rows = ang.shape[0]
    lane = lax.broadcasted_iota(jnp.int32, ang.shape, 1)
    first = lane < ROPE_DIM
    second = (lane >= HALF_LANES) & (lane < HALF_LANES + ROPE_DIM)
    freq = lane & (ROPE_HALF - 1)
    for a in range(ROPE_GROUPS):
        idx = freq + a * ROPE_HALF
        cos_a = jnp.take_along_axis(cos, idx, axis=1)
        sin_a = jnp.take_along_axis(sin, idx, axis=1)
        c_ref[a * rows:(a + 1) * rows, :] = jnp.where(first | second, cos_a, 1.0)
        sg_ref[a * rows:(a + 1) * rows, :] = jnp.where(first, -sin_a, jnp.where(second, sin_a, 0.0))


def _rope_tables(positions, tm=1024):
    T = positions.size
    rows = tm // ROPE_GROUPS
    pos = positions.astype(F32).reshape(T // tm, ROPE_GROUPS, rows).transpose(0, 2, 1)
    pos = jnp.repeat(pos, ROPE_HALF, axis=2).reshape(T // ROPE_GROUPS, LANES)
    inv_freq = ROPE_THETA ** (-jnp.arange(0, ROPE_DIM, 2, dtype=F32) / ROPE_DIM)
    inv = jnp.tile(inv_freq, ROPE_GROUPS).reshape(1, LANES)
    out = jax.ShapeDtypeStruct((T, LANES), F32)
    return pl.pallas_call(
        _rope_table_kernel,
        grid=(T // tm,),
        in_specs=[pl.BlockSpec((rows, LANES), lambda i: (i, 0)),
                  pl.BlockSpec((1, LANES), lambda i: (0, 0))],
        out_specs=[pl.BlockSpec((tm, LANES), lambda i: (i, 0))] * 2,
        out_shape=[out, out],
        name="rope_tables",
    )(pos, inv)


PROJ_SUB = 256


def _in_proj_kernel(x_ref, g_ref, w_ref, wab_ref, out_ref, ab_ref, h_ref):
    @pl.when(pl.program_id(1) == 0)
    def _():
        hb = _rms(x_ref[...], g_ref[...]).astype(BF16)
        h_ref[...] = hb
        ab_ref[...] = _dot(hb, wab_ref[...])

    hb = h_ref[...]
    for c0 in range(0, out_ref.shape[1], PROJ_SUB):
        cs = slice(c0, c0 + PROJ_SUB)
        out_ref[:, cs] = _dot(hb, w_ref[:, cs]).astype(out_ref.dtype)


def _in_proj(x2d, g_mix, w_plain, w_ab, tm=1024, tn=1792):
    T, D = x2d.shape
    N = w_plain.shape[1]
    tm = min(tm, T)
    return pl.pallas_call(
        _in_proj_kernel,
        grid=(T // tm, N // tn),
        in_specs=[pl.BlockSpec((tm, D), lambda i, j: (i, 0)),
                  pl.BlockSpec((1, D), lambda i, j: (0, 0)),
                  pl.BlockSpec((D, tn), lambda i, j: (0, j)),
                  pl.BlockSpec((D, LANES), lambda i, j: (0, 0))],
        out_specs=[pl.BlockSpec((tm, tn), lambda i, j: (i, j)),
                   pl.BlockSpec((tm, LANES), lambda i, j: (i, 0)),
                   pl.BlockSpec((tm, D), lambda i, j: (i, 0))],
        out_shape=[jax.ShapeDtypeStruct((T, N), BF16),
                   jax.ShapeDtypeStruct((T, LANES), F32),
                   jax.ShapeDtypeStruct((T, D), BF16)],
        compiler_params=pltpu.CompilerParams(
            dimension_semantics=("parallel", "arbitrary"), vmem_limit_bytes=VMEM_LIMIT),
        name="in_proj",
    )(x2d, g_mix, w_plain, w_ab)


def _rope_proj_kernel(h_ref, w_ref, c_ref, sg_ref, out_ref):
    scale = jnp.where(pl.program_id(1) == 0, DIFF_DH ** -0.5 * LOG2E, 1.0)
    c = c_ref[...] * scale
    sg = sg_ref[...] * scale
    hb = h_ref[...]
    for c0 in range(0, out_ref.shape[1], PROJ_SUB):
        acc = _dot(hb, w_ref[:, c0:c0 + PROJ_SUB])
        for l0 in range(0, PROJ_SUB, LANES):
            a = acc[:, l0:l0 + LANES]
            y = a * c + pltpu.roll(a, HALF_LANES, 1) * sg
            out_ref[:, c0 + l0:c0 + l0 + LANES] = y.astype(out_ref.dtype)


def _rope_proj(h, w_qk, tables, tm=1024):
    T, D = h.shape
    N = w_qk.shape[1]
    tn = N // 2
    tm = min(tm, T)
    return pl.pallas_call(
        _rope_proj_kernel,
        grid=(T // tm, 2),
        in_specs=[pl.BlockSpec((tm, D), lambda i, j: (i, 0)),
                  pl.BlockSpec((D, tn), lambda i, j: (0, j)),
                  pl.BlockSpec((tm, LANES), lambda i, j: (i, 0)),
                  pl.BlockSpec((tm, LANES), lambda i, j: (i, 0))],
        out_specs=pl.BlockSpec((tm, tn), lambda i, j: (i, j)),
        out_shape=jax.ShapeDtypeStruct((T, N), BF16),
        compiler_params=pltpu.CompilerParams(
            dimension_semantics=("parallel", "parallel"), vmem_limit_bytes=VMEM_LIMIT),
        name="rope_proj",
    )(h, w_qk, *tables)


GDN_TS = 4 * CHUNK
GDN_HEAD_GROUP = 4
SOLVE_BLOCK = 16
CHUNK_SHIFT = CHUNK.bit_length() - 1
SOLVE_SHIFT = SOLVE_BLOCK.bit_length() - 1
assert (1 << CHUNK_SHIFT) == CHUNK and (1 << SOLVE_SHIFT) == SOLVE_BLOCK and CHUNK // SOLVE_BLOCK == 4


def _gdn_prep_kernel(qkv_ref, ab_ref, cw_ref, alog_ref, dtb_ref,
                     u_ref, w_ref, qd_ref, kd_ref, attn_ref, egl_ref, halo_ref):
    TS, H, D = GDN_TS, GDN_HEADS, GDN_D

    @pl.when(pl.program_id(1) == 0)
    def _():
        halo_ref[...] = jnp.zeros(halo_ref.shape, F32)

    def conv_silu(c0):
        cs = slice(c0, c0 + D)
        ext = jnp.concatenate([halo_ref[:, cs], qkv_ref[:, cs].astype(F32)], axis=0)
        z = ext * cw_ref[0:1, cs]
        for t in range(1, CONV_WIDTH):
            z = pltpu.roll(z, 1, 0) + ext * cw_ref[t:t + 1, cs]
        return _silu(z[8:])

    def l2_normalised(x):
        return x * lax.rsqrt(jnp.sum(x * x, axis=-1, keepdims=True) + EPS)

    ri = lax.broadcasted_iota(jnp.int32, (TS, TS), 0)
    ci = lax.broadcasted_iota(jnp.int32, (TS, TS), 1)
    same = (ri >> CHUNK_SHIFT) == (ci >> CHUNK_SHIFT)
    incl = same & (ri >= ci)
    strict = same & (ri > ci)
    blk = (ri >> SOLVE_SHIFT) == (ci >> SOLVE_SHIFT)
    eye = jnp.where(ri == ci, 1.0, 0.0)

    ab = ab_ref[...]
    g_all = -jnp.exp(alog_ref[...]) * _softplus(ab + dtb_ref[...])
    b_all = _sigmoid(ab)
    gc_all = _dot_mask(jnp.where(incl, 1.0, 0.0).astype(BF16), g_all)
    gl_all = _dot_mask(jnp.where(same, 1.0, 0.0).astype(BF16), g_all)
    gc_t = _dot_rmask(g_all.T, jnp.where(same & (ri <= ci), 1.0, 0.0).astype(BF16))

    r8 = lax.broadcasted_iota(jnp.int32, (8, TS), 0)
    c8 = lax.broadcasted_iota(jnp.int32, (8, TS), 1)
    sel = jnp.where(c8 == r8 * CHUNK, 1.0, 0.0).astype(BF16)
    gl8 = _dot_mask(sel, gl_all)

    def head_group(hs):
        ld, lo, x, rhs = {}, {}, {}, {}
        for h in hs:
            cs = slice(h * D, (h + 1) * D)
            q = l2_normalised(conv_silu(h * D)) * (D ** -0.5)
            k = l2_normalised(conv_silu((H + h) * D))
            v = conv_silu((2 * H + h) * D)
            gc_col = gc_all[:, h:h + 1]
            gl_col = gl_all[:, h:h + 1]
            beta = b_all[:, H + h:H + h + 1]
            decay = jnp.exp(jnp.minimum(gc_col - gc_t[h:h + 1, :], 0.0))
            eg = jnp.exp(gc_col)
            kb = k * beta
            kbf = k.astype(BF16)
            lmat = jnp.where(strict, _dot_nt(kb.astype(BF16), kbf) * decay, 0.0)
            attn = jnp.where(incl, _dot_nt(q.astype(BF16), kbf) * decay, 0.0)
            attn_ref[:, h * TS:(h + 1) * TS] = attn.astype(attn_ref.dtype)
            qd_ref[:, cs] = (q * eg).astype(qd_ref.dtype)
            kd_ref[:, cs] = (k * jnp.exp(gl_col - gc_col)).astype(kd_ref.dtype)
            egl_ref[:, cs] = jnp.exp(jnp.broadcast_to(gl8[:, h:h + 1], (8, D)))
            ldh = jnp.where(blk, lmat, 0.0)
            ld[h] = ldh.astype(BF16)
            lo[h] = (lmat - ldh).astype(BF16)
            x[h] = eye - ldh
            rhs[h] = jnp.concatenate([v * beta, kb * eg], axis=1).astype(BF16)

        p = {h: _dot(ld[h], ld[h]) for h in hs}
        for _ in range(2):
            pb = {h: p[h].astype(BF16) for h in hs}
            x = {h: x[h] + _mm(x[h], pb[h]) for h in hs}
            p = {h: _dot(pb[h], pb[h]) for h in hs}
        td = {h: x[h] + _mm(x[h], p[h]) for h in hs}
        tdb = {h: td[h].astype(BF16) for h in hs}
        n = {h: _dot(tdb[h], lo[h]).astype(BF16) for h in hs}
        n2 = {h: _dot(n[h], n[h]) for h in hs}
        m1 = {h: td[h] + _mm(n2[h], tdb[h]) for h in hs}
        tinv = {h: m1[h] - _mm(n[h], m1[h]) for h in hs}
        for h in hs:
            cs = slice(h * D, (h + 1) * D)
            uw = _mm(tinv[h], rhs[h])
            u_ref[:, cs] = uw[:, :D].astype(u_ref.dtype)
            w_ref[:, cs] = uw[:, D:].astype(w_ref.dtype)

    for g0 in range(0, H, GDN_HEAD_GROUP):
        head_group(range(g0, g0 + GDN_HEAD_GROUP))
    halo_ref[...] = qkv_ref[TS - 8:TS, :].astype(F32)


def _gdn_prep(plain, ab, conv_w, a_log, dt_bias, B, S):
    T = B * S
    TS = GDN_TS
    ns = S // TS
    HD = GDN_HEADS * GDN_D
    row = lambda b, s: b * ns + s
    vec = pl.BlockSpec((1, LANES), lambda b, s: (0, 0))
    tokw = pl.BlockSpec((TS, HD), lambda b, s: (row(b, s), 0))
    act = jax.ShapeDtypeStruct((T, HD), BF16)
    return pl.pallas_call(
        _gdn_prep_kernel,
        grid=(B, ns),
        in_specs=[pl.BlockSpec((TS, 3 * HD), lambda b, s: (row(b, s), 0)),
                  pl.BlockSpec((TS, LANES), lambda b, s: (row(b, s), 0)),
                  pl.BlockSpec((CONV_WIDTH, 3 * HD), lambda b, s: (0, 0)),
                  vec, vec],
        out_specs=[tokw, tokw, tokw, tokw,
                   pl.BlockSpec((TS, GDN_HEADS * TS), lambda b, s: (row(b, s), 0)),
                   pl.BlockSpec((8, HD), lambda b, s: (row(b, s), 0))],
        out_shape=[act, act, act, act,
                   jax.ShapeDtypeStruct((T, GDN_HEADS * TS), BF16),
                   jax.ShapeDtypeStruct((B * ns * 8, HD), F32)],
        scratch_shapes=[pltpu.VMEM((8, 3 * HD), F32)],
        compiler_params=pltpu.CompilerParams(
            dimension_semantics=("parallel", "arbitrary"), vmem_limit_bytes=VMEM_LIMIT),
        name="gdn_prep",
    )(plain, ab, conv_w, a_log, dt_bias)


def _gdn_scan_kernel(u_ref, w_ref, qd_ref, kd_ref, attn_ref, egl_ref, z_ref, gn_ref, o_ref,
                     state_ref, vnew_ref, oq_ref):
    TS = GDN_TS
    D = GDN_D
    nb = u_ref.shape[0]
    s = pl.program_id(1)

    @pl.when(s == 0)
    def _():
        state_ref[...] = jnp.zeros(state_ref.shape, F32)

    for c in range(TS // CHUNK):
        r = slice(c * CHUNK, (c + 1) * CHUNK)
        for b in range(nb):
            for h in range(GDN_HEADS):
                cs = slice(h * D, (h + 1) * D)
                st = state_ref[b, h]
                wq = jnp.concatenate([w_ref[b, r, cs], qd_ref[b, r, cs]], axis=0)
                ws = _dot(wq, st.astype(BF16))
                v_new = (u_ref[b, r, cs].astype(F32) - ws[:CHUNK]).astype(BF16)
                vnew_ref[b, h, r, :] = v_new
                oq_ref[b, r, cs] = ws[CHUNK:]
                ktv = lax.dot_general(kd_ref[b, r, cs], v_new, (((0,), (0,)), ((), ())),
                                      preferred_element_type=F32)
                state_ref[b, h] = st * egl_ref[b, c:c + 1, cs] + ktv

    for b in range(nb):
        for h in range(GDN_HEADS):
            cs = slice(h * D, (h + 1) * D)
            o = oq_ref[b, :, cs] + _dot(attn_ref[b, :, h * TS:(h + 1) * TS], vnew_ref[b, h])
            o = _rms(o, gn_ref[...]) * _silu(z_ref[b, :, cs].astype(F32))
            o_ref[b, :, cs] = o.astype(o_ref.dtype)


def _gdn_scan(u, w, qd, kd, attn, egl, plain, gdn_norm_g, B, S, nb=4):
    TS = GDN_TS
    ns = S // TS
    HD = GDN_HEADS * GDN_D
    nb = min(nb, B)
    assert B % nb == 0
    seq = lambda a: a.reshape(B, S, a.shape[-1])
    tokw = pl.BlockSpec((nb, TS, HD), lambda b, s: (b, s, 0))
    out = pl.pallas_call(
        _gdn_scan_kernel,
        grid=(B // nb, ns),
        in_specs=[tokw, tokw, tokw, tokw,
                  pl.BlockSpec((nb, TS, GDN_HEADS * TS), lambda b, s: (b, s, 0)),
                  pl.BlockSpec((nb, 8, HD), lambda b, s: (b, s, 0)),
                  pl.BlockSpec((nb, TS, HD), lambda b, s: (b, s, P_Z * LANES // HD)),
                  pl.BlockSpec((1, GDN_D), lambda b, s: (0, 0))],
        out_specs=tokw,
        out_shape=jax.ShapeDtypeStruct((B, S, HD), BF16),
        scratch_shapes=[pltpu.VMEM((nb, GDN_HEADS, GDN_D, GDN_D), F32),
                        pltpu.VMEM((nb, GDN_HEADS, TS, GDN_D), BF16),
                        pltpu.VMEM((nb, TS, HD), F32)],
        compiler_params=pltpu.CompilerParams(
            dimension_semantics=("parallel", "arbitrary"), vmem_limit_bytes=VMEM_LIMIT),
        name="gdn_scan",
    )(seq(u), seq(w), seq(qd), seq(kd), seq(attn), egl.reshape(B, ns * 8, HD), seq(plain), gdn_norm_g)
    return out.reshape(B * S, HD)


def _diff_attn_kernel(q_ref, qn_ref, k_ref, v_ref, lq1_ref, lk1_ref, lq2_ref, lk2_ref, gn_ref, o_ref,
                      vext_ref, qz_ref, qnz_ref, sa_ref, sb_ref, sc_ref, sh_ref, m_ref, acc_ref,
                      *, tq, nq, lam_init):
    i = pl.program_id(2)
    dv = 2 * DIFF_DH
    tk = tq // 2

    def stack(src_ref, dst_ref):
        q = src_ref[...]
        map1 = _is_map1_lane(lax.broadcasted_iota(jnp.int32, q.shape, 1))
        zero = jnp.zeros_like(q)
        q1 = jnp.where(map1, q, zero)
        q2 = jnp.where(map1, zero, q)
        for half in range(2):
            rows = slice(half * tk, (half + 1) * tk)
            dst_ref[2 * half * tk:(2 * half + 1) * tk, :] = q1[rows]
            dst_ref[(2 * half + 1) * tk:(2 * half + 2) * tk, :] = q2[rows]

    def scores(j, s_ref, qsrc_ref, r0=0):
        off = pl.multiple_of(j * tk, tk)
        s_ref[...] = _dot_nt(qsrc_ref[r0:, :], k_ref[pl.ds(off, tk), :])

    def consume(j, s_ref, diag=None, r0=0):
        off = pl.multiple_of(j * tk, tk)
        sc = s_ref[...]
        if diag is not None:
            rr = lax.broadcasted_iota(jnp.int32, sc.shape, 0)
            cc = lax.broadcasted_iota(jnp.int32, sc.shape, 1)
            keep = (rr & (tk - 1)) >= cc
            if diag == "D1":
                keep = keep | (rr >= tq)
            sc = jnp.where(keep, sc, NEG_BIG)
        rows = slice(r0, 2 * tq)
        m_prev = m_ref[rows, :]
        m_new = jnp.maximum(m_prev, jnp.max(sc, axis=-1, keepdims=True))
        p = jnp.concatenate([jnp.exp2(sc[:, c0:c0 + LANES] - m_new) for c0 in range(0, tk, LANES)],
                            axis=1)
        alpha = jnp.exp2(m_prev - m_new)
        pv = _dot(p.astype(BF16), vext_ref[pl.ds(off, tk), :])
        for c0 in range(0, 2 * dv, LANES):
            acc_ref[rows, c0:c0 + LANES] = alpha * acc_ref[rows, c0:c0 + LANES] + pv[:, c0:c0 + LANES]
        m_ref[rows, :] = m_new

    def next_d1():
        nxt = jnp.minimum(i + 1, nq - 1)
        stack(qn_ref, qnz_ref)
        scores(2 * nxt, sc_ref, qnz_ref)

    stack(q_ref, qz_ref)
    m_ref[...] = jnp.full(m_ref.shape, NEG_BIG, F32)
    acc_ref[...] = jnp.zeros(acc_ref.shape, F32)

    @pl.when(i == 0)
    def _():
        vext_ref[:, :dv] = v_ref[...]
        vext_ref[:, dv:] = jnp.ones((vext_ref.shape[0], dv), BF16)
        scores(0, sc_ref, qz_ref)

    scores(2 * i + 1, sh_ref, qz_ref, r0=tq)
    consume(2 * i, sc_ref, diag="D1")

    @pl.when(i == 0)
    def _():
        next_d1()
        consume(1, sh_ref, diag="D2", r0=tq)

    @pl.when(i > 0)
    def _():
        scores(0, sa_ref, qz_ref)
        consume(2 * i + 1, sh_ref, diag="D2", r0=tq)

        def pair(p, carry):
            j = 2 * p
            scores(j + 1, sb_ref, qz_ref)
            consume(j, sa_ref)
            scores(j + 2, sa_ref, qz_ref)
            consume(j + 1, sb_ref)
            return carry

        lax.fori_loop(0, i - 1, pair, 0)
        scores(2 * i - 1, sb_ref, qz_ref)
        consume(2 * i - 2, sa_ref)
        next_d1()
        consume(2 * i - 1, sb_ref)

    lam = (jnp.exp(jnp.sum(lq1_ref[...] * lk1_ref[...], axis=-1, keepdims=True))
           - jnp.exp(jnp.sum(lq2_ref[...] * lk2_ref[...], axis=-1, keepdims=True))
           + lam_init)
    for half in range(2):
        a1 = acc_ref[2 * half * tk:(2 * half + 1) * tk, :]
        a2 = acc_ref[(2 * half + 1) * tk:(2 * half + 2) * tk, :]
        o = a1[:, :dv] / a1[:, dv:] - lam * (a2[:, :dv] / a2[:, dv:])
        o = _rms(o, gn_ref[...]) * (1.0 - lam_init)
        o_ref[half * tk:(half + 1) * tk, :] = o.astype(o_ref.dtype)


def _diff_attn(rope, plain, lq1, lk1, lq2, lk2, diff_norm_g, lam_init, B, S, tq=1024):
    T = B * S
    nq = S // tq
    dv = 2 * DIFF_DH
    lam_spec = pl.BlockSpec((1, DIFF_DH), lambda b, h, i: (0, 0))
    score_buf = pltpu.VMEM((2 * tq, tq // 2), F32)
    stacked_q = pltpu.VMEM((2 * tq, LANES), BF16)
    return pl.pallas_call(
        functools.partial(_diff_attn_kernel, tq=tq, nq=nq, lam_init=lam_init),
        grid=(B, DIFF_HEADS, nq),
        in_specs=[pl.BlockSpec((tq, LANES), lambda b, h, i: (b * nq + i, R_Q + h)),
                  pl.BlockSpec((tq, LANES),
                               lambda b, h, i: (b * nq + jnp.minimum(i + 1, nq - 1), R_Q + h)),
                  pl.BlockSpec((S, LANES), lambda b, h, i: (b, R_K + h)),
                  pl.BlockSpec((S, LANES), lambda b, h, i: (b, P_VB + h)),
                  lam_spec, lam_spec, lam_spec, lam_spec,
                  pl.BlockSpec((1, dv), lambda b, h, i: (0, 0))],
        out_specs=pl.BlockSpec((tq, LANES), lambda b, h, i: (b * nq + i, h)),
        out_shape=jax.ShapeDtypeStruct((T, DIFF_HEADS * dv), BF16),
        scratch_shapes=[pltpu.VMEM((S, 2 * dv), BF16),
                        stacked_q, stacked_q,
                        score_buf, score_buf, score_buf,
                        pltpu.VMEM((tq, tq // 2), F32),
                        pltpu.VMEM((2 * tq, LANES), F32),
                        pltpu.VMEM((2 * tq, 2 * dv), F32)],
        compiler_params=pltpu.CompilerParams(
            dimension_semantics=("parallel", "parallel", "arbitrary"),
            vmem_limit_bytes=VMEM_LIMIT),
        name="diff_attn",
    )(rope, rope, rope, plain, lq1, lk1, lq2, lk2, diff_norm_g)


def _merge_kernel(oa_ref, ob_ref, ga_ref, gb_ref, x_ref, woa_ref, wob_ref, wo_ref, x1_ref):
    ya = _dot(oa_ref[...], woa_ref[...])
    yb = _dot(ob_ref[...], wob_ref[...])
    merged = _sigmoid(ga_ref[...].astype(F32)) * ya + _sigmoid(gb_ref[...].astype(F32)) * yb
    x1_ref[...] = x_ref[...] + _dot(merged.astype(BF16), wo_ref[...])


def _merge(o_a, o_b, plain, x2d, w_out_a, w_out_b, w_o, tm=512):
    T, D = x2d.shape
    tokd = pl.BlockSpec((tm, D), lambda i: (i, 0))
    wspec = pl.BlockSpec((D, D), lambda i: (0, 0))
    return pl.pallas_call(
        _merge_kernel,
        grid=(T // tm,),
        in_specs=[tokd, tokd,
                  pl.BlockSpec((tm, D), lambda i: (i, P_GA * LANES // D)),
                  pl.BlockSpec((tm, D), lambda i: (i, P_GB * LANES // D)),
                  tokd, wspec, wspec, wspec],
        out_specs=tokd,
        out_shape=jax.ShapeDtypeStruct((T, D), F32),
        compiler_params=pltpu.CompilerParams(
            dimension_semantics=("parallel",), vmem_limit_bytes=VMEM_LIMIT),
        name="merge",
    )(o_a, o_b, plain, plain, x2d, w_out_a, w_out_b, w_o)


def _mem_kv_kernel(m_ref, g_ref, w_ref, o_ref):
    o_ref[...] = _dot(_rms(m_ref[...], g_ref[...]).astype(BF16), w_ref[...]).astype(o_ref.dtype)


def _mem_kv(mem2d, g_mem, w_ckv):
    R, D = mem2d.shape
    N = w_ckv.shape[1]
    return pl.pallas_call(
        _mem_kv_kernel,
        grid=(1,),
        in_specs=[pl.BlockSpec((R, D), lambda i: (0, 0)),
                  pl.BlockSpec((1, D), lambda i: (0, 0)),
                  pl.BlockSpec((D, N), lambda i: (0, 0))],
        out_specs=pl.BlockSpec((R, N), lambda i: (0, 0)),
        out_shape=jax.ShapeDtypeStruct((R, N), BF16),
        compiler_params=pltpu.CompilerParams(vmem_limit_bytes=VMEM_LIMIT),
        name="mem_kv",
    )(mem2d, g_mem, w_ckv)


def _cross_kernel(x_ref, g_ref, wq_ref, kv_ref, wo_ref, o_ref):
    x = x_ref[...]
    hx = _rms(x, g_ref[...]).astype(BF16)
    qc = (_dot(hx, wq_ref[...]) * (X_DH ** -0.5)).astype(BF16)
    xw = X_HEADS * X_DH
    outs = []
    for hh in range(X_HEADS):
        cs = slice(hh * X_DH, (hh + 1) * X_DH)
        sc = _dot_nt(qc[:, cs], kv_ref[:, cs])
        p = jnp.exp(sc - jnp.max(sc, axis=-1, keepdims=True))
        p = p / jnp.sum(p, axis=-1, keepdims=True)
        outs.append(_dot(p.astype(BF16), kv_ref[:, xw + hh * X_DH:xw + (hh + 1) * X_DH]))
    oc = jnp.concatenate(outs, axis=1).astype(BF16)
    o_ref[...] = x + _dot(oc, wo_ref[...])


def _cross(x1, g_cross, w_cq, kv, w_co, S, M, tm=512):
    T, D = x1.shape
    xw = X_HEADS * X_DH
    per_b = S // tm
    tokd = pl.BlockSpec((tm, D), lambda i: (i, 0))
    return pl.pallas_call(
        _cross_kernel,
        grid=(T // tm,),
        in_specs=[tokd,
                  pl.BlockSpec((1, D), lambda i: (0, 0)),
                  pl.BlockSpec((D, xw), lambda i: (0, 0)),
                  pl.BlockSpec((M, 2 * xw), lambda i: (i // per_b, 0)),
                  pl.BlockSpec((xw, D), lambda i: (0, 0))],
        out_specs=tokd,
        out_shape=jax.ShapeDtypeStruct((T, D), F32),
        compiler_params=pltpu.CompilerParams(
            dimension_semantics=("parallel",), vmem_limit_bytes=VMEM_LIMIT),
        name="cross",
    )(x1, g_cross, w_cq, kv, w_co)


def _ffn_kernel(x_ref, g_ref, wi_ref, wo_ref, gf_ref, o_ref, *, tf):
    x = x_ref[...]
    hb = _rms(x, g_ref[...]).astype(BF16)
    F = wo_ref.shape[0]
    acc = x
    for f in range(F // tf):
        act = _silu(_dot(hb, wi_ref[:, f * tf:(f + 1) * tf])) * _dot(hb, wi_ref[:, F + f * tf:F + (f + 1) * tf])
        acc = acc + _dot(act.astype(BF16), wo_ref[f * tf:(f + 1) * tf, :])
    o_ref[...] = _rms(acc, gf_ref[...])


def _ffn(x2, g_ffn, w_ffn_in, w_ffn_out, g_final, tm=512, tf=256):
    T, D = x2.shape
    F = w_ffn_out.shape[0]
    tokd = pl.BlockSpec((tm, D), lambda i: (i, 0))
    vec = pl.BlockSpec((1, D), lambda i: (0, 0))
    resident = pl.Buffered(1)
    return pl.pallas_call(
        functools.partial(_ffn_kernel, tf=tf),
        grid=(T // tm,),
        in_specs=[tokd, vec,
                  pl.BlockSpec((D, 2 * F), lambda i: (0, 0), pipeline_mode=resident),
                  pl.BlockSpec((F, D), lambda i: (0, 0), pipeline_mode=resident),
                  vec],
        out_specs=tokd,
        out_shape=jax.ShapeDtypeStruct((T, D), F32),
        compiler_params=pltpu.CompilerParams(
            dimension_semantics=("parallel",), vmem_limit_bytes=VMEM_LIMIT),
        name="ffn",
    )(x2, g_ffn, w_ffn_in, w_ffn_out, g_final)


def _pad_lanes(v):
    v = v.reshape(1, -1).astype(F32)
    return jnp.pad(v, ((0, 0), (0, LANES - v.shape[1])))


def kernel(x, mem, positions, g_mix, w_in, conv_w, a_log, dt_bias, gdn_norm_g, lambda_q1, lambda_k1, lambda_q2, lambda_k2, diff_norm_g, w_branch_gate, w_out_a, w_out_b, w_o, g_cross, g_mem, w_cq, w_ckv, w_co, g_ffn, w_ffn_in, w_ffn_out, g_final):
    B, S, D = x.shape
    M = mem.shape[1]
    depth = w_in.shape[0]
    assert depth == 1, "the final rmsnorm is fused into the (single) layer's ffn call"
    qkvz = 4 * GDN_HEADS * GDN_D
    xs = x.reshape(B * S, D)
    tables = _rope_tables(positions)
    kv = None
    for l in range(depth):
        ab_end = qkvz + 2 * GDN_HEADS
        qk_end = ab_end + 2 * DIFF_HEADS * 2 * DIFF_DH
        w_plain = jnp.concatenate(
            [w_in[l][:, :qkvz].astype(BF16), w_in[l][:, qk_end:].astype(BF16),
             w_branch_gate[l].astype(BF16)], axis=1)
        w_ab = jnp.pad(w_in[l][:, qkvz:ab_end], ((0, 0), (0, LANES - 2 * GDN_HEADS))).astype(BF16)
        plain, ab, h = _in_proj(xs, g_mix[l].reshape(1, D), w_plain, w_ab)
        src = jnp.asarray([hh * LANES + c for hh in range(2 * DIFF_HEADS) for c in _head_lane_source()],
                          dtype=jnp.int32)
        w_qk = jnp.take(w_in[l][:, ab_end:qk_end], src, axis=1).astype(BF16)
        rope = _rope_proj(h, w_qk, tables)

        u, w, qd, kd, attn, egl = _gdn_prep(plain, ab, conv_w[l], _pad_lanes(a_log[l]),
                                            _pad_lanes(dt_bias[l]), B, S)
        o_a = _gdn_scan(u, w, qd, kd, attn, egl, plain, gdn_norm_g[l].reshape(1, GDN_D), B, S)

        lam_init = 0.8 - 0.6 * math.exp(-0.3 * l)
        o_b = _diff_attn(rope, plain, lambda_q1[l].reshape(1, -1), lambda_k1[l].reshape(1, -1),
                         lambda_q2[l].reshape(1, -1), lambda_k2[l].reshape(1, -1),
                         diff_norm_g[l].reshape(1, -1), lam_init, B, S)

        xs = _merge(o_a, o_b, plain, xs,w_out_a[l].astype(BF16), w_out_b[l].astype(BF16),
                    w_o[l].astype(BF16))

        kv = _mem_kv(mem.reshape(B * M, D), g_mem[l].reshape(1, D), w_ckv[l].astype(BF16))
        xs = _cross(xs, g_cross[l].reshape(1, D), w_cq[l].astype(BF16), kv, w_co[l].astype(BF16), S, M)

        xs = _ffn(xs, g_ffn[l].reshape(1, D), w_ffn_in[l].astype(BF16), w_ffn_out[l].astype(BF16),
                  g_final.reshape(1, D))
    return xs.reshape(B, S, D)
```

```python
import functools
import math

import jax
import jax.numpy as jnp
from jax import lax
from jax.experimental import pallas as pl
from jax.experimental.pallas import tpu as pltpu

F32 = jnp.float32
BF16 = jnp.bfloat16

EPS = 1e-6
LANES = 128
GDN_HEADS = 8
GDN_D = 128
CONV_WIDTH = 4
CHUNK = 64
DIFF_HEADS = 8
DIFF_DH = 64
ROPE_DIM = DIFF_DH // 4
ROPE_THETA = 500000.0
X_HEADS = 4
X_DH = 128
NEG_BIG = -1e30
LOG2E = math.log2(math.e)

VMEM_LIMIT = 48 * 1024 * 1024

P_QA, P_KA, P_VA, P_Z, P_VB, P_GA, P_GB, P_END = 0, 8, 16, 24, 32, 40, 48, 56
R_Q, R_K = 0, 8


def _dot(a, b):
    return jnp.dot(a, b, preferred_element_type=F32)


def _dot_nt(a, b):
    return lax.dot_general(a, b, (((1,), (1,)), ((), ())), preferred_element_type=F32)


def _mm(a, b):
    return _dot(a.astype(BF16), b.astype(BF16))


def _split3(a):
    hi = a.astype(BF16)
    r = a - hi.astype(F32)
    mid = r.astype(BF16)
    lo = (r - mid.astype(F32)).astype(BF16)
    return hi, mid, lo


def _dot_mask(mask_bf16, b):
    hi, mid, lo = _split3(b)
    return _dot(mask_bf16, hi) + (_dot(mask_bf16, mid) + _dot(mask_bf16, lo))


def _dot_rmask(b, mask_bf16):
    hi, mid, lo = _split3(b)
    return _dot(hi, mask_bf16) + (_dot(mid, mask_bf16) + _dot(lo, mask_bf16))


def _rms(x, g):
    ms = jnp.mean(x * x, axis=-1, keepdims=True)
    return x * lax.rsqrt(ms + EPS) * g


def _sigmoid(x):
    return 0.5 * jnp.tanh(0.5 * x) + 0.5


def _silu(x):
    half = 0.5 * x
    return half * jnp.tanh(half) + half


def _softplus(x):
    return jnp.maximum(x, 0.0) + jnp.log(1.0 + jnp.exp(-jnp.abs(x)))


ROPE_HALF = ROPE_DIM // 2
HALF_LANES = LANES // 2


def _head_lane_source():
    src = [0] * LANES
    for m in range(2):
        for d in range(DIFF_DH):
            if d < ROPE_HALF:
                lane = m * ROPE_HALF + d
            elif d < ROPE_DIM:
                lane = HALF_LANES + m * ROPE_HALF + (d - ROPE_HALF)
            else:
                lane = (ROPE_DIM if m == 0 else HALF_LANES + ROPE_DIM) + (d - ROPE_DIM)
            src[lane] = m * DIFF_DH + d
    return src


def _is_map1_lane(lane):
    return (lane < ROPE_HALF) | ((lane >= ROPE_DIM) & (lane < HALF_LANES + ROPE_HALF))


ROPE_GROUPS = LANES // ROPE_HALF


def _rope_table_kernel(pos_ref, inv_ref, c_ref, sg_ref):
    ang = pos_ref[...] * inv_ref[...]
    cos = jnp.cos(ang)
    sin = jnp.sin(ang)
    rows = ang.shape[0]
    lane = lax.broadcasted_iota(jnp.int32, ang.shape, 1)
    first = lane < ROPE_DIM
    second = (lane >= HALF_LANES) & (lane < HALF_LANES + ROPE_DIM)
    freq = lane & (ROPE_HALF - 1)
    for a in range(ROPE_GROUPS):
        idx = freq + a * ROPE_HALF
        cos_a = jnp.take_along_axis(cos, idx, axis=1)
        sin_a = jnp.take_along_axis(sin, idx, axis=1)
        c_ref[a * rows:(a + 1) * rows, :] = jnp.where(first | second, cos_a, 1.0)
        sg_ref[a * rows:(a + 1) * rows, :] = jnp.where(first, -sin_a, jnp.where(second, sin_a, 0.0))


def _rope_tables(positions, tm=1024):
    T = positions.size
    rows = tm // ROPE_GROUPS
    pos = positions.astype(F32).reshape(T // tm, ROPE_GROUPS, rows).transpose(0, 2, 1)
    pos = jnp.repeat(pos, ROPE_HALF, axis=2).reshape(T // ROPE_GROUPS, LANES)
    inv_freq = ROPE_THETA ** (-jnp.arange(0, ROPE_DIM, 2, dtype=F32) / ROPE_DIM)
    inv = jnp.tile(inv_freq, ROPE_GROUPS).reshape(1, LANES)
    out = jax.ShapeDtypeStruct((T, LANES), F32)
    return pl.pallas_call(
        _rope_table_kernel,
        grid=(T // tm,),
        in_specs=[pl.BlockSpec((rows, LANES), lambda i: (i, 0)),
                  pl.BlockSpec((1, LANES), lambda i: (0, 0))],
        out_specs=[pl.BlockSpec((tm, LANES), lambda i: (i, 0))] * 2,
        out_shape=[out, out],
        name="rope_tables",
    )(pos, inv)


PROJ_SUB = 256


def _in_proj_kernel(x_ref, g_ref, w_ref, wab_ref, out_ref, ab_ref, h_ref):
    @pl.when(pl.program_id(1) == 0)
    def _():
        hb = _rms(x_ref[...], g_ref[...]).astype(BF16)
        h_ref[...] = hb
        ab_ref[...] = _dot(hb, wab_ref[...])

    hb = h_ref[...]
    for c0 in range(0, out_ref.shape[1], PROJ_SUB):
        cs = slice(c0, c0 + PROJ_SUB)
        out_ref[:, cs] = _dot(hb, w_ref[:, cs]).astype(out_ref.dtype)


def _in_proj(x2d, g_mix, w_all, N, ab_unit, tm=1024, tn=1792):
    T, D = x2d.shape
    tm = min(tm, T)
    assert N % tn == 0
    return pl.pallas_call(
        _in_proj_kernel,
        grid=(T // tm, N // tn),
        in_specs=[pl.BlockSpec((tm, D), lambda i, j: (i, 0)),
                  pl.BlockSpec((1, D), lambda i, j: (0, 0)),
                  pl.BlockSpec((D, tn), lambda i, j: (0, j)),
                  pl.BlockSpec((D, LANES), lambda i, j: (0, ab_unit))],
        out_specs=[pl.BlockSpec((tm, tn), lambda i, j: (i, j)),
                   pl.BlockSpec((tm, LANES), lambda i, j: (i, 0)),
                   pl.BlockSpec((tm, D), lambda i, j: (i, 0))],
        out_shape=[jax.ShapeDtypeStruct((T, N), BF16),
                   jax.ShapeDtypeStruct((T, LANES), F32),
                   jax.ShapeDtypeStruct((T, D), BF16)],
        compiler_params=pltpu.CompilerParams(
            dimension_semantics=("parallel", "arbitrary"), vmem_limit_bytes=VMEM_LIMIT),
        name="in_proj",
    )(x2d, g_mix, w_all, w_all)


def _rope_proj_kernel(h_ref, w_ref, c_ref, sg_ref, out_ref):
    scale = jnp.where(pl.program_id(1) == 0, DIFF_DH ** -0.5 * LOG2E, 1.0)
    c = c_ref[...] * scale
    sg = sg_ref[...] * scale
    hb = h_ref[...]
    for c0 in range(0, out_ref.shape[1], PROJ_SUB):
        acc = _dot(hb, w_ref[:, c0:c0 + PROJ_SUB])
        for l0 in range(0, PROJ_SUB, LANES):
            a = acc[:, l0:l0 + LANES]
            y = a * c + pltpu.roll(a, HALF_LANES, 1) * sg
            out_ref[:, c0 + l0:c0 + l0 + LANES] = y.astype(out_ref.dtype)


def _rope_proj(h, w_all, col0, N, tables, tm=1024):
    T, D = h.shape
    tn = N // 2
    tm = min(tm, T)
    assert col0 % tn == 0
    return pl.pallas_call(
        _rope_proj_kernel,
        grid=(T // tm, 2),
        in_specs=[pl.BlockSpec((tm, D), lambda i, j: (i, 0)),
                  pl.BlockSpec((D, tn), lambda i, j: (0, col0 // tn + j)),
                  pl.BlockSpec((tm, LANES), lambda i, j: (i, 0)),
                  pl.BlockSpec((tm, LANES), lambda i, j: (i, 0))],
        out_specs=pl.BlockSpec((tm, tn), lambda i, j: (i, j)),
        out_shape=jax.ShapeDtypeStruct((T, N), BF16),
        compiler_params=pltpu.CompilerParams(
            dimension_semantics=("parallel", "parallel"), vmem_limit_bytes=VMEM_LIMIT),
        name="rope_proj",
    )(h, w_all, *tables)


GDN_TS = 4 * CHUNK
GDN_HEAD_GROUP = 4
SOLVE_BLOCK = 16
CHUNK_SHIFT = CHUNK.bit_length() - 1
SOLVE_SHIFT = SOLVE_BLOCK.bit_length() - 1
assert (1 << CHUNK_SHIFT) == CHUNK and (1 << SOLVE_SHIFT) == SOLVE_BLOCK and CHUNK // SOLVE_BLOCK == 4


def _gdn_prep_kernel(qkv_ref, ab_ref, cw_ref, alog_ref, dtb_ref,
                     u_ref, w_ref, qd_ref, kd_ref, attn_ref, egl_ref, halo_ref):
    TS, H, D = GDN_TS, GDN_HEADS, GDN_D

    @pl.when(pl.program_id(1) == 0)
    def _():
        halo_ref[...] = jnp.zeros(halo_ref.shape, F32)

    def conv_silu(c0):
        cs = slice(c0, c0 + D)
        ext = jnp.concatenate([halo_ref[:, cs], qkv_ref[:, cs].astype(F32)], axis=0)
        z = ext * cw_ref[0:1, cs]
        for t in range(1, CONV_WIDTH):
            z = pltpu.roll(z, 1, 0) + ext * cw_ref[t:t + 1, cs]
        return _silu(z[8:])

    def l2_normalised(x):
        return x * lax.rsqrt(jnp.sum(x * x, axis=-1, keepdims=True) + EPS)

    ri = lax.broadcasted_iota(jnp.int32, (TS, TS), 0)
    ci = lax.broadcasted_iota(jnp.int32, (TS, TS), 1)
    same = (ri >> CHUNK_SHIFT) == (ci >> CHUNK_SHIFT)
    incl = same & (ri >= ci)
    strict = same & (ri > ci)
    blk = (ri >> SOLVE_SHIFT) == (ci >> SOLVE_SHIFT)
    eye = jnp.where(ri == ci, 1.0, 0.0)

    ab = ab_ref[...]
    g_all = -jnp.exp(alog_ref[...]) * _softplus(ab + dtb_ref[...])
    b_all = _sigmoid(ab)
    gc_all = _dot_mask(jnp.where(incl, 1.0, 0.0).astype(BF16), g_all)
    gl_all = _dot_mask(jnp.where(same, 1.0, 0.0).astype(BF16), g_all)
    gc_t = _dot_rmask(g_all.T, jnp.where(same & (ri <= ci), 1.0, 0.0).astype(BF16))

    r8 = lax.broadcasted_iota(jnp.int32, (8, TS), 0)
    c8 = lax.broadcasted_iota(jnp.int32, (8, TS), 1)
    sel = jnp.where(c8 == r8 * CHUNK, 1.0, 0.0).astype(BF16)
    gl8 = _dot_mask(sel, gl_all)

    def head_group(hs):
        ld, lo, x, rhs = {}, {}, {}, {}
        for h in hs:
            cs = slice(h * D, (h + 1) * D)
            q = l2_normalised(conv_silu(h * D)) * (D ** -0.5)
            k = l2_normalised(conv_silu((H + h) * D))
            v = conv_silu((2 * H + h) * D)
            gc_col = gc_all[:, h:h + 1]
            gl_col = gl_all[:, h:h + 1]
            beta = b_all[:, H + h:H + h + 1]
            decay = jnp.exp(jnp.minimum(gc_col - gc_t[h:h + 1, :], 0.0))
            eg = jnp.exp(gc_col)
            kb = k * beta
            kbf = k.astype(BF16)
            lmat = jnp.where(strict, _dot_nt(kb.astype(BF16), kbf) * decay, 0.0)
            attn = jnp.where(incl, _dot_nt(q.astype(BF16), kbf) * decay, 0.0)
            attn_ref[:, h * TS:(h + 1) * TS] = attn.astype(attn_ref.dtype)
            qd_ref[:, cs] = (q * eg).astype(qd_ref.dtype)
            kd_ref[:, cs] = (k * jnp.exp(gl_col - gc_col)).astype(kd_ref.dtype)
            egl_ref[:, cs] = jnp.exp(jnp.broadcast_to(gl8[:, h:h + 1], (8, D)))
            ldh = jnp.where(blk, lmat, 0.0)
            ld[h] = ldh.astype(BF16)
            lo[h] = (lmat - ldh).astype(BF16)
            x[h] = eye - ldh
            rhs[h] = jnp.concatenate([v * beta, kb * eg], axis=1).astype(BF16)

        p = {h: _dot(ld[h], ld[h]) for h in hs}
        for _ in range(2):
            pb = {h: p[h].astype(BF16) for h in hs}
            x = {h: x[h] + _mm(x[h], pb[h]) for h in hs}
            p = {h: _dot(pb[h], pb[h]) for h in hs}
        td = {h: x[h] + _mm(x[h], p[h]) for h in hs}
        tdb = {h: td[h].astype(BF16) for h in hs}
        n = {h: _dot(tdb[h], lo[h]).astype(BF16) for h in hs}
        n2 = {h: _dot(n[h], n[h]) for h in hs}
        m1 = {h: td[h] + _mm(n2[h], tdb[h]) for h in hs}
        tinv = {h: m1[h] - _mm(n[h], m1[h]) for h in hs}
        for h in hs:
            cs = slice(h * D, (h + 1) * D)
            uw = _mm(tinv[h], rhs[h])
            u_ref[:, cs] = uw[:, :D].astype(u_ref.dtype)
            w_ref[:, cs] = uw[:, D:].astype(w_ref.dtype)

    for g0 in range(0, H, GDN_HEAD_GROUP):
        head_group(range(g0, g0 + GDN_HEAD_GROUP))
    halo_ref[...] = qkv_ref[TS - 8:TS, :].astype(F32)


def _gdn_prep(plain, ab, conv_w, a_log, dt_bias, B, S):
    T = B * S
    TS = GDN_TS
    ns = S // TS
    HD = GDN_HEADS * GDN_D
    row = lambda b, s: b * ns + s
    vec = pl.BlockSpec((1, LANES), lambda b, s: (0, 0))
    tokw = pl.BlockSpec((TS, HD), lambda b, s: (row(b, s), 0))
    act = jax.ShapeDtypeStruct((T, HD), BF16)
    return pl.pallas_call(
        _gdn_prep_kernel,
        grid=(B, ns),
        in_specs=[pl.BlockSpec((TS, 3 * HD), lambda b, s: (row(b, s), 0)),
                  pl.BlockSpec((TS, LANES), lambda b, s: (row(b, s), 0)),
                  pl.BlockSpec((CONV_WIDTH, 3 * HD), lambda b, s: (0, 0)),
                  vec, vec],
        out_specs=[tokw, tokw, tokw, tokw,
                   pl.BlockSpec((TS, GDN_HEADS * TS), lambda b, s: (row(b, s), 0)),
                   pl.BlockSpec((8, HD), lambda b, s: (row(b, s), 0))],
        out_shape=[act, act, act, act,
                   jax.ShapeDtypeStruct((T, GDN_HEADS * TS), BF16),
                   jax.ShapeDtypeStruct((B * ns * 8, HD), F32)],
        scratch_shapes=[pltpu.VMEM((8, 3 * HD), F32)],
        compiler_params=pltpu.CompilerParams(
            dimension_semantics=("parallel", "arbitrary"), vmem_limit_bytes=VMEM_LIMIT),
        name="gdn_prep",
    )(plain, ab, conv_w, a_log, dt_bias)


def _gdn_scan_kernel(u_ref, w_ref, qd_ref, kd_ref, attn_ref, egl_ref, z_ref, gn_ref, o_ref,
                     state_ref, vnew_ref, oq_ref):
    TS = GDN_TS
    D = GDN_D
    nb = u_ref.shape[0]
    s = pl.program_id(1)

    @pl.when(s == 0)
    def _():
        state_ref[...] = jnp.zeros(state_ref.shape, F32)

    for c in range(TS // CHUNK):
        r = slice(c * CHUNK, (c + 1) * CHUNK)
        for b in range(nb):
            for h in range(GDN_HEADS):
                cs = slice(h * D, (h + 1) * D)
                st = state_ref[b, h]
                wq = jnp.concatenate([w_ref[b, r, cs], qd_ref[b, r, cs]], axis=0)
                ws = _dot(wq, st.astype(BF16))
                v_new = (u_ref[b, r, cs].astype(F32) - ws[:CHUNK]).astype(BF16)
                vnew_ref[b, h, r, :] = v_new
                oq_ref[b, r, cs] = ws[CHUNK:]
                ktv = lax.dot_general(kd_ref[b, r, cs], v_new, (((0,), (0,)), ((), ())),
                                      preferred_element_type=F32)
                state_ref[b, h] = st * egl_ref[b, c:c + 1, cs] + ktv

    for b in range(nb):
        for h in range(GDN_HEADS):
            cs = slice(h * D, (h + 1) * D)
            o = oq_ref[b, :, cs] + _dot(attn_ref[b, :, h * TS:(h + 1) * TS], vnew_ref[b, h])
            o = _rms(o, gn_ref[...]) * _silu(z_ref[b, :, cs].astype(F32))
            o_ref[b, :, cs] = o.astype(o_ref.dtype)


def _gdn_scan(u, w, qd, kd, attn, egl, plain, gdn_norm_g, B, S, nb=4):
    TS = GDN_TS
    ns = S // TS
    HD = GDN_HEADS * GDN_D
    nb = min(nb, B)
    assert B % nb == 0
    seq = lambda a: a.reshape(B, S, a.shape[-1])
    tokw = pl.BlockSpec((nb, TS, HD), lambda b, s: (b, s, 0))
    out = pl.pallas_call(
        _gdn_scan_kernel,
        grid=(B // nb, ns),
        in_specs=[tokw, tokw, tokw, tokw,
                  pl.BlockSpec((nb, TS, GDN_HEADS * TS), lambda b, s: (b, s, 0)),
                  pl.BlockSpec((nb, 8, HD), lambda b, s: (b, s, 0)),
                  pl.BlockSpec((nb, TS, HD), lambda b, s: (b, s, P_Z * LANES // HD)),
                  pl.BlockSpec((1, GDN_D), lambda b, s: (0, 0))],
        out_specs=tokw,
        out_shape=jax.ShapeDtypeStruct((B, S, HD), BF16),
        scratch_shapes=[pltpu.VMEM((nb, GDN_HEADS, GDN_D, GDN_D), F32),
                        pltpu.VMEM((nb, GDN_HEADS, TS, GDN_D), BF16),
                        pltpu.VMEM((nb, TS, HD), F32)],
        compiler_params=pltpu.CompilerParams(
            dimension_semantics=("parallel", "arbitrary"), vmem_limit_bytes=VMEM_LIMIT),
        name="gdn_scan",
    )(seq(u), seq(w), seq(qd), seq(kd), seq(attn), egl.reshape(B, ns * 8, HD), seq(plain), gdn_norm_g)
    return out.reshape(B * S, HD)


def _diff_attn_kernel(q_ref, qn_ref, k_ref, v_ref, lq1_ref, lk1_ref, lq2_ref, lk2_ref, gn_ref, o_ref,
                      vext_ref, qz_ref, qnz_ref, sa_ref, sb_ref, sc_ref, sh_ref, m_ref, acc_ref,
                      *, tq, nq, lam_init):
    i = pl.program_id(2)
    dv = 2 * DIFF_DH
    tk = tq // 2

    def stack(src_ref, dst_ref):
        q = src_ref[...]
        map1 = _is_map1_lane(lax.broadcasted_iota(jnp.int32, q.shape, 1))
        zero = jnp.zeros_like(q)
        q1 = jnp.where(map1, q, zero)
        q2 = jnp.where(map1, zero, q)
        for half in range(2):
            rows = slice(half * tk, (half + 1) * tk)
            dst_ref[2 * half * tk:(2 * half + 1) * tk, :] = q1[rows]
            dst_ref[(2 * half + 1) * tk:(2 * half + 2) * tk, :] = q2[rows]

    def scores(j, s_ref, qsrc_ref, r0=0):
        off = pl.multiple_of(j * tk, tk)
        s_ref[...] = _dot_nt(qsrc_ref[r0:, :], k_ref[pl.ds(off, tk), :])

    def consume(j, s_ref, diag=None, r0=0):
        off = pl.multiple_of(j * tk, tk)
        sc = s_ref[...]
        if diag is not None:
            rr = lax.broadcasted_iota(jnp.int32, sc.shape, 0)
            cc = lax.broadcasted_iota(jnp.int32, sc.shape, 1)
            keep = (rr & (tk - 1)) >= cc
            if diag == "D1":
                keep = keep | (rr >= tq)
            sc = jnp.where(keep, sc, NEG_BIG)
        rows = slice(r0, 2 * tq)
        m_prev = m_ref[rows, :]
        m_new = jnp.maximum(m_prev, jnp.max(sc, axis=-1, keepdims=True))
        p = jnp.concatenate([jnp.exp2(sc[:, c0:c0 + LANES] - m_new) for c0 in range(0, tk, LANES)],
                            axis=1)
        alpha = jnp.exp2(m_prev - m_new)
        pv = _dot(p.astype(BF16), vext_ref[pl.ds(off, tk), :])
        for c0 in range(0, 2 * dv, LANES):
            acc_ref[rows, c0:c0 + LANES] = alpha * acc_ref[rows, c0:c0 + LANES] + pv[:, c0:c0 + LANES]
        m_ref[rows, :] = m_new

    def next_d1():
        nxt = jnp.minimum(i + 1, nq - 1)
        stack(qn_ref, qnz_ref)
        scores(2 * nxt, sc_ref, qnz_ref)

    stack(q_ref, qz_ref)
    m_ref[...] = jnp.full(m_ref.shape, NEG_BIG, F32)
    acc_ref[...] = jnp.zeros(acc_ref.shape, F32)

    @pl.when(i == 0)
    def _():
        vext_ref[:, :dv] = v_ref[...]
        vext_ref[:, dv:] = jnp.ones((vext_ref.shape[0], dv), BF16)
        scores(0, sc_ref, qz_ref)

    scores(2 * i + 1, sh_ref, qz_ref, r0=tq)
    consume(2 * i, sc_ref, diag="D1")

    @pl.when(i == 0)
    def _():
        next_d1()
        consume(1, sh_ref, diag="D2", r0=tq)

    @pl.when(i > 0)
    def _():
        scores(0, sa_ref, qz_ref)
        consume(2 * i + 1, sh_ref, diag="D2", r0=tq)

        def pair(p, carry):
            j = 2 * p
            scores(j + 1, sb_ref, qz_ref)
            consume(j, sa_ref)
            scores(j + 2, sa_ref, qz_ref)
            consume(j + 1, sb_ref)
            return carry

        lax.fori_loop(0, i - 1, pair, 0)
        scores(2 * i - 1, sb_ref, qz_ref)
        consume(2 * i - 2, sa_ref)
        next_d1()
        consume(2 * i - 1, sb_ref)

    lam = (jnp.exp(jnp.sum(lq1_ref[...] * lk1_ref[...], axis=-1, keepdims=True))
           - jnp.exp(jnp.sum(lq2_ref[...] * lk2_ref[...], axis=-1, keepdims=True))
           + lam_init)
    for half in range(2):
        a1 = acc_ref[2 * half * tk:(2 * half + 1) * tk, :]
        a2 = acc_ref[(2 * half + 1) * tk:(2 * half + 2) * tk, :]
        o = a1[:, :dv] / a1[:, dv:] - lam * (a2[:, :dv] / a2[:, dv:])
        o = _rms(o, gn_ref[...]) * (1.0 - lam_init)
        o_ref[half * tk:(half + 1) * tk, :] = o.astype(o_ref.dtype)


def _diff_attn(rope, plain, lq1, lk1, lq2, lk2, diff_norm_g, lam_init, B, S, tq=1024):
    T = B * S
    nq = S // tq
    dv = 2 * DIFF_DH
    lam_spec = pl.BlockSpec((1, DIFF_DH), lambda b, h, i: (0, 0))
    score_buf = pltpu.VMEM((2 * tq, tq // 2), F32)
    stacked_q = pltpu.VMEM((2 * tq, LANES), BF16)
    return pl.pallas_call(
        functools.partial(_diff_attn_kernel, tq=tq, nq=nq, lam_init=lam_init),
        grid=(B, DIFF_HEADS, nq),
        in_specs=[pl.BlockSpec((tq, LANES), lambda b, h, i: (b * nq + i, R_Q + h)),
                  pl.BlockSpec((tq, LANES),
                               lambda b, h, i: (b * nq + jnp.minimum(i + 1, nq - 1), R_Q + h)),
                  pl.BlockSpec((S, LANES), lambda b, h, i: (b, R_K + h)),
                  pl.BlockSpec((S, LANES), lambda b, h, i: (b, P_VB + h)),
                  lam_spec, lam_spec, lam_spec, lam_spec,
                  pl.BlockSpec((1, dv), lambda b, h, i: (0, 0))],
        out_specs=pl.BlockSpec((tq, LANES), lambda b, h, i: (b * nq + i, h)),
        out_shape=jax.ShapeDtypeStruct((T, DIFF_HEADS * dv), BF16),
        scratch_shapes=[pltpu.VMEM((S, 2 * dv), BF16),
                        stacked_q, stacked_q,
                        score_buf, score_buf, score_buf,
                        pltpu.VMEM((tq, tq // 2), F32),
                        pltpu.VMEM((2 * tq, LANES), F32),
                        pltpu.VMEM((2 * tq, 2 * dv), F32)],
        compiler_params=pltpu.CompilerParams(
            dimension_semantics=("parallel", "parallel", "arbitrary"),
            vmem_limit_bytes=VMEM_LIMIT),
        name="diff_attn",
    )(rope, rope, rope, plain, lq1, lk1, lq2, lk2, diff_norm_g)


def _merge_kernel(oa_ref, ob_ref, ga_ref, gb_ref, x_ref, woa_ref, wob_ref, wo_ref, x1_ref):
    ya = _dot(oa_ref[...], woa_ref[...])
    yb = _dot(ob_ref[...], wob_ref[...])
    merged = _sigmoid(ga_ref[...].astype(F32)) * ya + _sigmoid(gb_ref[...].astype(F32)) * yb
    x1_ref[...] = x_ref[...] + _dot(merged.astype(BF16), wo_ref[...])


def _merge(o_a, o_b, plain, x2d, w_out_a, w_out_b, w_o, tm=512):
    T, D = x2d.shape
    tokd = pl.BlockSpec((tm, D), lambda i: (i, 0))
    wspec = pl.BlockSpec((D, D), lambda i: (0, 0))
    return pl.pallas_call(
        _merge_kernel,
        grid=(T // tm,),
        in_specs=[tokd, tokd,
                  pl.BlockSpec((tm, D), lambda i: (i, P_GA * LANES // D)),
                  pl.BlockSpec((tm, D), lambda i: (i, P_GB * LANES // D)),
                  tokd, wspec, wspec, wspec],
        out_specs=tokd,
        out_shape=jax.ShapeDtypeStruct((T, D), F32),
        compiler_params=pltpu.CompilerParams(
            dimension_semantics=("parallel",), vmem_limit_bytes=VMEM_LIMIT),
        name="merge",
    )(o_a, o_b, plain, plain, x2d, w_out_a, w_out_b, w_o)


def _mem_kv_kernel(m_ref, g_ref, w_ref, o_ref):
    o_ref[...] = _dot(_rms(m_ref[...], g_ref[...]).astype(BF16), w_ref[...]).astype(o_ref.dtype)


def _mem_kv(mem2d, g_mem, w_ckv):
    R, D = mem2d.shape
    N = w_ckv.shape[1]
    return pl.pallas_call(
        _mem_kv_kernel,
        grid=(1,),
        in_specs=[pl.BlockSpec((R, D), lambda i: (0, 0)),
                  pl.BlockSpec((1, D), lambda i: (0, 0)),
                  pl.BlockSpec((D, N), lambda i: (0, 0))],
        out_specs=pl.BlockSpec((R, N), lambda i: (0, 0)),
        out_shape=jax.ShapeDtypeStruct((R, N), BF16),
        compiler_params=pltpu.CompilerParams(vmem_limit_bytes=VMEM_LIMIT),
        name="mem_kv",
    )(mem2d, g_mem, w_ckv)


def _cross_kernel(x_ref, g_ref, wq_ref, kv_ref, wo_ref, o_ref):
    x = x_ref[...]
    hx = _rms(x, g_ref[...]).astype(BF16)
    qc = (_dot(hx, wq_ref[...]) * (X_DH ** -0.5)).astype(BF16)
    xw = X_HEADS * X_DH
    outs = []
    for hh in range(X_HEADS):
        cs = slice(hh * X_DH, (hh + 1) * X_DH)
        sc = _dot_nt(qc[:, cs], kv_ref[:, cs])
        p = jnp.exp(sc - jnp.max(sc, axis=-1, keepdims=True))
        p = p / jnp.sum(p, axis=-1, keepdims=True)
        outs.append(_dot(p.astype(BF16), kv_ref[:, xw + hh * X_DH:xw + (hh + 1) * X_DH]))
    oc = jnp.concatenate(outs, axis=1).astype(BF16)
    o_ref[...] = x + _dot(oc, wo_ref[...])


def _cross(x1, g_cross, w_cq, kv, w_co, S, M, tm=512):
    T, D = x1.shape
    xw = X_HEADS * X_DH
    per_b = S // tm
    tokd = pl.BlockSpec((tm, D), lambda i: (i, 0))
    return pl.pallas_call(
        _cross_kernel,
        grid=(T // tm,),
        in_specs=[tokd,
                  pl.BlockSpec((1, D), lambda i: (0, 0)),
                  pl.BlockSpec((D, xw), lambda i: (0, 0)),
                  pl.BlockSpec((M, 2 * xw), lambda i: (i // per_b, 0)),
                  pl.BlockSpec((xw, D), lambda i: (0, 0))],
        out_specs=tokd,
        out_shape=jax.ShapeDtypeStruct((T, D), F32),
        compiler_params=pltpu.CompilerParams(
            dimension_semantics=("parallel",), vmem_limit_bytes=VMEM_LIMIT),
        name="cross",
    )(x1, g_cross, w_cq, kv, w_co)


def _ffn_kernel(x_ref, g_ref, wi_ref, wo_ref, gf_ref, o_ref, *, tf):
    x = x_ref[...]
    hb = _rms(x, g_ref[...]).astype(BF16)
    F = wo_ref.shape[0]
    acc = x
    for f in range(F // tf):
        act = _silu(_dot(hb, wi_ref[:, f * tf:(f + 1) * tf])) * _dot(hb, wi_ref[:, F + f * tf:F + (f + 1) * tf])
        acc = acc + _dot(act.astype(BF16), wo_ref[f * tf:(f + 1) * tf, :])
    o_ref[...] = _rms(acc, gf_ref[...])


def _ffn(x2, g_ffn, w_ffn_in, w_ffn_out, g_final, tm=512, tf=256):
    T, D = x2.shape
    F = w_ffn_out.shape[0]
    tokd = pl.BlockSpec((tm, D), lambda i: (i, 0))
    vec = pl.BlockSpec((1, D), lambda i: (0, 0))
    resident = pl.Buffered(1)
    return pl.pallas_call(
        functools.partial(_ffn_kernel, tf=tf),
        grid=(T // tm,),
        in_specs=[tokd, vec,
                  pl.BlockSpec((D, 2 * F), lambda i: (0, 0), pipeline_mode=resident),
                  pl.BlockSpec((F, D), lambda i: (0, 0), pipeline_mode=resident),
                  vec],
        out_specs=tokd,
        out_shape=jax.ShapeDtypeStruct((T, D), F32),
        compiler_params=pltpu.CompilerParams(
            dimension_semantics=("parallel",), vmem_limit_bytes=VMEM_LIMIT),
        name="ffn",
    )(x2, g_ffn, w_ffn_in, w_ffn_out, g_final)


def _pad_lanes(v):
    v = v.reshape(1, -1).astype(F32)
    return jnp.pad(v, ((0, 0), (0, LANES - v.shape[1])))


def kernel(x, mem, positions, g_mix, w_in, conv_w, a_log, dt_bias, gdn_norm_g, lambda_q1, lambda_k1, lambda_q2, lambda_k2, diff_norm_g, w_branch_gate, w_out_a, w_out_b, w_o, g_cross, g_mem, w_cq, w_ckv, w_co, g_ffn, w_ffn_in, w_ffn_out, g_final):
    B, S, D = x.shape
    M = mem.shape[1]
    depth = w_in.shape[0]
    assert depth == 1, "the final rmsnorm is fused into the (single) layer's ffn call"
    qkvz = 4 * GDN_HEADS * GDN_D
    xs = x.reshape(B * S, D)
    tables = _rope_tables(positions)
    kv = None
    for l in range(depth):
        ab_end = qkvz + 2 * GDN_HEADS
        qk_end = ab_end + 2 * DIFF_HEADS * 2 * DIFF_DH
        src = jnp.asarray([hh * LANES + c for hh in range(2 * DIFF_HEADS) for c in _head_lane_source()],
                          dtype=jnp.int32)
        n_v = w_in.shape[2] - qk_end
        n_gate = w_branch_gate.shape[2]
        n_plain = qkvz + n_v + n_gate
        n_rope = qk_end - ab_end
        pieces = [(0, w_in[l][:, :qkvz]),
                  (qkvz, w_in[l][:, qk_end:]),
                  (qkvz + n_v, w_branch_gate[l]),
                  (n_plain, jnp.take(w_in[l][:, ab_end:qk_end], src, axis=1)),
                  (n_plain + n_rope, w_in[l][:, qkvz:ab_end])]
        w_all = jnp.zeros((D, n_plain + n_rope + LANES), BF16)
        for c0, piece in pieces:
            w_all = lax.dynamic_update_slice(w_all, piece.astype(BF16), (0, c0))
        plain, ab, h = _in_proj(xs, g_mix[l].reshape(1, D), w_all, n_plain, (n_plain + n_rope) // LANES)
        rope = _rope_proj(h, w_all, n_plain, n_rope, tables)

        u, w, qd, kd, attn, egl = _gdn_prep(plain, ab, conv_w[l], _pad_lanes(a_log[l]),
                                            _pad_lanes(dt_bias[l]), B, S)
        o_a = _gdn_scan(u, w, qd, kd, attn, egl, plain, gdn_norm_g[l].reshape(1, GDN_D), B, S)

        lam_init = 0.8 - 0.6 * math.exp(-0.3 * l)
        o_b = _diff_attn(rope, plain, lambda_q1[l].reshape(1, -1), lambda_k1[l].reshape(1, -1),
                         lambda_q2[l].reshape(1, -1), lambda_k2[l].reshape(1, -1),
                         diff_norm_g[l].reshape(1, -1), lam_init, B, S)

        xs = _merge(o_a, o_b, plain, xs,w_out_a[l].astype(BF16), w_out_b[l].astype(BF16),
                    w_o[l].astype(BF16))

        kv = _mem_kv(mem.reshape(B * M, D), g_mem[l].reshape(1, D), w_ckv[l].astype(BF16))
        xs = _cross(xs, g_cross[l].reshape(1, D), w_cq[l].astype(BF16), kv, w_co[l].astype(BF16), S, M)

        xs = _ffn(xs, g_ffn[l].reshape(1, D), w_ffn_in[l].astype(BF16), w_ffn_out[l].astype(BF16),
                  g_final.reshape(1, D))
    return xs.reshape(B, S, D)
```

```python
import functools
import math

import jax
import jax.numpy as jnp
from jax import lax
from jax.experimental import pallas as pl
from jax.experimental.pallas import tpu as pltpu

F32 = jnp.float32
BF16 = jnp.bfloat16

EPS = 1e-6
LANES = 128
GDN_HEADS = 8
GDN_D = 128
CONV_WIDTH = 4
CHUNK = 64
DIFF_HEADS = 8
DIFF_DH = 64
ROPE_DIM = DIFF_DH // 4
ROPE_THETA = 500000.0
X_HEADS = 4
X_DH = 128
NEG_BIG = -1e30
LOG2E = math.log2(math.e)

VMEM_LIMIT = 48 * 1024 * 1024

P_QA, P_KA, P_VA, P_Z, P_VB, P_GA, P_GB, P_END = 0, 8, 16, 24, 32, 40, 48, 56
R_Q, R_K = 0, 8


def _dot(a, b):
    return jnp.dot(a, b, preferred_element_type=F32)


def _dot_nt(a, b):
    return lax.dot_general(a, b, (((1,), (1,)), ((), ())), preferred_element_type=F32)


def _mm(a, b):
    return _dot(a.astype(BF16), b.astype(BF16))


def _split3(a):
    hi = a.astype(BF16)
    r = a - hi.astype(F32)
    mid = r.astype(BF16)
    lo = (r - mid.astype(F32)).astype(BF16)
    return hi, mid, lo


def _dot_mask(mask_bf16, b):
    hi, mid, lo = _split3(b)
    return _dot(mask_bf16, hi) + (_dot(mask_bf16, mid) + _dot(mask_bf16, lo))


def _dot_rmask(b, mask_bf16):
    hi, mid, lo = _split3(b)
    return _dot(hi, mask_bf16) + (_dot(mid, mask_bf16) + _dot(lo, mask_bf16))


def _rms(x, g):
    ms = jnp.mean(x * x, axis=-1, keepdims=True)
    return x * lax.rsqrt(ms + EPS) * g


def _sigmoid(x):
    return 0.5 * jnp.tanh(0.5 * x) + 0.5


def _silu(x):
    half = 0.5 * x
    return half * jnp.tanh(half) + half


def _softplus(x):
    return jnp.maximum(x, 0.0) + jnp.log(1.0 + jnp.exp(-jnp.abs(x)))


ROPE_HALF = ROPE_DIM // 2
HALF_LANES = LANES // 2


def _head_lane_source():
    src = [0] * LANES
    for m in range(2):
        for d in range(DIFF_DH):
            if d < ROPE_HALF:
                lane = m * ROPE_HALF + d
            elif d < ROPE_DIM:
                lane = HALF_LANES + m * ROPE_HALF + (d - ROPE_HALF)
            else:
                lane = (ROPE_DIM if m == 0 else HALF_LANES + ROPE_DIM) + (d - ROPE_DIM)
            src[lane] = m * DIFF_DH + d
    return src


def _is_map1_lane(lane):
    return (lane < ROPE_HALF) | ((lane >= ROPE_DIM) & (lane < HALF_LANES + ROPE_HALF))


ROPE_GROUPS = LANES // ROPE_HALF


def _rope_table_kernel(pos_ref, inv_ref, c_ref, sg_ref):
    ang = pos_ref[...] * inv_ref[...]
    cos = jnp.cos(ang)
    sin = jnp.sin(ang)
    rows = ang.shape[0]
    lane = lax.broadcasted_iota(jnp.int32, ang.shape, 1)
    first = lane < ROPE_DIM
    second = (lane >= HALF_LANES) & (lane < HALF_LANES + ROPE_DIM)
    freq = lane & (ROPE_HALF - 1)
    for a in range(ROPE_GROUPS):
        idx = freq + a * ROPE_HALF
        cos_a = jnp.take_along_axis(cos, idx, axis=1)
        sin_a = jnp.take_along_axis(sin, idx, axis=1)
        c_ref[a * rows:(a + 1) * rows, :] = jnp.where(first | second, cos_a, 1.0)
        sg_ref[a * rows:(a + 1) * rows, :] = jnp.where(first, -sin_a, jnp.where(second, sin_a, 0.0))


def _rope_tables(positions, tm=1024):
    T = positions.size
    rows = tm // ROPE_GROUPS
    pos = positions.astype(F32).reshape(T // tm, ROPE_GROUPS, rows).transpose(0, 2, 1)
    pos = jnp.repeat(pos, ROPE_HALF, axis=2).reshape(T // ROPE_GROUPS, LANES)
    inv_freq = ROPE_THETA ** (-jnp.arange(0, ROPE_DIM, 2, dtype=F32) / ROPE_DIM)
    inv = jnp.tile(inv_freq, ROPE_GROUPS).reshape(1, LANES)
    out = jax.ShapeDtypeStruct((T, LANES), F32)
    return pl.pallas_call(
        _rope_table_kernel,
        grid=(T // tm,),
        in_specs=[pl.BlockSpec((rows, LANES), lambda i: (i, 0)),
                  pl.BlockSpec((1, LANES), lambda i: (0, 0))],
        out_specs=[pl.BlockSpec((tm, LANES), lambda i: (i, 0))] * 2,
        out_shape=[out, out],
        name="rope_tables",
    )(pos, inv)


PROJ_SUB = 256


def _in_proj_kernel(x_ref, g_ref, w_ref, wab_ref, out_ref, ab_ref, h_ref):
    @pl.when(pl.program_id(1) == 0)
    def _():
        hb = _rms(x_ref[...], g_ref[...]).astype(BF16)
        h_ref[...] = hb
        ab_ref[...] = _dot(hb, wab_ref[...])

    hb = h_ref[...]
    for c0 in range(0, out_ref.shape[1], PROJ_SUB):
        cs = slice(c0, c0 + PROJ_SUB)
        out_ref[:, cs] = _dot(hb, w_ref[:, cs]).astype(out_ref.dtype)


def _in_proj(x2d, g_mix, w_plain, w_ab, tm=1024, tn=1792):
    T, D = x2d.shape
    N = w_plain.shape[1]
    tm = min(tm, T)
    return pl.pallas_call(
        _in_proj_kernel,
        grid=(T // tm, N // tn),
        in_specs=[pl.BlockSpec((tm, D), lambda i, j: (i, 0)),
                  pl.BlockSpec((1, D), lambda i, j: (0, 0)),
                  pl.BlockSpec((D, tn), lambda i, j: (0, j)),
                  pl.BlockSpec((D, LANES), lambda i, j: (0, 0))],
        out_specs=[pl.BlockSpec((tm, tn), lambda i, j: (i, j)),
                   pl.BlockSpec((tm, LANES), lambda i, j: (i, 0)),
                   pl.BlockSpec((tm, D), lambda i, j: (i, 0))],
        out_shape=[jax.ShapeDtypeStruct((T, N), BF16),
                   jax.ShapeDtypeStruct((T, LANES), F32),
                   jax.ShapeDtypeStruct((T, D), BF16)],
        compiler_params=pltpu.CompilerParams(
            dimension_semantics=("parallel", "arbitrary"), vmem_limit_bytes=VMEM_LIMIT),
        name="in_proj",
    )(x2d, g_mix, w_plain, w_ab)


def _rope_proj_kernel(h_ref, w_ref, c_ref, sg_ref, out_ref):
    scale = jnp.where(pl.program_id(1) == 0, DIFF_DH ** -0.5 * LOG2E, 1.0)
    c = c_ref[...] * scale
    sg = sg_ref[...] * scale
    hb = h_ref[...]
    for c0 in range(0, out_ref.shape[1], PROJ_SUB):
        acc = _dot(hb, w_ref[:, c0:c0 + PROJ_SUB])
        for l0 in range(0, PROJ_SUB, LANES):
            a = acc[:, l0:l0 + LANES]
            y = a * c + pltpu.roll(a, HALF_LANES, 1) * sg
            out_ref[:, c0 + l0:c0 + l0 + LANES] = y.astype(out_ref.dtype)


def _rope_proj(h, w_qk, tables, tm=1024):
    T, D = h.shape
    N = w_qk.shape[1]
    tn = N // 2
    tm = min(tm, T)
    return pl.pallas_call(
        _rope_proj_kernel,
        grid=(T // tm, 2),
        in_specs=[pl.BlockSpec((tm, D), lambda i, j: (i, 0)),
                  pl.BlockSpec((D, tn), lambda i, j: (0, j)),
                  pl.BlockSpec((tm, LANES), lambda i, j: (i, 0)),
                  pl.BlockSpec((tm, LANES), lambda i, j: (i, 0))],
        out_specs=pl.BlockSpec((tm, tn), lambda i, j: (i, j)),
        out_shape=jax.ShapeDtypeStruct((T, N), BF16),
        compiler_params=pltpu.CompilerParams(
            dimension_semantics=("parallel", "parallel"), vmem_limit_bytes=VMEM_LIMIT),
        name="rope_proj",
    )(h, w_qk, *tables)


GDN_TS = 4 * CHUNK
GDN_HEAD_GROUP = 4
SOLVE_BLOCK = 16
CHUNK_SHIFT = CHUNK.bit_length() - 1
SOLVE_SHIFT = SOLVE_BLOCK.bit_length() - 1
assert (1 << CHUNK_SHIFT) == CHUNK and (1 << SOLVE_SHIFT) == SOLVE_BLOCK and CHUNK // SOLVE_BLOCK == 4


def _gdn_prep_kernel(qkv_ref, ab_ref, cw_ref, alog_ref, dtb_ref,
                     u_ref, w_ref, qd_ref, kd_ref, attn_ref, egl_ref, halo_ref):
    TS, H, D = GDN_TS, GDN_HEADS, GDN_D

    @pl.when(pl.program_id(1) == 0)
    def _():
        halo_ref[...] = jnp.zeros(halo_ref.shape, F32)

    def conv_silu(c0):
        cs = slice(c0, c0 + D)
        ext = jnp.concatenate([halo_ref[:, cs], qkv_ref[:, cs].astype(F32)], axis=0)
        z = ext * cw_ref[0:1, cs]
        for t in range(1, CONV_WIDTH):
            z = pltpu.roll(z, 1, 0) + ext * cw_ref[t:t + 1, cs]
        return _silu(z[8:])

    def l2_normalised(x):
        return x * lax.rsqrt(jnp.sum(x * x, axis=-1, keepdims=True) + EPS)

    ri = lax.broadcasted_iota(jnp.int32, (TS, TS), 0)
    ci = lax.broadcasted_iota(jnp.int32, (TS, TS), 1)
    same = (ri >> CHUNK_SHIFT) == (ci >> CHUNK_SHIFT)
    incl = same & (ri >= ci)
    strict = same & (ri > ci)
    blk = (ri >> SOLVE_SHIFT) == (ci >> SOLVE_SHIFT)
    eye = jnp.where(ri == ci, 1.0, 0.0)

    ab = ab_ref[...]
    g_all = -jnp.exp(alog_ref[...]) * _softplus(ab + dtb_ref[...])
    b_all = _sigmoid(ab)
    gc_all = _dot_mask(jnp.where(incl, 1.0, 0.0).astype(BF16), g_all)
    gl_all = _dot_mask(jnp.where(same, 1.0, 0.0).astype(BF16), g_all)
    gc_t = _dot_rmask(g_all.T, jnp.where(same & (ri <= ci), 1.0, 0.0).astype(BF16))

    r8 = lax.broadcasted_iota(jnp.int32, (8, TS), 0)
    c8 = lax.broadcasted_iota(jnp.int32, (8, TS), 1)
    sel = jnp.where(c8 == r8 * CHUNK, 1.0, 0.0).astype(BF16)
    gl8 = _dot_mask(sel, gl_all)

    def head_group(hs):
        ld, lo, x, rhs = {}, {}, {}, {}
        for h in hs:
            cs = slice(h * D, (h + 1) * D)
            q = l2_normalised(conv_silu(h * D)) * (D ** -0.5)
            k = l2_normalised(conv_silu((H + h) * D))
            v = conv_silu((2 * H + h) * D)
            gc_col = gc_all[:, h:h + 1]
            gl_col = gl_all[:, h:h + 1]
            beta = b_all[:, H + h:H + h + 1]
            decay = jnp.exp(jnp.minimum(gc_col - gc_t[h:h + 1, :], 0.0))
            eg = jnp.exp(gc_col)
            kb = k * beta
            kbf = k.astype(BF16)
            kq = _dot_nt(jnp.concatenate([kb.astype(BF16), q.astype(BF16)], axis=0), kbf)
            lmat = jnp.where(strict, kq[:TS] * decay, 0.0)
            attn = jnp.where(incl, kq[TS:] * decay, 0.0)
            attn_ref[:, h * TS:(h + 1) * TS] = attn.astype(attn_ref.dtype)
            qd_ref[:, cs] = (q * eg).astype(qd_ref.dtype)
            kd_ref[:, cs] = (k * jnp.exp(gl_col - gc_col)).astype(kd_ref.dtype)
            egl_ref[:, cs] = jnp.exp(jnp.broadcast_to(gl8[:, h:h + 1], (8, D)))
            ldh = jnp.where(blk, lmat, 0.0)
            ld[h] = ldh.astype(BF16)
            lo[h] = (lmat - ldh).astype(BF16)
            x[h] = eye - ldh
            rhs[h] = jnp.concatenate([v * beta, kb * eg], axis=1).astype(BF16)

        p = {h: _dot(ld[h], ld[h]) for h in hs}
        for _ in range(2):
            pb = {h: p[h].astype(BF16) for h in hs}
            x = {h: x[h] + _mm(x[h], pb[h]) for h in hs}
            p = {h: _dot(pb[h], pb[h]) for h in hs}
        td = {h: x[h] + _mm(x[h], p[h]) for h in hs}
        tdb = {h: td[h].astype(BF16) for h in hs}
        n = {h: _dot(tdb[h], lo[h]).astype(BF16) for h in hs}
        n2 = {h: _dot(n[h], n[h]) for h in hs}
        m1 = {h: td[h] + _mm(n2[h], tdb[h]) for h in hs}
        tinv = {h: m1[h] - _mm(n[h], m1[h]) for h in hs}
        for h in hs:
            cs = slice(h * D, (h + 1) * D)
            uw = _mm(tinv[h], rhs[h])
            u_ref[:, cs] = uw[:, :D].astype(u_ref.dtype)
            w_ref[:, cs] = uw[:, D:].astype(w_ref.dtype)

    for g0 in range(0, H, GDN_HEAD_GROUP):
        head_group(range(g0, g0 + GDN_HEAD_GROUP))
    halo_ref[...] = qkv_ref[TS - 8:TS, :].astype(F32)


def _gdn_prep(plain, ab, conv_w, a_log, dt_bias, B, S):
    T = B * S
    TS = GDN_TS
    ns = S // TS
    HD = GDN_HEADS * GDN_D
    row = lambda b, s: b * ns + s
    vec = pl.BlockSpec((1, LANES), lambda b, s: (0, 0))
    tokw = pl.BlockSpec((TS, HD), lambda b, s: (row(b, s), 0))
    act = jax.ShapeDtypeStruct((T, HD), BF16)
    return pl.pallas_call(
        _gdn_prep_kernel,
        grid=(B, ns),
        in_specs=[pl.BlockSpec((TS, 3 * HD), lambda b, s: (row(b, s), 0)),
                  pl.BlockSpec((TS, LANES), lambda b, s: (row(b, s), 0)),
                  pl.BlockSpec((CONV_WIDTH, 3 * HD), lambda b, s: (0, 0)),
                  vec, vec],
        out_specs=[tokw, tokw, tokw, tokw,
                   pl.BlockSpec((TS, GDN_HEADS * TS), lambda b, s: (row(b, s), 0)),
                   pl.BlockSpec((8, HD), lambda b, s: (row(b, s), 0))],
        out_shape=[act, act, act, act,
                   jax.ShapeDtypeStruct((T, GDN_HEADS * TS), BF16),
                   jax.ShapeDtypeStruct((B * ns * 8, HD), F32)],
        scratch_shapes=[pltpu.VMEM((8, 3 * HD), F32)],
        compiler_params=pltpu.CompilerParams(
            dimension_semantics=("parallel", "arbitrary"), vmem_limit_bytes=VMEM_LIMIT),
        name="gdn_prep",
    )(plain, ab, conv_w, a_log, dt_bias)


def _gdn_scan_kernel(u_ref, w_ref, qd_ref, kd_ref, attn_ref, egl_ref, z_ref, gn_ref, o_ref,
                     state_ref, vnew_ref, oq_ref):
    TS = GDN_TS
    D = GDN_D
    nb = u_ref.shape[0]
    s = pl.program_id(1)

    @pl.when(s == 0)
    def _():
        state_ref[...] = jnp.zeros(state_ref.shape, F32)

    for c in range(TS // CHUNK):
        r = slice(c * CHUNK, (c + 1) * CHUNK)
        for b in range(nb):
            for h in range(GDN_HEADS):
                cs = slice(h * D, (h + 1) * D)
                st = state_ref[b, h]
                wq = jnp.concatenate([w_ref[b, r, cs], qd_ref[b, r, cs]], axis=0)
                ws = _dot(wq, st.astype(BF16))
                v_new = (u_ref[b, r, cs].astype(F32) - ws[:CHUNK]).astype(BF16)
                vnew_ref[b, h, r, :] = v_new
                oq_ref[b, r, cs] = ws[CHUNK:]
                ktv = lax.dot_general(kd_ref[b, r, cs], v_new, (((0,), (0,)), ((), ())),
                                      preferred_element_type=F32)
                state_ref[b, h] = st * egl_ref[b, c:c + 1, cs] + ktv

    for b in range(nb):
        for h in range(GDN_HEADS):
            cs = slice(h * D, (h + 1) * D)
            o = oq_ref[b, :, cs] + _dot(attn_ref[b, :, h * TS:(h + 1) * TS], vnew_ref[b, h])
            o = _rms(o, gn_ref[...]) * _silu(z_ref[b, :, cs].astype(F32))
            o_ref[b, :, cs] = o.astype(o_ref.dtype)


def _gdn_scan(u, w, qd, kd, attn, egl, plain, gdn_norm_g, B, S, nb=4):
    TS = GDN_TS
    ns = S // TS
    HD = GDN_HEADS * GDN_D
    nb = min(nb, B)
    assert B % nb == 0
    seq = lambda a: a.reshape(B, S, a.shape[-1])
    tokw = pl.BlockSpec((nb, TS, HD), lambda b, s: (b, s, 0))
    out = pl.pallas_call(
        _gdn_scan_kernel,
        grid=(B // nb, ns),
        in_specs=[tokw, tokw, tokw, tokw,
                  pl.BlockSpec((nb, TS, GDN_HEADS * TS), lambda b, s: (b, s, 0)),
                  pl.BlockSpec((nb, 8, HD), lambda b, s: (b, s, 0)),
                  pl.BlockSpec((nb, TS, HD), lambda b, s: (b, s, P_Z * LANES // HD)),
                  pl.BlockSpec((1, GDN_D), lambda b, s: (0, 0))],
        out_specs=tokw,
        out_shape=jax.ShapeDtypeStruct((B, S, HD), BF16),
        scratch_shapes=[pltpu.VMEM((nb, GDN_HEADS, GDN_D, GDN_D), F32),
                        pltpu.VMEM((nb, GDN_HEADS, TS, GDN_D), BF16),
                        pltpu.VMEM((nb, TS, HD), F32)],
        compiler_params=pltpu.CompilerParams(
            dimension_semantics=("parallel", "arbitrary"), vmem_limit_bytes=VMEM_LIMIT),
        name="gdn_scan",
    )(seq(u), seq(w), seq(qd), seq(kd), seq(attn), egl.reshape(B, ns * 8, HD), seq(plain), gdn_norm_g)
    return out.reshape(B * S, HD)


def _diff_attn_kernel(q_ref, qn_ref, k_ref, v_ref, lq1_ref, lk1_ref, lq2_ref, lk2_ref, gn_ref, o_ref,
                      vext_ref, qz_ref, qnz_ref, sa_ref, sb_ref, sc_ref, sh_ref, m_ref, acc_ref,
                      *, tq, nq, lam_init):
    i = pl.program_id(2)
    dv = 2 * DIFF_DH
    tk = tq // 2

    def stack(src_ref, dst_ref):
        q = src_ref[...]
        map1 = _is_map1_lane(lax.broadcasted_iota(jnp.int32, q.shape, 1))
        zero = jnp.zeros_like(q)
        q1 = jnp.where(map1, q, zero)
        q2 = jnp.where(map1, zero, q)
        for half in range(2):
            rows = slice(half * tk, (half + 1) * tk)
            dst_ref[2 * half * tk:(2 * half + 1) * tk, :] = q1[rows]
            dst_ref[(2 * half + 1) * tk:(2 * half + 2) * tk, :] = q2[rows]

    def scores(j, s_ref, qsrc_ref, r0=0):
        off = pl.multiple_of(j * tk, tk)
        s_ref[...] = _dot_nt(qsrc_ref[r0:, :], k_ref[pl.ds(off, tk), :])

    def consume(j, s_ref, diag=None, r0=0, first=False):
        off = pl.multiple_of(j * tk, tk)
        sc = s_ref[...]
        if diag is not None:
            rr = lax.broadcasted_iota(jnp.int32, sc.shape, 0)
            cc = lax.broadcasted_iota(jnp.int32, sc.shape, 1)
            keep = (rr & (tk - 1)) >= cc
            if diag == "D1":
                keep = keep | (rr >= tq)
            sc = jnp.where(keep, sc, NEG_BIG)
        rows = slice(r0, 2 * tq)
        row_max = jnp.max(sc, axis=-1, keepdims=True)
        if first:
            m_new = jnp.broadcast_to(row_max, (sc.shape[0], LANES))
        else:
            m_prev = m_ref[rows, :]
            m_new = jnp.maximum(m_prev, row_max)
        p = jnp.concatenate([jnp.exp2(sc[:, c0:c0 + LANES] - m_new) for c0 in range(0, tk, LANES)],
                            axis=1)
        pv = _dot(p.astype(BF16), vext_ref[pl.ds(off, tk), :])
        if first:
            acc_ref[rows, :] = pv
        else:
            alpha = jnp.exp2(m_prev - m_new)
            for c0 in range(0, 2 * dv, LANES):
                acc_ref[rows, c0:c0 + LANES] = (alpha * acc_ref[rows, c0:c0 + LANES]
                                                + pv[:, c0:c0 + LANES])
        m_ref[rows, :] = m_new

    def next_d1():
        nxt = jnp.minimum(i + 1, nq - 1)
        stack(qn_ref, qnz_ref)
        scores(2 * nxt, sc_ref, qnz_ref)

    stack(q_ref, qz_ref)

    @pl.when(i == 0)
    def _():
        vext_ref[:, :dv] = v_ref[...]
        vext_ref[:, dv:] = jnp.ones((vext_ref.shape[0], dv), BF16)
        scores(0, sc_ref, qz_ref)

    scores(2 * i + 1, sh_ref, qz_ref, r0=tq)
    consume(2 * i, sc_ref, diag="D1", first=True)

    @pl.when(i == 0)
    def _():
        next_d1()
        consume(1, sh_ref, diag="D2", r0=tq)

    @pl.when(i > 0)
    def _():
        scores(0, sa_ref, qz_ref)
        consume(2 * i + 1, sh_ref, diag="D2", r0=tq)

        def pair(p, carry):
            j = 2 * p
            scores(j + 1, sb_ref, qz_ref)
            consume(j, sa_ref)
            scores(j + 2, sa_ref, qz_ref)
            consume(j + 1, sb_ref)
            return carry

        lax.fori_loop(0, i - 1, pair, 0)
        scores(2 * i - 1, sb_ref, qz_ref)
        consume(2 * i - 2, sa_ref)
        next_d1()
        consume(2 * i - 1, sb_ref)

    lam = (jnp.exp(jnp.sum(lq1_ref[...] * lk1_ref[...], axis=-1, keepdims=True))
           - jnp.exp(jnp.sum(lq2_ref[...] * lk2_ref[...], axis=-1, keepdims=True))
           + lam_init)
    for half in range(2):
        a1 = acc_ref[2 * half * tk:(2 * half + 1) * tk, :]
        a2 = acc_ref[(2 * half + 1) * tk:(2 * half + 2) * tk, :]
        o = a1[:, :dv] / a1[:, dv:] - lam * (a2[:, :dv] / a2[:, dv:])
        o = _rms(o, gn_ref[...]) * (1.0 - lam_init)
        o_ref[half * tk:(half + 1) * tk, :] = o.astype(o_ref.dtype)


def _diff_attn(rope, plain, lq1, lk1, lq2, lk2, diff_norm_g, lam_init, B, S, tq=1024):
    T = B * S
    nq = S // tq
    dv = 2 * DIFF_DH
    lam_spec = pl.BlockSpec((1, DIFF_DH), lambda b, h, i: (0, 0))
    score_buf = pltpu.VMEM((2 * tq, tq // 2), F32)
    stacked_q = pltpu.VMEM((2 * tq, LANES), BF16)
    return pl.pallas_call(
        functools.partial(_diff_attn_kernel, tq=tq, nq=nq, lam_init=lam_init),
        grid=(B, DIFF_HEADS, nq),
        in_specs=[pl.BlockSpec((tq, LANES), lambda b, h, i: (b * nq + i, R_Q + h)),
                  pl.BlockSpec((tq, LANES),
                               lambda b, h, i: (b * nq + jnp.minimum(i + 1, nq - 1), R_Q + h)),
                  pl.BlockSpec((S, LANES), lambda b, h, i: (b, R_K + h)),
                  pl.BlockSpec((S, LANES), lambda b, h, i: (b, P_VB + h)),
                  lam_spec, lam_spec, lam_spec, lam_spec,
                  pl.BlockSpec((1, dv), lambda b, h, i: (0, 0))],
        out_specs=pl.BlockSpec((tq, LANES), lambda b, h, i: (b * nq + i, h)),
        out_shape=jax.ShapeDtypeStruct((T, DIFF_HEADS * dv), BF16),
        scratch_shapes=[pltpu.VMEM((S, 2 * dv), BF16),
                        stacked_q, stacked_q,
                        score_buf, score_buf, score_buf,
                        pltpu.VMEM((tq, tq // 2), F32),
                        pltpu.VMEM((2 * tq, LANES), F32),
                        pltpu.VMEM((2 * tq, 2 * dv), F32)],
        compiler_params=pltpu.CompilerParams(
            dimension_semantics=("parallel", "parallel", "arbitrary"),
            vmem_limit_bytes=VMEM_LIMIT),
        name="diff_attn",
    )(rope, rope, rope, plain, lq1, lk1, lq2, lk2, diff_norm_g)


def _merge_kernel(oa_ref, ob_ref, ga_ref, gb_ref, x_ref, woa_ref, wob_ref, wo_ref, x1_ref):
    ya = _dot(oa_ref[...], woa_ref[...])
    yb = _dot(ob_ref[...], wob_ref[...])
    merged = _sigmoid(ga_ref[...].astype(F32)) * ya + _sigmoid(gb_ref[...].astype(F32)) * yb
    x1_ref[...] = x_ref[...] + _dot(merged.astype(BF16), wo_ref[...])


def _merge(o_a, o_b, plain, x2d, w_out_a, w_out_b, w_o, tm=512):
    T, D = x2d.shape
    tokd = pl.BlockSpec((tm, D), lambda i: (i, 0))
    wspec = pl.BlockSpec((D, D), lambda i: (0, 0))
    return pl.pallas_call(
        _merge_kernel,
        grid=(T // tm,),
        in_specs=[tokd, tokd,
                  pl.BlockSpec((tm, D), lambda i: (i, P_GA * LANES // D)),
                  pl.BlockSpec((tm, D), lambda i: (i, P_GB * LANES // D)),
                  tokd, wspec, wspec, wspec],
        out_specs=tokd,
        out_shape=jax.ShapeDtypeStruct((T, D), F32),
        compiler_params=pltpu.CompilerParams(
            dimension_semantics=("parallel",), vmem_limit_bytes=VMEM_LIMIT),
        name="merge",
    )(o_a, o_b, plain, plain, x2d, w_out_a, w_out_b, w_o)


def _mem_kv_kernel(m_ref, g_ref, w_ref, o_ref):
    o_ref[...] = _dot(_rms(m_ref[...], g_ref[...]).astype(BF16), w_ref[...]).astype(o_ref.dtype)


def _mem_kv(mem2d, g_mem, w_ckv):
    R, D = mem2d.shape
    N = w_ckv.shape[1]
    return pl.pallas_call(
        _mem_kv_kernel,
        grid=(1,),
        in_specs=[pl.BlockSpec((R, D), lambda i: (0, 0)),
                  pl.BlockSpec((1, D), lambda i: (0, 0)),
                  pl.BlockSpec((D, N), lambda i: (0, 0))],
        out_specs=pl.BlockSpec((R, N), lambda i: (0, 0)),
        out_shape=jax.ShapeDtypeStruct((R, N), BF16),
        compiler_params=pltpu.CompilerParams(vmem_limit_bytes=VMEM_LIMIT),
        name="mem_kv",
    )(mem2d, g_mem, w_ckv)


def _cross_kernel(x_ref, g_ref, wq_ref, kv_ref, wo_ref, o_ref):
    x = x_ref[...]
    hx = _rms(x, g_ref[...]).astype(BF16)
    qc = (_dot(hx, wq_ref[...]) * (X_DH ** -0.5)).astype(BF16)
    xw = X_HEADS * X_DH
    outs = []
    for hh in range(X_HEADS):
        cs = slice(hh * X_DH, (hh + 1) * X_DH)
        sc = _dot_nt(qc[:, cs], kv_ref[:, cs])
        p = jnp.exp(sc - jnp.max(sc, axis=-1, keepdims=True))
        p = p / jnp.sum(p, axis=-1, keepdims=True)
        outs.append(_dot(p.astype(BF16), kv_ref[:, xw + hh * X_DH:xw + (hh + 1) * X_DH]))
    oc = jnp.concatenate(outs, axis=1).astype(BF16)
    o_ref[...] = x + _dot(oc, wo_ref[...])


def _cross(x1, g_cross, w_cq, kv, w_co, S, M, tm=512):
    T, D = x1.shape
    xw = X_HEADS * X_DH
    per_b = S // tm
    tokd = pl.BlockSpec((tm, D), lambda i: (i, 0))
    return pl.pallas_call(
        _cross_kernel,
        grid=(T // tm,),
        in_specs=[tokd,
                  pl.BlockSpec((1, D), lambda i: (0, 0)),
                  pl.BlockSpec((D, xw), lambda i: (0, 0)),
                  pl.BlockSpec((M, 2 * xw), lambda i: (i // per_b, 0)),
                  pl.BlockSpec((xw, D), lambda i: (0, 0))],
        out_specs=tokd,
        out_shape=jax.ShapeDtypeStruct((T, D), F32),
        compiler_params=pltpu.CompilerParams(
            dimension_semantics=("parallel",), vmem_limit_bytes=VMEM_LIMIT),
        name="cross",
    )(x1, g_cross, w_cq, kv, w_co)


def _ffn_kernel(x_ref, g_ref, wi_ref, wo_ref, gf_ref, o_ref, *, tf):
    x = x_ref[...]
    hb = _rms(x, g_ref[...]).astype(BF16)
    F = wo_ref.shape[0]
    acc = x
    for f in range(F // tf):
        act = _silu(_dot(hb, wi_ref[:, f * tf:(f + 1) * tf])) * _dot(hb, wi_ref[:, F + f * tf:F + (f + 1) * tf])
        acc = acc + _dot(act.astype(BF16), wo_ref[f * tf:(f + 1) * tf, :])
    o_ref[...] = _rms(acc, gf_ref[...])


def _ffn(x2, g_ffn, w_ffn_in, w_ffn_out, g_final, tm=512, tf=256):
    T, D = x2.shape
    F = w_ffn_out.shape[0]
    tokd = pl.BlockSpec((tm, D), lambda i: (i, 0))
    vec = pl.BlockSpec((1, D), lambda i: (0, 0))
    resident = pl.Buffered(1)
    return pl.pallas_call(
        functools.partial(_ffn_kernel, tf=tf),
        grid=(T // tm,),
        in_specs=[tokd, vec,
                  pl.BlockSpec((D, 2 * F), lambda i: (0, 0), pipeline_mode=resident),
                  pl.BlockSpec((F, D), lambda i: (0, 0), pipeline_mode=resident),
                  vec],
        out_specs=tokd,
        out_shape=jax.ShapeDtypeStruct((T, D), F32),
        compiler_params=pltpu.CompilerParams(
            dimension_semantics=("parallel",), vmem_limit_bytes=VMEM_LIMIT),
        name="ffn",
    )(x2, g_ffn, w_ffn_in, w_ffn_out, g_final)


def _pad_lanes(v):
    v = v.reshape(1, -1).astype(F32)
    return jnp.pad(v, ((0, 0), (0, LANES - v.shape[1])))


def kernel(x, mem, positions, g_mix, w_in, conv_w, a_log, dt_bias, gdn_norm_g, lambda_q1, lambda_k1, lambda_q2, lambda_k2, diff_norm_g, w_branch_gate, w_out_a, w_out_b, w_o, g_cross, g_mem, w_cq, w_ckv, w_co, g_ffn, w_ffn_in, w_ffn_out, g_final):
    B, S, D = x.shape
    M = mem.shape[1]
    depth = w_in.shape[0]
    assert depth == 1, "the final rmsnorm is fused into the (single) layer's ffn call"
    qkvz = 4 * GDN_HEADS * GDN_D
    xs = x.reshape(B * S, D)
    tables = _rope_tables(positions)
    kv = None
    for l in range(depth):
        ab_end = qkvz + 2 * GDN_HEADS
        qk_end = ab_end + 2 * DIFF_HEADS * 2 * DIFF_DH
        w16 = w_in[l].astype(BF16)
        w_plain = jnp.concatenate([w16[:, :qkvz], w16[:, qk_end:], w_branch_gate[l].astype(BF16)], axis=1)
        w_ab = jnp.pad(w16[:, qkvz:ab_end], ((0, 0), (0, LANES - 2 * GDN_HEADS)))
        plain, ab, h = _in_proj(xs, g_mix[l].reshape(1, D), w_plain, w_ab)
        src = jnp.asarray([hh * LANES + c for hh in range(2 * DIFF_HEADS) for c in _head_lane_source()],
                          dtype=jnp.int32)
        w_qk = jnp.take(w16[:, ab_end:qk_end], src, axis=1)
        rope = _rope_proj(h, w_qk, tables)

        u, w, qd, kd, attn, egl = _gdn_prep(plain, ab, conv_w[l], _pad_lanes(a_log[l]),
                                            _pad_lanes(dt_bias[l]), B, S)
        o_a = _gdn_scan(u, w, qd, kd, attn, egl, plain, gdn_norm_g[l].reshape(1, GDN_D), B, S)

        lam_init = 0.8 - 0.6 * math.exp(-0.3 * l)
        o_b = _diff_attn(rope, plain, lambda_q1[l].reshape(1, -1), lambda_k1[l].reshape(1, -1),
                         lambda_q2[l].reshape(1, -1), lambda_k2[l].reshape(1, -1),
                         diff_norm_g[l].reshape(1, -1), lam_init, B, S)

        xs = _merge(o_a, o_b, plain, xs,w_out_a[l].astype(BF16), w_out_b[l].astype(BF16),
                    w_o[l].astype(BF16))

        kv = _mem_kv(mem.reshape(B * M, D), g_mem[l].reshape(1, D), w_ckv[l].astype(BF16))
        xs = _cross(xs, g_cross[l].reshape(1, D), w_cq[l].astype(BF16), kv, w_co[l].astype(BF16), S, M)

        xs = _ffn(xs, g_ffn[l].reshape(1, D), w_ffn_in[l].astype(BF16), w_ffn_out[l].astype(BF16),
                  g_final.reshape(1, D))
    return xs.reshape(B, S, D)
```

```python
import functools
import math

import jax
import jax.numpy as jnp
from jax import lax
from jax.experimental import pallas as pl
from jax.experimental.pallas import tpu as pltpu

F32 = jnp.float32
BF16 = jnp.bfloat16

EPS = 1e-6
LANES = 128
GDN_HEADS = 8
GDN_D = 128
CONV_WIDTH = 4
CHUNK = 64
DIFF_HEADS = 8
DIFF_DH = 64
ROPE_DIM = DIFF_DH // 4
ROPE_THETA = 500000.0
X_HEADS = 4
X_DH = 128
NEG_BIG = -1e30
LOG2E = math.log2(math.e)

VMEM_LIMIT = 48 * 1024 * 1024

P_QA, P_KA, P_VA, P_Z, P_VB, P_GA, P_GB, P_END = 0, 8, 16, 24, 32, 40, 48, 56
R_Q, R_K = 0, 8


def _dot(a, b):
    return jnp.dot(a, b, preferred_element_type=F32)


def _dot_nt(a, b):
    return lax.dot_general(a, b, (((1,), (1,)), ((), ())), preferred_element_type=F32)


def _mm(a, b):
    return _dot(a.astype(BF16), b.astype(BF16))


def _split3(a):
    hi = a.astype(BF16)
    r = a - hi.astype(F32)
    mid = r.astype(BF16)
    lo = (r - mid.astype(F32)).astype(BF16)
    return hi, mid, lo


def _dot_mask(mask_bf16, b):
    hi, mid, lo = _split3(b)
    return _dot(mask_bf16, hi) + (_dot(mask_bf16, mid) + _dot(mask_bf16, lo))


def _dot_rmask(b, mask_bf16):
    hi, mid, lo = _split3(b)
    return _dot(hi, mask_bf16) + (_dot(mid, mask_bf16) + _dot(lo, mask_bf16))


def _rms(x, g):
    ms = jnp.mean(x * x, axis=-1, keepdims=True)
    return x * lax.rsqrt(ms + EPS) * g


def _sigmoid(x):
    return 0.5 * jnp.tanh(0.5 * x) + 0.5


def _silu(x):
    half = 0.5 * x
    return half * jnp.tanh(half) + half


def _softplus(x):
    return jnp.maximum(x, 0.0) + jnp.log(1.0 + jnp.exp(-jnp.abs(x)))


ROPE_HALF = ROPE_DIM // 2
HALF_LANES = LANES // 2


def _head_lane_source():
    src = [0] * LANES
    for m in range(2):
        for d in range(DIFF_DH):
            if d < ROPE_HALF:
                lane = m * ROPE_HALF + d
            elif d < ROPE_DIM:
                lane = HALF_LANES + m * ROPE_HALF + (d - ROPE_HALF)
            else:
                lane = (ROPE_DIM if m == 0 else HALF_LANES + ROPE_DIM) + (d - ROPE_DIM)
            src[lane] = m * DIFF_DH + d
    return src


def _is_map1_lane(lane):
    return (lane < ROPE_HALF) | ((lane >= ROPE_DIM) & (lane < HALF_LANES + ROPE_HALF))


ROPE_GROUPS = LANES // ROPE_HALF


def _rope_table_kernel(pos_ref, inv_ref, c_ref, sg_ref):
    ang = pos_ref[...] * inv_ref[...]
    cos = jnp.cos(ang)
    sin = jnp.sin(ang)
    rows = ang.shape[0]
    lane = lax.broadcasted_iota(jnp.int32, ang.shape, 1)
    first = lane < ROPE_DIM
    second = (lane >= HALF_LANES) & (lane < HALF_LANES + ROPE_DIM)
    freq = lane & (ROPE_HALF - 1)
    for a in range(ROPE_GROUPS):
        idx = freq + a * ROPE_HALF
        cos_a = jnp.take_along_axis(cos, idx, axis=1)
        sin_a = jnp.take_along_axis(sin, idx, axis=1)
        c_ref[a * rows:(a + 1) * rows, :] = jnp.where(first | second, cos_a, 1.0)
        sg_ref[a * rows:(a + 1) * rows, :] = jnp.where(first, -sin_a, jnp.where(second, sin_a, 0.0))


def _rope_tables(positions, tm=1024):
    T = positions.size
    rows = tm // ROPE_GROUPS
    pos = positions.astype(F32).reshape(T // tm, ROPE_GROUPS, rows).transpose(0, 2, 1)
    pos = jnp.repeat(pos, ROPE_HALF, axis=2).reshape(T // ROPE_GROUPS, LANES)
    inv_freq = ROPE_THETA ** (-jnp.arange(0, ROPE_DIM, 2, dtype=F32) / ROPE_DIM)
    inv = jnp.tile(inv_freq, ROPE_GROUPS).reshape(1, LANES)
    out = jax.ShapeDtypeStruct((T, LANES), F32)
    return pl.pallas_call(
        _rope_table_kernel,
        grid=(T // tm,),
        in_specs=[pl.BlockSpec((rows, LANES), lambda i: (i, 0)),
                  pl.BlockSpec((1, LANES), lambda i: (0, 0))],
        out_specs=[pl.BlockSpec((tm, LANES), lambda i: (i, 0))] * 2,
        out_shape=[out, out],
        name="rope_tables",
    )(pos, inv)


PROJ_SUB = 256


def _in_proj_kernel(x_ref, g_ref, w_ref, wab_ref, out_ref, ab_ref, h_ref):
    @pl.when(pl.program_id(1) == 0)
    def _():
        hb = _rms(x_ref[...], g_ref[...]).astype(BF16)
        h_ref[...] = hb
        ab_ref[...] = _dot(hb, wab_ref[...])

    hb = h_ref[...]
    for c0 in range(0, out_ref.shape[1], PROJ_SUB):
        cs = slice(c0, c0 + PROJ_SUB)
        out_ref[:, cs] = _dot(hb, w_ref[:, cs]).astype(out_ref.dtype)


def _in_proj(x2d, g_mix, w_plain, w_ab, tm=1024, tn=1792):
    T, D = x2d.shape
    N = w_plain.shape[1]
    tm = min(tm, T)
    return pl.pallas_call(
        _in_proj_kernel,
        grid=(T // tm, N // tn),
        in_specs=[pl.BlockSpec((tm, D), lambda i, j: (i, 0)),
                  pl.BlockSpec((1, D), lambda i, j: (0, 0)),
                  pl.BlockSpec((D, tn), lambda i, j: (0, j)),
                  pl.BlockSpec((D, LANES), lambda i, j: (0, 0))],
        out_specs=[pl.BlockSpec((tm, tn), lambda i, j: (i, j)),
                   pl.BlockSpec((tm, LANES), lambda i, j: (i, 0)),
                   pl.BlockSpec((tm, D), lambda i, j: (i, 0))],
        out_shape=[jax.ShapeDtypeStruct((T, N), BF16),
                   jax.ShapeDtypeStruct((T, LANES), F32),
                   jax.ShapeDtypeStruct((T, D), BF16)],
        compiler_params=pltpu.CompilerParams(
            dimension_semantics=("parallel", "arbitrary"), vmem_limit_bytes=VMEM_LIMIT),
        name="in_proj",
    )(x2d, g_mix, w_plain, w_ab)


def _rope_proj_kernel(h_ref, w_ref, c_ref, sg_ref, out_ref):
    scale = jnp.where(pl.program_id(1) == 0, DIFF_DH ** -0.5 * LOG2E, 1.0)
    c = c_ref[...] * scale
    sg = sg_ref[...] * scale
    hb = h_ref[...]
    for c0 in range(0, out_ref.shape[1], PROJ_SUB):
        acc = _dot(hb, w_ref[:, c0:c0 + PROJ_SUB])
        for l0 in range(0, PROJ_SUB, LANES):
            a = acc[:, l0:l0 + LANES]
            y = a * c + pltpu.roll(a, HALF_LANES, 1) * sg
            out_ref[:, c0 + l0:c0 + l0 + LANES] = y.astype(out_ref.dtype)


def _rope_proj(h, w_qk, tables, tm=1024):
    T, D = h.shape
    N = w_qk.shape[1]
    tn = N // 2
    tm = min(tm, T)
    return pl.pallas_call(
        _rope_proj_kernel,
        grid=(T // tm, 2),
        in_specs=[pl.BlockSpec((tm, D), lambda i, j: (i, 0)),
                  pl.BlockSpec((D, tn), lambda i, j: (0, j)),
                  pl.BlockSpec((tm, LANES), lambda i, j: (i, 0)),
                  pl.BlockSpec((tm, LANES), lambda i, j: (i, 0))],
        out_specs=pl.BlockSpec((tm, tn), lambda i, j: (i, j)),
        out_shape=jax.ShapeDtypeStruct((T, N), BF16),
        compiler_params=pltpu.CompilerParams(
            dimension_semantics=("parallel", "parallel"), vmem_limit_bytes=VMEM_LIMIT),
        name="rope_proj",
    )(h, w_qk, *tables)


GDN_TS = 4 * CHUNK
GDN_HEAD_GROUP = 4
SOLVE_BLOCK = 16
CHUNK_SHIFT = CHUNK.bit_length() - 1
SOLVE_SHIFT = SOLVE_BLOCK.bit_length() - 1
assert (1 << CHUNK_SHIFT) == CHUNK and (1 << SOLVE_SHIFT) == SOLVE_BLOCK and CHUNK // SOLVE_BLOCK == 4


def _gdn_prep_kernel(qkv_ref, ab_ref, cw_ref, alog_ref, dtb_ref,
                     u_ref, w_ref, qd_ref, kd_ref, attn_ref, egl_ref, halo_ref):
    TS, H, D = GDN_TS, GDN_HEADS, GDN_D

    @pl.when(pl.program_id(1) == 0)
    def _():
        halo_ref[...] = jnp.zeros(halo_ref.shape, F32)

    def conv_silu(c0):
        cs = slice(c0, c0 + D)
        ext = jnp.concatenate([halo_ref[:, cs], qkv_ref[:, cs].astype(F32)], axis=0)
        z = ext * cw_ref[0:1, cs]
        for t in range(1, CONV_WIDTH):
            z = pltpu.roll(z, 1, 0) + ext * cw_ref[t:t + 1, cs]
        return _silu(z[8:])

    def l2_normalised(x):
        return x * lax.rsqrt(jnp.sum(x * x, axis=-1, keepdims=True) + EPS)

    ri = lax.broadcasted_iota(jnp.int32, (TS, TS), 0)
    ci = lax.broadcasted_iota(jnp.int32, (TS, TS), 1)
    same = (ri >> CHUNK_SHIFT) == (ci >> CHUNK_SHIFT)
    incl = same & (ri >= ci)
    strict = same & (ri > ci)
    blk = (ri >> SOLVE_SHIFT) == (ci >> SOLVE_SHIFT)
    eye = jnp.where(ri == ci, 1.0, 0.0)

    ab = ab_ref[...]
    g_all = -jnp.exp(alog_ref[...]) * _softplus(ab + dtb_ref[...])
    b_all = _sigmoid(ab)
    gc_all = _dot_mask(jnp.where(incl, 1.0, 0.0).astype(BF16), g_all)
    gl_all = _dot_mask(jnp.where(same, 1.0, 0.0).astype(BF16), g_all)
    gc_t = _dot_rmask(g_all.T, jnp.where(same & (ri <= ci), 1.0, 0.0).astype(BF16))

    r8 = lax.broadcasted_iota(jnp.int32, (8, TS), 0)
    c8 = lax.broadcasted_iota(jnp.int32, (8, TS), 1)
    sel = jnp.where(c8 == r8 * CHUNK, 1.0, 0.0).astype(BF16)
    gl8 = _dot_mask(sel, gl_all)

    def head_group(hs):
        ld, lo, x, rhs = {}, {}, {}, {}
        for h in hs:
            cs = slice(h * D, (h + 1) * D)
            q = l2_normalised(conv_silu(h * D)) * (D ** -0.5)
            k = l2_normalised(conv_silu((H + h) * D))
            v = conv_silu((2 * H + h) * D)
            gc_col = gc_all[:, h:h + 1]
            gl_col = gl_all[:, h:h + 1]
            beta = b_all[:, H + h:H + h + 1]
            decay = jnp.exp(jnp.minimum(gc_col - gc_t[h:h + 1, :], 0.0))
            eg = jnp.exp(gc_col)
            kb = k * beta
            kbf = k.astype(BF16)
            lmat = jnp.where(strict, _dot_nt(kb.astype(BF16), kbf) * decay, 0.0)
            attn = jnp.where(incl, _dot_nt(q.astype(BF16), kbf) * decay, 0.0)
            attn_ref[:, h * TS:(h + 1) * TS] = attn.astype(attn_ref.dtype)
            qd_ref[:, cs] = (q * eg).astype(qd_ref.dtype)
            kd_ref[:, cs] = (k * jnp.exp(gl_col - gc_col)).astype(kd_ref.dtype)
            egl_ref[:, cs] = jnp.exp(jnp.broadcast_to(gl8[:, h:h + 1], (8, D)))
            ldh = jnp.where(blk, lmat, 0.0)
            ld[h] = ldh.astype(BF16)
            lo[h] = (lmat - ldh).astype(BF16)
            x[h] = eye - ldh
            rhs[h] = jnp.concatenate([v * beta, kb * eg], axis=1).astype(BF16)

        p = {h: _dot(ld[h], ld[h]) for h in hs}
        for _ in range(2):
            pb = {h: p[h].astype(BF16) for h in hs}
            x = {h: x[h] + _mm(x[h], pb[h]) for h in hs}
            p = {h: _dot(pb[h], pb[h]) for h in hs}
        td = {h: x[h] + _mm(x[h], p[h]) for h in hs}
        tdb = {h: td[h].astype(BF16) for h in hs}
        n = {h: _dot(tdb[h], lo[h]).astype(BF16) for h in hs}
        n2 = {h: _dot(n[h], n[h]) for h in hs}
        m1 = {h: td[h] + _mm(n2[h], tdb[h]) for h in hs}
        tinv = {h: m1[h] - _mm(n[h], m1[h]) for h in hs}
        for h in hs:
            cs = slice(h * D, (h + 1) * D)
            uw = _mm(tinv[h], rhs[h])
            u_ref[:, cs] = uw[:, :D].astype(u_ref.dtype)
            w_ref[:, cs] = uw[:, D:].astype(w_ref.dtype)

    for g0 in range(0, H, GDN_HEAD_GROUP):
        head_group(range(g0, g0 + GDN_HEAD_GROUP))
    halo_ref[...] = qkv_ref[TS - 8:TS, :].astype(F32)


def _gdn_prep(plain, ab, conv_w, a_log, dt_bias, B, S):
    T = B * S
    TS = GDN_TS
    ns = S // TS
    HD = GDN_HEADS * GDN_D
    row = lambda b, s: b * ns + s
    vec = pl.BlockSpec((1, LANES), lambda b, s: (0, 0))
    tokw = pl.BlockSpec((TS, HD), lambda b, s: (row(b, s), 0))
    act = jax.ShapeDtypeStruct((T, HD), BF16)
    return pl.pallas_call(
        _gdn_prep_kernel,
        grid=(B, ns),
        in_specs=[pl.BlockSpec((TS, 3 * HD), lambda b, s: (row(b, s), 0)),
                  pl.BlockSpec((TS, LANES), lambda b, s: (row(b, s), 0)),
                  pl.BlockSpec((CONV_WIDTH, 3 * HD), lambda b, s: (0, 0)),
                  vec, vec],
        out_specs=[tokw, tokw, tokw, tokw,
                   pl.BlockSpec((TS, GDN_HEADS * TS), lambda b, s: (row(b, s), 0)),
                   pl.BlockSpec((8, HD), lambda b, s: (row(b, s), 0))],
        out_shape=[act, act, act, act,
                   jax.ShapeDtypeStruct((T, GDN_HEADS * TS), BF16),
                   jax.ShapeDtypeStruct((B * ns * 8, HD), F32)],
        scratch_shapes=[pltpu.VMEM((8, 3 * HD), F32)],
        compiler_params=pltpu.CompilerParams(
            dimension_semantics=("parallel", "arbitrary"), vmem_limit_bytes=VMEM_LIMIT),
        name="gdn_prep",
    )(plain, ab, conv_w, a_log, dt_bias)


def _gdn_scan_kernel(u_ref, w_ref, qd_ref, kd_ref, attn_ref, egl_ref, z_ref, gn_ref, o_ref,
                     state_ref, vnew_ref, oq_ref):
    TS = GDN_TS
    D = GDN_D
    nb = u_ref.shape[0]
    s = pl.program_id(1)

    @pl.when(s == 0)
    def _():
        state_ref[...] = jnp.zeros(state_ref.shape, F32)

    for c in range(TS // CHUNK):
        r = slice(c * CHUNK, (c + 1) * CHUNK)
        for b in range(nb):
            for h in range(GDN_HEADS):
                cs = slice(h * D, (h + 1) * D)
                st = state_ref[b, h]
                wq = jnp.concatenate([w_ref[b, r, cs], qd_ref[b, r, cs]], axis=0)
                ws = _dot(wq, st.astype(BF16))
                v_new = (u_ref[b, r, cs].astype(F32) - ws[:CHUNK]).astype(BF16)
                vnew_ref[b, h, r, :] = v_new
                oq_ref[b, r, cs] = ws[CHUNK:]
                ktv = lax.dot_general(kd_ref[b, r, cs], v_new, (((0,), (0,)), ((), ())),
                                      preferred_element_type=F32)
                state_ref[b, h] = st * egl_ref[b, c:c + 1, cs] + ktv

    for b in range(nb):
        for h in range(GDN_HEADS):
            cs = slice(h * D, (h + 1) * D)
            o = oq_ref[b, :, cs] + _dot(attn_ref[b, :, h * TS:(h + 1) * TS], vnew_ref[b, h])
            o = _rms(o, gn_ref[...]) * _silu(z_ref[b, :, cs].astype(F32))
            o_ref[b, :, cs] = o.astype(o_ref.dtype)


def _gdn_scan(u, w, qd, kd, attn, egl, plain, gdn_norm_g, B, S, nb=4):
    TS = GDN_TS
    ns = S // TS
    HD = GDN_HEADS * GDN_D
    nb = min(nb, B)
    assert B % nb == 0
    seq = lambda a: a.reshape(B, S, a.shape[-1])
    tokw = pl.BlockSpec((nb, TS, HD), lambda b, s: (b, s, 0))
    out = pl.pallas_call(
        _gdn_scan_kernel,
        grid=(B // nb, ns),
        in_specs=[tokw, tokw, tokw, tokw,
                  pl.BlockSpec((nb, TS, GDN_HEADS * TS), lambda b, s: (b, s, 0)),
                  pl.BlockSpec((nb, 8, HD), lambda b, s: (b, s, 0)),
                  pl.BlockSpec((nb, TS, HD), lambda b, s: (b, s, P_Z * LANES // HD)),
                  pl.BlockSpec((1, GDN_D), lambda b, s: (0, 0))],
        out_specs=tokw,
        out_shape=jax.ShapeDtypeStruct((B, S, HD), BF16),
        scratch_shapes=[pltpu.VMEM((nb, GDN_HEADS, GDN_D, GDN_D), F32),
                        pltpu.VMEM((nb, GDN_HEADS, TS, GDN_D), BF16),
                        pltpu.VMEM((nb, TS, HD), F32)],
        compiler_params=pltpu.CompilerParams(
            dimension_semantics=("parallel", "arbitrary"), vmem_limit_bytes=VMEM_LIMIT),
        name="gdn_scan",
    )(seq(u), seq(w), seq(qd), seq(kd), seq(attn), egl.reshape(B, ns * 8, HD), seq(plain), gdn_norm_g)
    return out.reshape(B * S, HD)


def _diff_attn_kernel(q_ref, qn_ref, k_ref, v_ref, lq1_ref, lk1_ref, lq2_ref, lk2_ref, gn_ref, o_ref,
                      vext_ref, qz_ref, qnz_ref, sa_ref, sb_ref, sc_ref, sh_ref, m_ref, acc_ref,
                      *, tq, nq, lam_init):
    i = pl.program_id(2)
    dv = 2 * DIFF_DH
    tk = tq // 2

    def stack(src_ref, dst_ref):
        q = src_ref[...]
        map1 = _is_map1_lane(lax.broadcasted_iota(jnp.int32, q.shape, 1))
        zero = jnp.zeros_like(q)
        q1 = jnp.where(map1, q, zero)
        q2 = jnp.where(map1, zero, q)
        for half in range(2):
            rows = slice(half * tk, (half + 1) * tk)
            dst_ref[2 * half * tk:(2 * half + 1) * tk, :] = q1[rows]
            dst_ref[(2 * half + 1) * tk:(2 * half + 2) * tk, :] = q2[rows]

    def scores(j, s_ref, qsrc_ref, r0=0):
        off = pl.multiple_of(j * tk, tk)
        s_ref[...] = _dot_nt(qsrc_ref[r0:, :], k_ref[pl.ds(off, tk), :])

    def consume(j, s_ref, diag=None, r0=0, first=False):
        off = pl.multiple_of(j * tk, tk)
        sc = s_ref[...]
        if diag is not None:
            rr = lax.broadcasted_iota(jnp.int32, sc.shape, 0)
            cc = lax.broadcasted_iota(jnp.int32, sc.shape, 1)
            keep = (rr & (tk - 1)) >= cc
            if diag == "D1":
                keep = keep | (rr >= tq)
            sc = jnp.where(keep, sc, NEG_BIG)
        rows = slice(r0, 2 * tq)
        row_max = jnp.max(sc, axis=-1, keepdims=True)
        if first:
            m_new = jnp.broadcast_to(row_max, (sc.shape[0], LANES))
        else:
            m_prev = m_ref[rows, :]
            m_new = jnp.maximum(m_prev, row_max)
        p = jnp.concatenate([jnp.exp2(sc[:, c0:c0 + LANES] - m_new) for c0 in range(0, tk, LANES)],
                            axis=1)
        pv = _dot(p.astype(BF16), vext_ref[pl.ds(off, tk), :])
        if first:
            acc_ref[rows, :] = pv
        else:
            alpha = jnp.exp2(m_prev - m_new)
            for c0 in range(0, 2 * dv, LANES):
                acc_ref[rows, c0:c0 + LANES] = (alpha * acc_ref[rows, c0:c0 + LANES]
                                                + pv[:, c0:c0 + LANES])
        m_ref[rows, :] = m_new

    def next_d1():
        nxt = jnp.minimum(i + 1, nq - 1)
        stack(qn_ref, qnz_ref)
        scores(2 * nxt, sc_ref, qnz_ref)

    stack(q_ref, qz_ref)

    @pl.when(i == 0)
    def _():
        vext_ref[:, :dv] = v_ref[...]
        vext_ref[:, dv:] = jnp.ones((vext_ref.shape[0], dv), BF16)
        scores(0, sc_ref, qz_ref)

    scores(2 * i + 1, sh_ref, qz_ref, r0=tq)
    consume(2 * i, sc_ref, diag="D1", first=True)

    @pl.when(i == 0)
    def _():
        next_d1()
        consume(1, sh_ref, diag="D2", r0=tq)

    @pl.when(i > 0)
    def _():
        scores(0, sa_ref, qz_ref)
        consume(2 * i + 1, sh_ref, diag="D2", r0=tq)

        def pair(p, carry):
            j = 2 * p
            scores(j + 1, sb_ref, qz_ref)
            consume(j, sa_ref)
            scores(j + 2, sa_ref, qz_ref)
            consume(j + 1, sb_ref)
            return carry

        lax.fori_loop(0, i - 1, pair, 0)
        scores(2 * i - 1, sb_ref, qz_ref)
        consume(2 * i - 2, sa_ref)
        next_d1()
        consume(2 * i - 1, sb_ref)

    lam = (jnp.exp(jnp.sum(lq1_ref[...] * lk1_ref[...], axis=-1, keepdims=True))
           - jnp.exp(jnp.sum(lq2_ref[...] * lk2_ref[...], axis=-1, keepdims=True))
           + lam_init)
    for half in range(2):
        a1 = acc_ref[2 * half * tk:(2 * half + 1) * tk, :]
        a2 = acc_ref[(2 * half + 1) * tk:(2 * half + 2) * tk, :]
        o = a1[:, :dv] / a1[:, dv:] - lam * (a2[:, :dv] / a2[:, dv:])
        o = _rms(o, gn_ref[...]) * (1.0 - lam_init)
        o_ref[half * tk:(half + 1) * tk, :] = o.astype(o_ref.dtype)


def _diff_attn(rope, plain, lq1, lk1, lq2, lk2, diff_norm_g, lam_init, B, S, tq=1024):
    T = B * S
    nq = S // tq
    dv = 2 * DIFF_DH
    lam_spec = pl.BlockSpec((1, DIFF_DH), lambda b, h, i: (0, 0))
    score_buf = pltpu.VMEM((2 * tq, tq // 2), F32)
    stacked_q = pltpu.VMEM((2 * tq, LANES), BF16)
    return pl.pallas_call(
        functools.partial(_diff_attn_kernel, tq=tq, nq=nq, lam_init=lam_init),
        grid=(B, DIFF_HEADS, nq),
        in_specs=[pl.BlockSpec((tq, LANES), lambda b, h, i: (b * nq + i, R_Q + h)),
                  pl.BlockSpec((tq, LANES),
                               lambda b, h, i: (b * nq + jnp.minimum(i + 1, nq - 1), R_Q + h)),
                  pl.BlockSpec((S, LANES), lambda b, h, i: (b, R_K + h)),
                  pl.BlockSpec((S, LANES), lambda b, h, i: (b, P_VB + h)),
                  lam_spec, lam_spec, lam_spec, lam_spec,
                  pl.BlockSpec((1, dv), lambda b, h, i: (0, 0))],
        out_specs=pl.BlockSpec((tq, LANES), lambda b, h, i: (b * nq + i, h)),
        out_shape=jax.ShapeDtypeStruct((T, DIFF_HEADS * dv), BF16),
        scratch_shapes=[pltpu.VMEM((S, 2 * dv), BF16),
                        stacked_q, stacked_q,
                        score_buf, score_buf, score_buf,
                        pltpu.VMEM((tq, tq // 2), F32),
                        pltpu.VMEM((2 * tq, LANES), F32),
                        pltpu.VMEM((2 * tq, 2 * dv), F32)],
        compiler_params=pltpu.CompilerParams(
            dimension_semantics=("parallel", "parallel", "arbitrary"),
            vmem_limit_bytes=VMEM_LIMIT),
        name="diff_attn",
    )(rope, rope, rope, plain, lq1, lk1, lq2, lk2, diff_norm_g)


def _merge_kernel(oa_ref, ob_ref, ga_ref, gb_ref, x_ref, woa_ref, wob_ref, wo_ref, x1_ref):
    ya = _dot(oa_ref[...], woa_ref[...])
    yb = _dot(ob_ref[...], wob_ref[...])
    merged = _sigmoid(ga_ref[...].astype(F32)) * ya + _sigmoid(gb_ref[...].astype(F32)) * yb
    x1_ref[...] = x_ref[...] + _dot(merged.astype(BF16), wo_ref[...])


def _merge(o_a, o_b, plain, x2d, w_out_a, w_out_b, w_o, tm=512):
    T, D = x2d.shape
    tokd = pl.BlockSpec((tm, D), lambda i: (i, 0))
    wspec = pl.BlockSpec((D, D), lambda i: (0, 0))
    return pl.pallas_call(
        _merge_kernel,
        grid=(T // tm,),
        in_specs=[tokd, tokd,
                  pl.BlockSpec((tm, D), lambda i: (i, P_GA * LANES // D)),
                  pl.BlockSpec((tm, D), lambda i: (i, P_GB * LANES // D)),
                  tokd, wspec, wspec, wspec],
        out_specs=tokd,
        out_shape=jax.ShapeDtypeStruct((T, D), F32),
        compiler_params=pltpu.CompilerParams(
            dimension_semantics=("parallel",), vmem_limit_bytes=VMEM_LIMIT),
        name="merge",
    )(o_a, o_b, plain, plain, x2d, w_out_a, w_out_b, w_o)


def _mem_kv_kernel(m_ref, g_ref, w_ref, o_ref):
    o_ref[...] = _dot(_rms(m_ref[...], g_ref[...]).astype(BF16), w_ref[...]).astype(o_ref.dtype)


def _mem_kv(mem2d, g_mem, w_ckv):
    R, D = mem2d.shape
    N = w_ckv.shape[1]
    return pl.pallas_call(
        _mem_kv_kernel,
        grid=(1,),
        in_specs=[pl.BlockSpec((R, D), lambda i: (0, 0)),
                  pl.BlockSpec((1, D), lambda i: (0, 0)),
                  pl.BlockSpec((D, N), lambda i: (0, 0))],
        out_specs=pl.BlockSpec((R, N), lambda i: (0, 0)),
        out_shape=jax.ShapeDtypeStruct((R, N), BF16),
        compiler_params=pltpu.CompilerParams(vmem_limit_bytes=VMEM_LIMIT),
        name="mem_kv",
    )(mem2d, g_mem, w_ckv)


def _cross_kernel(x_ref, g_ref, wq_ref, kv_ref, wo_ref, o_ref):
    x = x_ref[...]
    hx = _rms(x, g_ref[...]).astype(BF16)
    qc = (_dot(hx, wq_ref[...]) * (X_DH ** -0.5)).astype(BF16)
    xw = X_HEADS * X_DH
    outs = []
    for hh in range(X_HEADS):
        cs = slice(hh * X_DH, (hh + 1) * X_DH)
        sc = _dot_nt(qc[:, cs], kv_ref[:, cs])
        p = jnp.exp(sc - jnp.max(sc, axis=-1, keepdims=True))
        p = p / jnp.sum(p, axis=-1, keepdims=True)
        outs.append(_dot(p.astype(BF16), kv_ref[:, xw + hh * X_DH:xw + (hh + 1) * X_DH]))
    oc = jnp.concatenate(outs, axis=1).astype(BF16)
    o_ref[...] = x + _dot(oc, wo_ref[...])


def _cross(x1, g_cross, w_cq, kv, w_co, S, M, tm=512):
    T, D = x1.shape
    xw = X_HEADS * X_DH
    per_b = S // tm
    tokd = pl.BlockSpec((tm, D), lambda i: (i, 0))
    return pl.pallas_call(
        _cross_kernel,
        grid=(T // tm,),
        in_specs=[tokd,
                  pl.BlockSpec((1, D), lambda i: (0, 0)),
                  pl.BlockSpec((D, xw), lambda i: (0, 0)),
                  pl.BlockSpec((M, 2 * xw), lambda i: (i // per_b, 0)),
                  pl.BlockSpec((xw, D), lambda i: (0, 0))],
        out_specs=tokd,
        out_shape=jax.ShapeDtypeStruct((T, D), F32),
        compiler_params=pltpu.CompilerParams(
            dimension_semantics=("parallel",), vmem_limit_bytes=VMEM_LIMIT),
        name="cross",
    )(x1, g_cross, w_cq, kv, w_co)


def _ffn_kernel(x_ref, g_ref, wi_ref, wo_ref, gf_ref, o_ref, *, tf):
    x = x_ref[...]
    hb = _rms(x, g_ref[...]).astype(BF16)
    F = wo_ref.shape[0]
    acc = x
    for f in range(F // tf):
        act = _silu(_dot(hb, wi_ref[:, f * tf:(f + 1) * tf])) * _dot(hb, wi_ref[:, F + f * tf:F + (f + 1) * tf])
        acc = acc + _dot(act.astype(BF16), wo_ref[f * tf:(f + 1) * tf, :])
    o_ref[...] = _rms(acc, gf_ref[...])


def _ffn(x2, g_ffn, w_ffn_in, w_ffn_out, g_final, tm=512, tf=256):
    T, D = x2.shape
    F = w_ffn_out.shape[0]
    tokd = pl.BlockSpec((tm, D), lambda i: (i, 0))
    vec = pl.BlockSpec((1, D), lambda i: (0, 0))
    resident = pl.Buffered(1)
    return pl.pallas_call(
        functools.partial(_ffn_kernel, tf=tf),
        grid=(T // tm,),
        in_specs=[tokd, vec,
                  pl.BlockSpec((D, 2 * F), lambda i: (0, 0), pipeline_mode=resident),
                  pl.BlockSpec((F, D), lambda i: (0, 0), pipeline_mode=resident),
                  vec],
        out_specs=tokd,
        out_shape=jax.ShapeDtypeStruct((T, D), F32),
        compiler_params=pltpu.CompilerParams(
            dimension_semantics=("parallel",), vmem_limit_bytes=VMEM_LIMIT),
        name="ffn",
    )(x2, g_ffn, w_ffn_in, w_ffn_out, g_final)


def _pad_lanes(v):
    v = v.reshape(1, -1).astype(F32)
    return jnp.pad(v, ((0, 0), (0, LANES - v.shape[1])))


def kernel(x, mem, positions, g_mix, w_in, conv_w, a_log, dt_bias, gdn_norm_g, lambda_q1, lambda_k1, lambda_q2, lambda_k2, diff_norm_g, w_branch_gate, w_out_a, w_out_b, w_o, g_cross, g_mem, w_cq, w_ckv, w_co, g_ffn, w_ffn_in, w_ffn_out, g_final):
    B, S, D = x.shape
    M = mem.shape[1]
    depth = w_in.shape[0]
    assert depth == 1, "the final rmsnorm is fused into the (single) layer's ffn call"
    qkvz = 4 * GDN_HEADS * GDN_D
    xs = x.reshape(B * S, D)
    tables = _rope_tables(positions)
    kv = None
    for l in range(depth):
        ab_end = qkvz + 2 * GDN_HEADS
        qk_end = ab_end + 2 * DIFF_HEADS * 2 * DIFF_DH
        w16 = w_in[l].astype(BF16)
        w_plain = jnp.concatenate([w16[:, :qkvz], w16[:, qk_end:], w_branch_gate[l].astype(BF16)], axis=1)
        w_ab = jnp.pad(w16[:, qkvz:ab_end], ((0, 0), (0, LANES - 2 * GDN_HEADS)))
        plain, ab, h = _in_proj(xs, g_mix[l].reshape(1, D), w_plain, w_ab)
        src = jnp.asarray([hh * LANES + c for hh in range(2 * DIFF_HEADS) for c in _head_lane_source()],
                          dtype=jnp.int32)
        w_qk = jnp.take(w16[:, ab_end:qk_end], src, axis=1)
        rope = _rope_proj(h, w_qk, tables)

        u, w, qd, kd, attn, egl = _gdn_prep(plain, ab, conv_w[l], _pad_lanes(a_log[l]),
                                            _pad_lanes(dt_bias[l]), B, S)
        o_a = _gdn_scan(u, w, qd, kd, attn, egl, plain, gdn_norm_g[l].reshape(1, GDN_D), B, S)

        lam_init = 0.8 - 0.6 * math.exp(-0.3 * l)
        o_b = _diff_attn(rope, plain, lambda_q1[l].reshape(1, -1), lambda_k1[l].reshape(1, -1),
                         lambda_q2[l].reshape(1, -1), lambda_k2[l].reshape(1, -1),
                         diff_norm_g[l].reshape(1, -1), lam_init, B, S)

        xs = _merge(o_a, o_b, plain, xs,w_out_a[l].astype(BF16), w_out_b[l].astype(BF16),
                    w_o[l].astype(BF16))

        kv = _mem_kv(mem.reshape(B * M, D), g_mem[l].reshape(1, D), w_ckv[l].astype(BF16))
        xs = _cross(xs, g_cross[l].reshape(1, D), w_cq[l].astype(BF16), kv, w_co[l].astype(BF16), S, M)

        xs = _ffn(xs, g_ffn[l].reshape(1, D), w_ffn_in[l].astype(BF16), w_ffn_out[l].astype(BF16),
                  g_final.reshape(1, D))
    return xs.reshape(B, S, D)
```

```python
import functools
import math

import jax
import jax.numpy as jnp
from jax import lax
from jax.experimental import pallas as pl
from jax.experimental.pallas import tpu as pltpu

F32 = jnp.float32
BF16 = jnp.bfloat16

EPS = 1e-6
LANES = 128
GDN_HEADS = 8
GDN_D = 128
CONV_WIDTH = 4
CHUNK = 64
DIFF_HEADS = 8
DIFF_DH = 64
ROPE_DIM = DIFF_DH // 4
ROPE_THETA = 500000.0
X_HEADS = 4
X_DH = 128
NEG_BIG = -1e30
LOG2E = math.log2(math.e)

VMEM_LIMIT = 48 * 1024 * 1024

P_QA, P_KA, P_VA, P_Z, P_VB, P_GA, P_GB, P_END = 0, 8, 16, 24, 32, 40, 48, 56
R_Q, R_K = 0, 8


def _dot(a, b):
    return jnp.dot(a, b, preferred_element_type=F32)


def _dot_nt(a, b):
    return lax.dot_general(a, b, (((1,), (1,)), ((), ())), preferred_element_type=F32)


def _mm(a, b):
    return _dot(a.astype(BF16), b.astype(BF16))


def _split3(a):
    hi = a.astype(BF16)
    r = a - hi.astype(F32)
    mid = r.astype(BF16)
    lo = (r - mid.astype(F32)).astype(BF16)
    return hi, mid, lo


def _dot_mask(mask_bf16, b):
    hi, mid, lo = _split3(b)
    return _dot(mask_bf16, hi) + (_dot(mask_bf16, mid) + _dot(mask_bf16, lo))


def _dot_rmask(b, mask_bf16):
    hi, mid, lo = _split3(b)
    return _dot(hi, mask_bf16) + (_dot(mid, mask_bf16) + _dot(lo, mask_bf16))


def _rms(x, g):
    ms = jnp.mean(x * x, axis=-1, keepdims=True)
    return x * lax.rsqrt(ms + EPS) * g


def _sigmoid(x):
    return 0.5 * jnp.tanh(0.5 * x) + 0.5


def _silu(x):
    half = 0.5 * x
    return half * jnp.tanh(half) + half


def _softplus(x):
    return jnp.maximum(x, 0.0) + jnp.log(1.0 + jnp.exp(-jnp.abs(x)))


ROPE_HALF = ROPE_DIM // 2
HALF_LANES = LANES // 2


def _head_lane_source():
    src = [0] * LANES
    for m in range(2):
        for d in range(DIFF_DH):
            if d < ROPE_HALF:
                lane = m * ROPE_HALF + d
            elif d < ROPE_DIM:
                lane = HALF_LANES + m * ROPE_HALF + (d - ROPE_HALF)
            else:
                lane = (ROPE_DIM if m == 0 else HALF_LANES + ROPE_DIM) + (d - ROPE_DIM)
            src[lane] = m * DIFF_DH + d
    return src


def _is_map1_lane(lane):
    return (lane < ROPE_HALF) | ((lane >= ROPE_DIM) & (lane < HALF_LANES + ROPE_HALF))


ROPE_GROUPS = LANES // ROPE_HALF


def _rope_table_kernel(pos_ref, inv_ref, c_ref, sg_ref):
    ang = pos_ref[...] * inv_ref[...]
    cos = jnp.cos(ang)
    sin = jnp.sin(ang)
    rows = ang.shape[0]
    lane = lax.broadcasted_iota(jnp.int32, ang.shape, 1)
    first = lane < ROPE_DIM
    second = (lane >= HALF_LANES) & (lane < HALF_LANES + ROPE_DIM)
    freq = lane & (ROPE_HALF - 1)
    for a in range(ROPE_GROUPS):
        idx = freq + a * ROPE_HALF
        cos_a = jnp.take_along_axis(cos, idx, axis=1)
        sin_a = jnp.take_along_axis(sin, idx, axis=1)
        c_ref[a * rows:(a + 1) * rows, :] = jnp.where(first | second, cos_a, 1.0)
        sg_ref[a * rows:(a + 1) * rows, :] = jnp.where(first, -sin_a, jnp.where(second, sin_a, 0.0))


def _rope_tables(positions, tm=1024):
    T = positions.size
    rows = tm // ROPE_GROUPS
    pos = positions.astype(F32).reshape(T // tm, ROPE_GROUPS, rows).transpose(0, 2, 1)
    pos = jnp.repeat(pos, ROPE_HALF, axis=2).reshape(T // ROPE_GROUPS, LANES)
    inv_freq = ROPE_THETA ** (-jnp.arange(0, ROPE_DIM, 2, dtype=F32) / ROPE_DIM)
    inv = jnp.tile(inv_freq, ROPE_GROUPS).reshape(1, LANES)
    out = jax.ShapeDtypeStruct((T, LANES), F32)
    return pl.pallas_call(
        _rope_table_kernel,
        grid=(T // tm,),
        in_specs=[pl.BlockSpec((rows, LANES), lambda i: (i, 0)),
                  pl.BlockSpec((1, LANES), lambda i: (0, 0))],
        out_specs=[pl.BlockSpec((tm, LANES), lambda i: (i, 0))] * 2,
        out_shape=[out, out],
        name="rope_tables",
    )(pos, inv)


PROJ_SUB = 256


def _in_proj_kernel(x_ref, g_ref, w_ref, wab_ref, out_ref, ab_ref, h_ref):
    @pl.when(pl.program_id(1) == 0)
    def _():
        hb = _rms(x_ref[...], g_ref[...]).astype(BF16)
        h_ref[...] = hb
        ab_ref[...] = _dot(hb, wab_ref[...])

    hb = h_ref[...]
    for c0 in range(0, out_ref.shape[1], PROJ_SUB):
        cs = slice(c0, c0 + PROJ_SUB)
        out_ref[:, cs] = _dot(hb, w_ref[:, cs]).astype(out_ref.dtype)


def _in_proj(x2d, g_mix, w_plain, w_ab, tm=1024, tn=1792):
    T, D = x2d.shape
    N = w_plain.shape[1]
    tm = min(tm, T)
    return pl.pallas_call(
        _in_proj_kernel,
        grid=(T // tm, N // tn),
        in_specs=[pl.BlockSpec((tm, D), lambda i, j: (i, 0)),
                  pl.BlockSpec((1, D), lambda i, j: (0, 0)),
                  pl.BlockSpec((D, tn), lambda i, j: (0, j)),
                  pl.BlockSpec((D, LANES), lambda i, j: (0, 0))],
        out_specs=[pl.BlockSpec((tm, tn), lambda i, j: (i, j)),
                   pl.BlockSpec((tm, LANES), lambda i, j: (i, 0)),
                   pl.BlockSpec((tm, D), lambda i, j: (i, 0))],
        out_shape=[jax.ShapeDtypeStruct((T, N), BF16),
                   jax.ShapeDtypeStruct((T, LANES), F32),
                   jax.ShapeDtypeStruct((T, D), BF16)],
        compiler_params=pltpu.CompilerParams(
            dimension_semantics=("parallel", "arbitrary"), vmem_limit_bytes=VMEM_LIMIT),
        name="in_proj",
    )(x2d, g_mix, w_plain, w_ab)


def _rope_proj_kernel(h_ref, w_ref, c_ref, sg_ref, out_ref):
    scale = jnp.where(pl.program_id(1) == 0, DIFF_DH ** -0.5 * LOG2E, 1.0)
    c = c_ref[...] * scale
    sg = sg_ref[...] * scale
    hb = h_ref[...]
    for c0 in range(0, out_ref.shape[1], PROJ_SUB):
        acc = _dot(hb, w_ref[:, c0:c0 + PROJ_SUB])
        for l0 in range(0, PROJ_SUB, LANES):
            a = acc[:, l0:l0 + LANES]
            y = a * c + pltpu.roll(a, HALF_LANES, 1) * sg
            out_ref[:, c0 + l0:c0 + l0 + LANES] = y.astype(out_ref.dtype)


def _rope_proj(h, w_qk, tables, tm=1024):
    T, D = h.shape
    N = w_qk.shape[1]
    tn = N // 2
    tm = min(tm, T)
    return pl.pallas_call(
        _rope_proj_kernel,
        grid=(T // tm, 2),
        in_specs=[pl.BlockSpec((tm, D), lambda i, j: (i, 0)),
                  pl.BlockSpec((D, tn), lambda i, j: (0, j)),
                  pl.BlockSpec((tm, LANES), lambda i, j: (i, 0)),
                  pl.BlockSpec((tm, LANES), lambda i, j: (i, 0))],
        out_specs=pl.BlockSpec((tm, tn), lambda i, j: (i, j)),
        out_shape=jax.ShapeDtypeStruct((T, N), BF16),
        compiler_params=pltpu.CompilerParams(
            dimension_semantics=("parallel", "parallel"), vmem_limit_bytes=VMEM_LIMIT),
        name="rope_proj",
    )(h, w_qk, *tables)


GDN_TS = 4 * CHUNK
GDN_HEAD_GROUP = 4
SOLVE_BLOCK = 16
CHUNK_SHIFT = CHUNK.bit_length() - 1
SOLVE_SHIFT = SOLVE_BLOCK.bit_length() - 1
assert (1 << CHUNK_SHIFT) == CHUNK and (1 << SOLVE_SHIFT) == SOLVE_BLOCK and CHUNK // SOLVE_BLOCK == 4


def _gdn_prep_kernel(qkv_ref, ab_ref, cw_ref, alog_ref, dtb_ref,
                     u_ref, w_ref, qd_ref, kd_ref, attn_ref, egl_ref, halo_ref):
    TS, H, D = GDN_TS, GDN_HEADS, GDN_D

    @pl.when(pl.program_id(1) == 0)
    def _():
        halo_ref[...] = jnp.zeros(halo_ref.shape, F32)

    def conv_silu(c0):
        cs = slice(c0, c0 + D)
        ext = jnp.concatenate([halo_ref[:, cs], qkv_ref[:, cs].astype(F32)], axis=0)
        z = ext * cw_ref[0:1, cs]
        for t in range(1, CONV_WIDTH):
            z = pltpu.roll(z, 1, 0) + ext * cw_ref[t:t + 1, cs]
        return _silu(z[8:])

    def l2_normalised(x):
        return x * lax.rsqrt(jnp.sum(x * x, axis=-1, keepdims=True) + EPS)

    ri = lax.broadcasted_iota(jnp.int32, (TS, TS), 0)
    ci = lax.broadcasted_iota(jnp.int32, (TS, TS), 1)
    same = (ri >> CHUNK_SHIFT) == (ci >> CHUNK_SHIFT)
    incl = same & (ri >= ci)
    strict = same & (ri > ci)
    blk = (ri >> SOLVE_SHIFT) == (ci >> SOLVE_SHIFT)
    eye = jnp.where(ri == ci, 1.0, 0.0)

    ab = ab_ref[...]
    g_all = -jnp.exp(alog_ref[...]) * _softplus(ab + dtb_ref[...])
    b_all = _sigmoid(ab)
    gc_all = _dot_mask(jnp.where(incl, 1.0, 0.0).astype(BF16), g_all)
    gl_all = _dot_mask(jnp.where(same, 1.0, 0.0).astype(BF16), g_all)
    gc_t = _dot_rmask(g_all.T, jnp.where(same & (ri <= ci), 1.0, 0.0).astype(BF16))

    r8 = lax.broadcasted_iota(jnp.int32, (8, TS), 0)
    c8 = lax.broadcasted_iota(jnp.int32, (8, TS), 1)
    sel = jnp.where(c8 == r8 * CHUNK, 1.0, 0.0).astype(BF16)
    gl8 = _dot_mask(sel, gl_all)

    def head_group(hs):
        ld, lo, x, rhs = {}, {}, {}, {}
        for h in hs:
            cs = slice(h * D, (h + 1) * D)
            q = l2_normalised(conv_silu(h * D)) * (D ** -0.5)
            k = l2_normalised(conv_silu((H + h) * D))
            v = conv_silu((2 * H + h) * D)
            gc_col = gc_all[:, h:h + 1]
            gl_col = gl_all[:, h:h + 1]
            beta = b_all[:, H + h:H + h + 1]
            decay = jnp.exp(jnp.minimum(gc_col - gc_t[h:h + 1, :], 0.0))
            eg = jnp.exp(gc_col)
            kb = k * beta
            kbf = k.astype(BF16)
            lmat = jnp.where(strict, _dot_nt(kb.astype(BF16), kbf) * decay, 0.0)
            attn = jnp.where(incl, _dot_nt(q.astype(BF16), kbf) * decay, 0.0)
            attn_ref[:, h * TS:(h + 1) * TS] = attn.astype(attn_ref.dtype)
            qd_ref[:, cs] = (q * eg).astype(qd_ref.dtype)
            kd_ref[:, cs] = (k * jnp.exp(gl_col - gc_col)).astype(kd_ref.dtype)
            egl_ref[:, cs] = jnp.exp(jnp.broadcast_to(gl8[:, h:h + 1], (8, D)))
            ldh = jnp.where(blk, lmat, 0.0)
            ld[h] = ldh.astype(BF16)
            lo[h] = (lmat - ldh).astype(BF16)
            x[h] = eye - ldh
            rhs[h] = jnp.concatenate([v * beta, kb * eg], axis=1).astype(BF16)

        p = {h: _dot(ld[h], ld[h]) for h in hs}
        for _ in range(2):
            pb = {h: p[h].astype(BF16) for h in hs}
            x = {h: x[h] + _mm(x[h], pb[h]) for h in hs}
            p = {h: _dot(pb[h], pb[h]) for h in hs}
        td = {h: x[h] + _mm(x[h], p[h]) for h in hs}
        tdb = {h: td[h].astype(BF16) for h in hs}
        n = {h: _dot(tdb[h], lo[h]).astype(BF16) for h in hs}
        n2 = {h: _dot(n[h], n[h]) for h in hs}
        m1 = {h: td[h] + _mm(n2[h], tdb[h]) for h in hs}
        tinv = {h: m1[h] - _mm(n[h], m1[h]) for h in hs}
        for h in hs:
            cs = slice(h * D, (h + 1) * D)
            uw = _mm(tinv[h], rhs[h])
            u_ref[:, cs] = uw[:, :D].astype(u_ref.dtype)
            w_ref[:, cs] = uw[:, D:].astype(w_ref.dtype)

    for g0 in range(0, H, GDN_HEAD_GROUP):
        head_group(range(g0, g0 + GDN_HEAD_GROUP))
    halo_ref[...] = qkv_ref[TS - 8:TS, :].astype(F32)


def _gdn_prep(plain, ab, conv_w, a_log, dt_bias, B, S):
    T = B * S
    TS = GDN_TS
    ns = S // TS
    HD = GDN_HEADS * GDN_D
    row = lambda b, s: b * ns + s
    vec = pl.BlockSpec((1, LANES), lambda b, s: (0, 0))
    tokw = pl.BlockSpec((TS, HD), lambda b, s: (row(b, s), 0))
    act = jax.ShapeDtypeStruct((T, HD), BF16)
    return pl.pallas_call(
        _gdn_prep_kernel,
        grid=(B, ns),
        in_specs=[pl.BlockSpec((TS, 3 * HD), lambda b, s: (row(b, s), 0)),
                  pl.BlockSpec((TS, LANES), lambda b, s: (row(b, s), 0)),
                  pl.BlockSpec((CONV_WIDTH, 3 * HD), lambda b, s: (0, 0)),
                  vec, vec],
        out_specs=[tokw, tokw, tokw, tokw,
                   pl.BlockSpec((TS, GDN_HEADS * TS), lambda b, s: (row(b, s), 0)),
                   pl.BlockSpec((8, HD), lambda b, s: (row(b, s), 0))],
        out_shape=[act, act, act, act,
                   jax.ShapeDtypeStruct((T, GDN_HEADS * TS), BF16),
                   jax.ShapeDtypeStruct((B * ns * 8, HD), F32)],
        scratch_shapes=[pltpu.VMEM((8, 3 * HD), F32)],
        compiler_params=pltpu.CompilerParams(
            dimension_semantics=("parallel", "arbitrary"), vmem_limit_bytes=VMEM_LIMIT),
        name="gdn_prep",
    )(plain, ab, conv_w, a_log, dt_bias)


def _gdn_scan_kernel(u_ref, w_ref, qd_ref, kd_ref, attn_ref, egl_ref, z_ref, gn_ref, o_ref,
                     state_ref, vnew_ref, oq_ref):
    TS = GDN_TS
    D = GDN_D
    nb = u_ref.shape[0]
    s = pl.program_id(1)

    @pl.when(s == 0)
    def _():
        state_ref[...] = jnp.zeros(state_ref.shape, F32)

    for c in range(TS // CHUNK):
        r = slice(c * CHUNK, (c + 1) * CHUNK)
        for b in range(nb):
            for h in range(GDN_HEADS):
                cs = slice(h * D, (h + 1) * D)
                st = state_ref[b, h]
                wq = jnp.concatenate([w_ref[b, r, cs], qd_ref[b, r, cs]], axis=0)
                ws = _dot(wq, st.astype(BF16))
                v_new = (u_ref[b, r, cs].astype(F32) - ws[:CHUNK]).astype(BF16)
                vnew_ref[b, h, r, :] = v_new
                oq_ref[b, r, cs] = ws[CHUNK:]
                ktv = lax.dot_general(kd_ref[b, r, cs], v_new, (((0,), (0,)), ((), ())),
                                      preferred_element_type=F32)
                state_ref[b, h] = st * egl_ref[b, c:c + 1, cs] + ktv

    for b in range(nb):
        for h in range(GDN_HEADS):
            cs = slice(h * D, (h + 1) * D)
            o = oq_ref[b, :, cs] + _dot(attn_ref[b, :, h * TS:(h + 1) * TS], vnew_ref[b, h])
            o = _rms(o, gn_ref[...]) * _silu(z_ref[b, :, cs].astype(F32))
            o_ref[b, :, cs] = o.astype(o_ref.dtype)


def _gdn_scan(u, w, qd, kd, attn, egl, plain, gdn_norm_g, B, S, nb=4):
    TS = GDN_TS
    ns = S // TS
    HD = GDN_HEADS * GDN_D
    nb = min(nb, B)
    assert B % nb == 0
    seq = lambda a: a.reshape(B, S, a.shape[-1])
    tokw = pl.BlockSpec((nb, TS, HD), lambda b, s: (b, s, 0))
    out = pl.pallas_call(
        _gdn_scan_kernel,
        grid=(B // nb, ns),
        in_specs=[tokw, tokw, tokw, tokw,
                  pl.BlockSpec((nb, TS, GDN_HEADS * TS), lambda b, s: (b, s, 0)),
                  pl.BlockSpec((nb, 8, HD), lambda b, s: (b, s, 0)),
                  pl.BlockSpec((nb, TS, HD), lambda b, s: (b, s, P_Z * LANES // HD)),
                  pl.BlockSpec((1, GDN_D), lambda b, s: (0, 0))],
        out_specs=tokw,
        out_shape=jax.ShapeDtypeStruct((B, S, HD), BF16),
        scratch_shapes=[pltpu.VMEM((nb, GDN_HEADS, GDN_D, GDN_D), F32),
                        pltpu.VMEM((nb, GDN_HEADS, TS, GDN_D), BF16),
                        pltpu.VMEM((nb, TS, HD), F32)],
        compiler_params=pltpu.CompilerParams(
            dimension_semantics=("parallel", "arbitrary"), vmem_limit_bytes=VMEM_LIMIT),
        name="gdn_scan",
    )(seq(u), seq(w), seq(qd), seq(kd), seq(attn), egl.reshape(B, ns * 8, HD), seq(plain), gdn_norm_g)
    return out.reshape(B * S, HD)


def _diff_attn_kernel(q_ref, qn_ref, k_ref, v_ref, lq1_ref, lk1_ref, lq2_ref, lk2_ref, gn_ref, o_ref,
                      vext_ref, qz_ref, qnz_ref, sa_ref, sb_ref, sc_ref, sh_ref, m_ref, acc_ref,
                      *, tq, nq, lam_init):
    i = pl.program_id(2)
    dv = 2 * DIFF_DH
    tk = tq // 2

    def stack(src_ref, dst_ref):
        q = src_ref[...]
        map1 = _is_map1_lane(lax.broadcasted_iota(jnp.int32, q.shape, 1))
        zero = jnp.zeros_like(q)
        q1 = jnp.where(map1, q, zero)
        q2 = jnp.where(map1, zero, q)
        for half in range(2):
            rows = slice(half * tk, (half + 1) * tk)
            dst_ref[2 * half * tk:(2 * half + 1) * tk, :] = q1[rows]
            dst_ref[(2 * half + 1) * tk:(2 * half + 2) * tk, :] = q2[rows]

    def scores(j, s_ref, qsrc_ref, r0=0):
        off = pl.multiple_of(j * tk, tk)
        s_ref[...] = _dot_nt(qsrc_ref[r0:, :], k_ref[pl.ds(off, tk), :])

    def consume(j, s_ref, diag=None, r0=0, first=False):
        off = pl.multiple_of(j * tk, tk)
        sc = s_ref[...]
        if diag is not None:
            rr = lax.broadcasted_iota(jnp.int32, sc.shape, 0)
            cc = lax.broadcasted_iota(jnp.int32, sc.shape, 1)
            keep = (rr & (tk - 1)) >= cc
            if diag == "D1":
                keep = keep | (rr >= tq)
            sc = jnp.where(keep, sc, NEG_BIG)
        rows = slice(r0, 2 * tq)
        row_max = jnp.max(sc, axis=-1, keepdims=True)
        if first:
            m_new = jnp.broadcast_to(row_max, (sc.shape[0], LANES))
        else:
            m_prev = m_ref[rows, :]
            m_new = jnp.maximum(m_prev, row_max)
        p = jnp.concatenate([jnp.exp2(sc[:, c0:c0 + LANES] - m_new) for c0 in range(0, tk, LANES)],
                            axis=1)
        pv = _dot(p.astype(BF16), vext_ref[pl.ds(off, tk), :])
        if first:
            acc_ref[rows, :] = pv
        else:
            alpha = jnp.exp2(m_prev - m_new)
            for c0 in range(0, 2 * dv, LANES):
                acc_ref[rows, c0:c0 + LANES] = (alpha * acc_ref[rows, c0:c0 + LANES]
                                                + pv[:, c0:c0 + LANES])
        m_ref[rows, :] = m_new

    def next_d1():
        nxt = jnp.minimum(i + 1, nq - 1)
        stack(qn_ref, qnz_ref)
        scores(2 * nxt, sc_ref, qnz_ref)

    stack(q_ref, qz_ref)

    @pl.when(i == 0)
    def _():
        vext_ref[:, :dv] = v_ref[...]
        vext_ref[:, dv:] = jnp.ones((vext_ref.shape[0], dv), BF16)
        scores(0, sc_ref, qz_ref)

    scores(2 * i + 1, sh_ref, qz_ref, r0=tq)
    consume(2 * i, sc_ref, diag="D1", first=True)

    @pl.when(i == 0)
    def _():
        next_d1()
        consume(1, sh_ref, diag="D2", r0=tq)

    @pl.when(i > 0)
    def _():
        scores(0, sa_ref, qz_ref)
        consume(2 * i + 1, sh_ref, diag="D2", r0=tq)

        def pair(p, carry):
            j = 2 * p
            scores(j + 1, sb_ref, qz_ref)
            consume(j, sa_ref)
            scores(j + 2, sa_ref, qz_ref)
            consume(j + 1, sb_ref)
            return carry

        lax.fori_loop(0, i - 1, pair, 0)
        scores(2 * i - 1, sb_ref, qz_ref)
        consume(2 * i - 2, sa_ref)
        next_d1()
        consume(2 * i - 1, sb_ref)

    lam = (jnp.exp(jnp.sum(lq1_ref[...] * lk1_ref[...], axis=-1, keepdims=True))
           - jnp.exp(jnp.sum(lq2_ref[...] * lk2_ref[...], axis=-1, keepdims=True))
           + lam_init)
    for half in range(2):
        a1 = acc_ref[2 * half * tk:(2 * half + 1) * tk, :]
        a2 = acc_ref[(2 * half + 1) * tk:(2 * half + 2) * tk, :]
        o = a1[:, :dv] / a1[:, dv:] - lam * (a2[:, :dv] / a2[:, dv:])
        o = _rms(o, gn_ref[...]) * (1.0 - lam_init)
        o_ref[half * tk:(half + 1) * tk, :] = o.astype(o_ref.dtype)


def _diff_attn(rope, plain, lq1, lk1, lq2, lk2, diff_norm_g, lam_init, B, S, tq=1024):
    T = B * S
    nq = S // tq
    dv = 2 * DIFF_DH
    lam_spec = pl.BlockSpec((1, DIFF_DH), lambda b, h, i: (0, 0))
    score_buf = pltpu.VMEM((2 * tq, tq // 2), F32)
    stacked_q = pltpu.VMEM((2 * tq, LANES), BF16)
    return pl.pallas_call(
        functools.partial(_diff_attn_kernel, tq=tq, nq=nq, lam_init=lam_init),
        grid=(B, DIFF_HEADS, nq),
        in_specs=[pl.BlockSpec((tq, LANES), lambda b, h, i: (b * nq + i, R_Q + h)),
                  pl.BlockSpec((tq, LANES),
                               lambda b, h, i: (b * nq + jnp.minimum(i + 1, nq - 1), R_Q + h)),
                  pl.BlockSpec((S, LANES), lambda b, h, i: (b, R_K + h)),
                  pl.BlockSpec((S, LANES), lambda b, h, i: (b, P_VB + h)),
                  lam_spec, lam_spec, lam_spec, lam_spec,
                  pl.BlockSpec((1, dv), lambda b, h, i: (0, 0))],
        out_specs=pl.BlockSpec((tq, LANES), lambda b, h, i: (b * nq + i, h)),
        out_shape=jax.ShapeDtypeStruct((T, DIFF_HEADS * dv), BF16),
        scratch_shapes=[pltpu.VMEM((S, 2 * dv), BF16),
                        stacked_q, stacked_q,
                        score_buf, score_buf, score_buf,
                        pltpu.VMEM((tq, tq // 2), F32),
                        pltpu.VMEM((2 * tq, LANES), F32),
                        pltpu.VMEM((2 * tq, 2 * dv), F32)],
        compiler_params=pltpu.CompilerParams(
            dimension_semantics=("parallel", "parallel", "arbitrary"),
            vmem_limit_bytes=VMEM_LIMIT),
        name="diff_attn",
    )(rope, rope, rope, plain, lq1, lk1, lq2, lk2, diff_norm_g)


def _merge_kernel(oa_ref, ob_ref, ga_ref, gb_ref, x_ref, woa_ref, wob_ref, wo_ref, x1_ref):
    ya = _dot(oa_ref[...], woa_ref[...])
    yb = _dot(ob_ref[...], wob_ref[...])
    merged = _sigmoid(ga_ref[...].astype(F32)) * ya + _sigmoid(gb_ref[...].astype(F32)) * yb
    x1_ref[...] = x_ref[...] + _dot(merged.astype(BF16), wo_ref[...])


def _merge(o_a, o_b, plain, x2d, w_out_a, w_out_b, w_o, tm=512):
    T, D = x2d.shape
    tokd = pl.BlockSpec((tm, D), lambda i: (i, 0))
    wspec = pl.BlockSpec((D, D), lambda i: (0, 0))
    return pl.pallas_call(
        _merge_kernel,
        grid=(T // tm,),
        in_specs=[tokd, tokd,
                  pl.BlockSpec((tm, D), lambda i: (i, P_GA * LANES // D)),
                  pl.BlockSpec((tm, D), lambda i: (i, P_GB * LANES // D)),
                  tokd, wspec, wspec, wspec],
        out_specs=tokd,
        out_shape=jax.ShapeDtypeStruct((T, D), F32),
        compiler_params=pltpu.CompilerParams(
            dimension_semantics=("parallel",), vmem_limit_bytes=VMEM_LIMIT),
        name="merge",
    )(o_a, o_b, plain, plain, x2d, w_out_a, w_out_b, w_o)


def _mem_kv_kernel(m_ref, g_ref, w_ref, o_ref):
    o_ref[...] = _dot(_rms(m_ref[...], g_ref[...]).astype(BF16), w_ref[...]).astype(o_ref.dtype)


def _mem_kv(mem2d, g_mem, w_ckv):
    R, D = mem2d.shape
    N = w_ckv.shape[1]
    return pl.pallas_call(
        _mem_kv_kernel,
        grid=(1,),
        in_specs=[pl.BlockSpec((R, D), lambda i: (0, 0)),
                  pl.BlockSpec((1, D), lambda i: (0, 0)),
                  pl.BlockSpec((D, N), lambda i: (0, 0))],
        out_specs=pl.BlockSpec((R, N), lambda i: (0, 0)),
        out_shape=jax.ShapeDtypeStruct((R, N), BF16),
        compiler_params=pltpu.CompilerParams(vmem_limit_bytes=VMEM_LIMIT),
        name="mem_kv",
    )(mem2d, g_mem, w_ckv)


def _cross_kernel(x_ref, g_ref, wq_ref, kv_ref, wo_ref, o_ref):
    x = x_ref[...]
    hx = _rms(x, g_ref[...]).astype(BF16)
    qc = (_dot(hx, wq_ref[...]) * (X_DH ** -0.5)).astype(BF16)
    xw = X_HEADS * X_DH
    outs = []
    ones = jnp.ones((kv_ref.shape[0], X_DH), BF16)
    for hh in range(X_HEADS):
        cs = slice(hh * X_DH, (hh + 1) * X_DH)
        sc = _dot_nt(qc[:, cs], kv_ref[:, cs])
        m = jnp.broadcast_to(jnp.max(sc, axis=-1, keepdims=True), (sc.shape[0], LANES))
        p = jnp.concatenate([jnp.exp(sc[:, c0:c0 + LANES] - m) for c0 in range(0, sc.shape[1], LANES)],
                            axis=1)
        vext = jnp.concatenate([kv_ref[:, xw + hh * X_DH:xw + (hh + 1) * X_DH], ones], axis=1)
        oe = _dot(p.astype(BF16), vext)
        outs.append(oe[:, :X_DH] / oe[:, X_DH:])
    oc = jnp.concatenate(outs, axis=1).astype(BF16)
    o_ref[...] = x + _dot(oc, wo_ref[...])


def _cross(x1, g_cross, w_cq, kv, w_co, S, M, tm=1024):
    T, D = x1.shape
    xw = X_HEADS * X_DH
    per_b = S // tm
    tokd = pl.BlockSpec((tm, D), lambda i: (i, 0))
    return pl.pallas_call(
        _cross_kernel,
        grid=(T // tm,),
        in_specs=[tokd,
                  pl.BlockSpec((1, D), lambda i: (0, 0)),
                  pl.BlockSpec((D, xw), lambda i: (0, 0)),
                  pl.BlockSpec((M, 2 * xw), lambda i: (i // per_b, 0)),
                  pl.BlockSpec((xw, D), lambda i: (0, 0))],
        out_specs=tokd,
        out_shape=jax.ShapeDtypeStruct((T, D), F32),
        compiler_params=pltpu.CompilerParams(
            dimension_semantics=("parallel",), vmem_limit_bytes=VMEM_LIMIT),
        name="cross",
    )(x1, g_cross, w_cq, kv, w_co)


def _ffn_kernel(x_ref, g_ref, wi_ref, wo_ref, gf_ref, o_ref, *, tf):
    x = x_ref[...]
    hb = _rms(x, g_ref[...]).astype(BF16)
    F = wo_ref.shape[0]
    acc = x
    for f in range(F // tf):
        act = _silu(_dot(hb, wi_ref[:, f * tf:(f + 1) * tf])) * _dot(hb, wi_ref[:, F + f * tf:F + (f + 1) * tf])
        acc = acc + _dot(act.astype(BF16), wo_ref[f * tf:(f + 1) * tf, :])
    o_ref[...] = _rms(acc, gf_ref[...])


def _ffn(x2, g_ffn, w_ffn_in, w_ffn_out, g_final, tm=1024, tf=256):
    T, D = x2.shape
    F = w_ffn_out.shape[0]
    tokd = pl.BlockSpec((tm, D), lambda i: (i, 0))
    vec = pl.BlockSpec((1, D), lambda i: (0, 0))
    resident = pl.Buffered(1)
    return pl.pallas_call(
        functools.partial(_ffn_kernel, tf=tf),
        grid=(T // tm,),
        in_specs=[tokd, vec,
                  pl.BlockSpec((D, 2 * F), lambda i: (0, 0), pipeline_mode=resident),
                  pl.BlockSpec((F, D), lambda i: (0, 0), pipeline_mode=resident),
                  vec],
        out_specs=tokd,
        out_shape=jax.ShapeDtypeStruct((T, D), F32),
        compiler_params=pltpu.CompilerParams(
            dimension_semantics=("parallel",), vmem_limit_bytes=VMEM_LIMIT),
        name="ffn",
    )(x2, g_ffn, w_ffn_in, w_ffn_out, g_final)


def _pad_lanes(v):
    v = v.reshape(1, -1).astype(F32)
    return jnp.pad(v, ((0, 0), (0, LANES - v.shape[1])))


def kernel(x, mem, positions, g_mix, w_in, conv_w, a_log, dt_bias, gdn_norm_g, lambda_q1, lambda_k1, lambda_q2, lambda_k2, diff_norm_g, w_branch_gate, w_out_a, w_out_b, w_o, g_cross, g_mem, w_cq, w_ckv, w_co, g_ffn, w_ffn_in, w_ffn_out, g_final):
    B, S, D = x.shape
    M = mem.shape[1]
    depth = w_in.shape[0]
    assert depth == 1, "the final rmsnorm is fused into the (single) layer's ffn call"
    qkvz = 4 * GDN_HEADS * GDN_D
    xs = x.reshape(B * S, D)
    tables = _rope_tables(positions)
    kv = None
    for l in range(depth):
        ab_end = qkvz + 2 * GDN_HEADS
        qk_end = ab_end + 2 * DIFF_HEADS * 2 * DIFF_DH
        w16 = w_in[l].astype(BF16)
        w_plain = jnp.concatenate([w16[:, :qkvz], w16[:, qk_end:], w_branch_gate[l].astype(BF16)], axis=1)
        w_ab = jnp.pad(w16[:, qkvz:ab_end], ((0, 0), (0, LANES - 2 * GDN_HEADS)))
        plain, ab, h = _in_proj(xs, g_mix[l].reshape(1, D), w_plain, w_ab)
        src = jnp.asarray([hh * LANES + c for hh in range(2 * DIFF_HEADS) for c in _head_lane_source()],
                          dtype=jnp.int32)
        w_qk = jnp.take(w16[:, ab_end:qk_end], src, axis=1)
        rope = _rope_proj(h, w_qk, tables)

        u, w, qd, kd, attn, egl = _gdn_prep(plain, ab, conv_w[l], _pad_lanes(a_log[l]),
                                            _pad_lanes(dt_bias[l]), B, S)
        o_a = _gdn_scan(u, w, qd, kd, attn, egl, plain, gdn_norm_g[l].reshape(1, GDN_D), B, S)

        lam_init = 0.8 - 0.6 * math.exp(-0.3 * l)
        o_b = _diff_attn(rope, plain, lambda_q1[l].reshape(1, -1), lambda_k1[l].reshape(1, -1),
                         lambda_q2[l].reshape(1, -1), lambda_k2[l].reshape(1, -1),
                         diff_norm_g[l].reshape(1, -1), lam_init, B, S)

        xs = _merge(o_a, o_b, plain, xs,w_out_a[l].astype(BF16), w_out_b[l].astype(BF16),
                    w_o[l].astype(BF16))

        kv = _mem_kv(mem.reshape(B * M, D), g_mem[l].reshape(1, D), w_ckv[l].astype(BF16))
        xs = _cross(xs, g_cross[l].reshape(1, D), w_cq[l].astype(BF16), kv, w_co[l].astype(BF16), S, M)

        xs = _ffn(xs, g_ffn[l].reshape(1, D), w_ffn_in[l].astype(BF16), w_ffn_out[l].astype(BF16),
                  g_final.reshape(1, D))
    return xs.reshape(B, S, D)
```

```python
import functools
import math

import jax
import jax.numpy as jnp
from jax import lax
from jax.experimental import pallas as pl
from jax.experimental.pallas import tpu as pltpu

F32 = jnp.float32
BF16 = jnp.bfloat16

EPS = 1e-6
LANES = 128
GDN_HEADS = 8
GDN_D = 128
CONV_WIDTH = 4
CHUNK = 64
DIFF_HEADS = 8
DIFF_DH = 64
ROPE_DIM = DIFF_DH // 4
ROPE_THETA = 500000.0
X_HEADS = 4
X_DH = 128
NEG_BIG = -1e30
LOG2E = math.log2(math.e)

VMEM_LIMIT = 48 * 1024 * 1024

P_QA, P_KA, P_VA, P_Z, P_VB, P_GA, P_GB, P_END = 0, 8, 16, 24, 32, 40, 48, 56
R_Q, R_K = 0, 8


def _dot(a, b):
    return jnp.dot(a, b, preferred_element_type=F32)


def _dot_nt(a, b):
    return lax.dot_general(a, b, (((1,), (1,)), ((), ())), preferred_element_type=F32)


def _mm(a, b):
    return _dot(a.astype(BF16), b.astype(BF16))


def _split3(a):
    hi = a.astype(BF16)
    r = a - hi.astype(F32)
    mid = r.astype(BF16)
    lo = (r - mid.astype(F32)).astype(BF16)
    return hi, mid, lo


def _dot_mask(mask_bf16, b):
    hi, mid, lo = _split3(b)
    return _dot(mask_bf16, hi) + (_dot(mask_bf16, mid) + _dot(mask_bf16, lo))


def _dot_rmask(b, mask_bf16):
    hi, mid, lo = _split3(b)
    return _dot(hi, mask_bf16) + (_dot(mid, mask_bf16) + _dot(lo, mask_bf16))


def _rms(x, g):
    ms = jnp.mean(x * x, axis=-1, keepdims=True)
    return x * lax.rsqrt(ms + EPS) * g


def _sigmoid(x):
    return 0.5 * jnp.tanh(0.5 * x) + 0.5


def _silu(x):
    half = 0.5 * x
    return half * jnp.tanh(half) + half


def _softplus(x):
    return jnp.maximum(x, 0.0) + jnp.log(1.0 + jnp.exp(-jnp.abs(x)))


ROPE_HALF = ROPE_DIM // 2
HALF_LANES = LANES // 2


def _head_lane_source():
    src = [0] * LANES
    for m in range(2):
        for d in range(DIFF_DH):
            if d < ROPE_HALF:
                lane = m * ROPE_HALF + d
            elif d < ROPE_DIM:
                lane = HALF_LANES + m * ROPE_HALF + (d - ROPE_HALF)
            else:
                lane = (ROPE_DIM if m == 0 else HALF_LANES + ROPE_DIM) + (d - ROPE_DIM)
            src[lane] = m * DIFF_DH + d
    return src


def _is_map1_lane(lane):
    return (lane < ROPE_HALF) | ((lane >= ROPE_DIM) & (lane < HALF_LANES + ROPE_HALF))


ROPE_GROUPS = LANES // ROPE_HALF


def _rope_table_kernel(pos_ref, inv_ref, c_ref, sg_ref):
    ang = pos_ref[...] * inv_ref[...]
    cos = jnp.cos(ang)
    sin = jnp.sin(ang)
    rows = ang.shape[0]
    lane = lax.broadcasted_iota(jnp.int32, ang.shape, 1)
    first = lane < ROPE_DIM
    second = (lane >= HALF_LANES) & (lane < HALF_LANES + ROPE_DIM)
    freq = lane & (ROPE_HALF - 1)
    for a in range(ROPE_GROUPS):
        idx = freq + a * ROPE_HALF
        cos_a = jnp.take_along_axis(cos, idx, axis=1)
        sin_a = jnp.take_along_axis(sin, idx, axis=1)
        c_ref[a * rows:(a + 1) * rows, :] = jnp.where(first | second, cos_a, 1.0)
        sg_ref[a * rows:(a + 1) * rows, :] = jnp.where(first, -sin_a, jnp.where(second, sin_a, 0.0))


def _rope_tables(positions, tm=1024):
    T = positions.size
    rows = tm // ROPE_GROUPS
    pos = positions.astype(F32).reshape(T // tm, ROPE_GROUPS, rows).transpose(0, 2, 1)
    pos = jnp.repeat(pos, ROPE_HALF, axis=2).reshape(T // ROPE_GROUPS, LANES)
    inv_freq = ROPE_THETA ** (-jnp.arange(0, ROPE_DIM, 2, dtype=F32) / ROPE_DIM)
    inv = jnp.tile(inv_freq, ROPE_GROUPS).reshape(1, LANES)
    out = jax.ShapeDtypeStruct((T, LANES), F32)
    return pl.pallas_call(
        _rope_table_kernel,
        grid=(T // tm,),
        in_specs=[pl.BlockSpec((rows, LANES), lambda i: (i, 0)),
                  pl.BlockSpec((1, LANES), lambda i: (0, 0))],
        out_specs=[pl.BlockSpec((tm, LANES), lambda i: (i, 0))] * 2,
        out_shape=[out, out],
        name="rope_tables",
    )(pos, inv)


PROJ_SUB = 256


def _in_proj_kernel(x_ref, g_ref, w_ref, wab_ref, out_ref, ab_ref, h_ref):
    @pl.when(pl.program_id(1) == 0)
    def _():
        hb = _rms(x_ref[...], g_ref[...]).astype(BF16)
        h_ref[...] = hb
        ab_ref[...] = _dot(hb, wab_ref[...])

    hb = h_ref[...]
    for c0 in range(0, out_ref.shape[1], PROJ_SUB):
        cs = slice(c0, c0 + PROJ_SUB)
        out_ref[:, cs] = _dot(hb, w_ref[:, cs]).astype(out_ref.dtype)


def _in_proj(x2d, g_mix, w_plain, w_ab, tm=1024, tn=1792):
    T, D = x2d.shape
    N = w_plain.shape[1]
    tm = min(tm, T)
    return pl.pallas_call(
        _in_proj_kernel,
        grid=(T // tm, N // tn),
        in_specs=[pl.BlockSpec((tm, D), lambda i, j: (i, 0)),
                  pl.BlockSpec((1, D), lambda i, j: (0, 0)),
                  pl.BlockSpec((D, tn), lambda i, j: (0, j)),
                  pl.BlockSpec((D, LANES), lambda i, j: (0, 0))],
        out_specs=[pl.BlockSpec((tm, tn), lambda i, j: (i, j)),
                   pl.BlockSpec((tm, LANES), lambda i, j: (i, 0)),
                   pl.BlockSpec((tm, D), lambda i, j: (i, 0))],
        out_shape=[jax.ShapeDtypeStruct((T, N), BF16),
                   jax.ShapeDtypeStruct((T, LANES), F32),
                   jax.ShapeDtypeStruct((T, D), BF16)],
        compiler_params=pltpu.CompilerParams(
            dimension_semantics=("parallel", "arbitrary"), vmem_limit_bytes=VMEM_LIMIT),
        name="in_proj",
    )(x2d, g_mix, w_plain, w_ab)


def _rope_proj_kernel(h_ref, w_ref, c_ref, sg_ref, out_ref):
    scale = jnp.where(pl.program_id(1) == 0, DIFF_DH ** -0.5 * LOG2E, 1.0)
    c = c_ref[...] * scale
    sg = sg_ref[...] * scale
    hb = h_ref[...]
    for c0 in range(0, out_ref.shape[1], PROJ_SUB):
        acc = _dot(hb, w_ref[:, c0:c0 + PROJ_SUB])
        for l0 in range(0, PROJ_SUB, LANES):
            a = acc[:, l0:l0 + LANES]
            y = a * c + pltpu.roll(a, HALF_LANES, 1) * sg
            out_ref[:, c0 + l0:c0 + l0 + LANES] = y.astype(out_ref.dtype)


def _rope_proj(h, w_qk, tables, tm=2048):
    T, D = h.shape
    N = w_qk.shape[1]
    tn = N // 2
    tm = min(tm, T)
    return pl.pallas_call(
        _rope_proj_kernel,
        grid=(T // tm, 2),
        in_specs=[pl.BlockSpec((tm, D), lambda i, j: (i, 0)),
                  pl.BlockSpec((D, tn), lambda i, j: (0, j)),
                  pl.BlockSpec((tm, LANES), lambda i, j: (i, 0)),
                  pl.BlockSpec((tm, LANES), lambda i, j: (i, 0))],
        out_specs=pl.BlockSpec((tm, tn), lambda i, j: (i, j)),
        out_shape=jax.ShapeDtypeStruct((T, N), BF16),
        compiler_params=pltpu.CompilerParams(
            dimension_semantics=("parallel", "parallel"), vmem_limit_bytes=VMEM_LIMIT),
        name="rope_proj",
    )(h, w_qk, *tables)


GDN_TS = 4 * CHUNK
GDN_HEAD_GROUP = 4
SOLVE_BLOCK = 16
CHUNK_SHIFT = CHUNK.bit_length() - 1
SOLVE_SHIFT = SOLVE_BLOCK.bit_length() - 1
assert (1 << CHUNK_SHIFT) == CHUNK and (1 << SOLVE_SHIFT) == SOLVE_BLOCK and CHUNK // SOLVE_BLOCK == 4


def _gdn_prep_kernel(qkv_ref, ab_ref, cw_ref, alog_ref, dtb_ref,
                     u_ref, w_ref, qd_ref, kd_ref, attn_ref, egl_ref, halo_ref):
    TS, H, D = GDN_TS, GDN_HEADS, GDN_D

    @pl.when(pl.program_id(1) == 0)
    def _():
        halo_ref[...] = jnp.zeros(halo_ref.shape, F32)

    def conv_silu(c0):
        cs = slice(c0, c0 + D)
        ext = jnp.concatenate([halo_ref[:, cs], qkv_ref[:, cs].astype(F32)], axis=0)
        z = ext * cw_ref[0:1, cs]
        for t in range(1, CONV_WIDTH):
            z = pltpu.roll(z, 1, 0) + ext * cw_ref[t:t + 1, cs]
        return _silu(z[8:])

    def l2_normalised(x):
        return x * lax.rsqrt(jnp.sum(x * x, axis=-1, keepdims=True) + EPS)

    ri = lax.broadcasted_iota(jnp.int32, (TS, TS), 0)
    ci = lax.broadcasted_iota(jnp.int32, (TS, TS), 1)
    same = (ri >> CHUNK_SHIFT) == (ci >> CHUNK_SHIFT)
    incl = same & (ri >= ci)
    strict = same & (ri > ci)
    blk = (ri >> SOLVE_SHIFT) == (ci >> SOLVE_SHIFT)
    eye = jnp.where(ri == ci, 1.0, 0.0)

    ab = ab_ref[...]
    g_all = (-LOG2E) * jnp.exp(alog_ref[...]) * _softplus(ab + dtb_ref[...])
    b_all = _sigmoid(ab)
    gc_all = _dot_mask(jnp.where(incl, 1.0, 0.0).astype(BF16), g_all)
    gl_all = _dot_mask(jnp.where(same, 1.0, 0.0).astype(BF16), g_all)
    gc_t = _dot_rmask(g_all.T, jnp.where(same & (ri <= ci), 1.0, 0.0).astype(BF16))

    r8 = lax.broadcasted_iota(jnp.int32, (8, TS), 0)
    c8 = lax.broadcasted_iota(jnp.int32, (8, TS), 1)
    sel = jnp.where(c8 == r8 * CHUNK, 1.0, 0.0).astype(BF16)
    gl8 = _dot_mask(sel, gl_all)

    def head_group(hs):
        ld, lo, x, rhs = {}, {}, {}, {}
        for h in hs:
            cs = slice(h * D, (h + 1) * D)
            q = l2_normalised(conv_silu(h * D)) * (D ** -0.5)
            k = l2_normalised(conv_silu((H + h) * D))
            v = conv_silu((2 * H + h) * D)
            gc_col = gc_all[:, h:h + 1]
            gl_col = gl_all[:, h:h + 1]
            beta = b_all[:, H + h:H + h + 1]
            decay = jnp.exp2(jnp.minimum(gc_col - gc_t[h:h + 1, :], 0.0))
            eg = jnp.exp2(gc_col)
            kb = k * beta
            kbf = k.astype(BF16)
            lmat = jnp.where(strict, _dot_nt(kb.astype(BF16), kbf) * decay, 0.0)
            attn = jnp.where(incl, _dot_nt(q.astype(BF16), kbf) * decay, 0.0)
            attn_ref[:, h * TS:(h + 1) * TS] = attn.astype(attn_ref.dtype)
            qd_ref[:, cs] = (q * eg).astype(qd_ref.dtype)
            kd_ref[:, cs] = (k * jnp.exp2(gl_col - gc_col)).astype(kd_ref.dtype)
            egl_ref[:, cs] = jnp.exp2(jnp.broadcast_to(gl8[:, h:h + 1], (8, D)))
            ldh = jnp.where(blk, lmat, 0.0)
            ld[h] = ldh.astype(BF16)
            lo[h] = (lmat - ldh).astype(BF16)
            x[h] = eye - ldh
            rhs[h] = jnp.concatenate([v * beta, kb * eg], axis=1).astype(BF16)

        p = {h: _dot(ld[h], ld[h]) for h in hs}
        for _ in range(2):
            pb = {h: p[h].astype(BF16) for h in hs}
            x = {h: x[h] + _mm(x[h], pb[h]) for h in hs}
            p = {h: _dot(pb[h], pb[h]) for h in hs}
        td = {h: x[h] + _mm(x[h], p[h]) for h in hs}
        tdb = {h: td[h].astype(BF16) for h in hs}
        n = {h: _dot(tdb[h], lo[h]).astype(BF16) for h in hs}
        n2 = {h: _dot(n[h], n[h]) for h in hs}
        m1 = {h: td[h] + _mm(n2[h], tdb[h]) for h in hs}
        tinv = {h: m1[h] - _mm(n[h], m1[h]) for h in hs}
        for h in hs:
            cs = slice(h * D, (h + 1) * D)
            uw = _mm(tinv[h], rhs[h])
            u_ref[:, cs] = uw[:, :D].astype(u_ref.dtype)
            w_ref[:, cs] = uw[:, D:].astype(w_ref.dtype)

    for g0 in range(0, H, GDN_HEAD_GROUP):
        head_group(range(g0, g0 + GDN_HEAD_GROUP))
    halo_ref[...] = qkv_ref[TS - 8:TS, :].astype(F32)


def _gdn_prep(plain, ab, conv_w, a_log, dt_bias, B, S):
    T = B * S
    TS = GDN_TS
    ns = S // TS
    HD = GDN_HEADS * GDN_D
    row = lambda b, s: b * ns + s
    vec = pl.BlockSpec((1, LANES), lambda b, s: (0, 0))
    tokw = pl.BlockSpec((TS, HD), lambda b, s: (row(b, s), 0))
    act = jax.ShapeDtypeStruct((T, HD), BF16)
    return pl.pallas_call(
        _gdn_prep_kernel,
        grid=(B, ns),
        in_specs=[pl.BlockSpec((TS, 3 * HD), lambda b, s: (row(b, s), 0)),
                  pl.BlockSpec((TS, LANES), lambda b, s: (row(b, s), 0)),
                  pl.BlockSpec((CONV_WIDTH, 3 * HD), lambda b, s: (0, 0)),
                  vec, vec],
        out_specs=[tokw, tokw, tokw, tokw,
                   pl.BlockSpec((TS, GDN_HEADS * TS), lambda b, s: (row(b, s), 0)),
                   pl.BlockSpec((8, HD), lambda b, s: (row(b, s), 0))],
        out_shape=[act, act, act, act,
                   jax.ShapeDtypeStruct((T, GDN_HEADS * TS), BF16),
                   jax.ShapeDtypeStruct((B * ns * 8, HD), F32)],
        scratch_shapes=[pltpu.VMEM((8, 3 * HD), F32)],
        compiler_params=pltpu.CompilerParams(
            dimension_semantics=("parallel", "arbitrary"), vmem_limit_bytes=VMEM_LIMIT),
        name="gdn_prep",
    )(plain, ab, conv_w, a_log, dt_bias)


def _gdn_scan_kernel(u_ref, w_ref, qd_ref, kd_ref, attn_ref, egl_ref, z_ref, gn_ref, o_ref,
                     state_ref, vnew_ref, oq_ref):
    TS = GDN_TS
    D = GDN_D
    nb = u_ref.shape[0]
    s = pl.program_id(1)

    @pl.when(s == 0)
    def _():
        state_ref[...] = jnp.zeros(state_ref.shape, F32)

    for c in range(TS // CHUNK):
        r = slice(c * CHUNK, (c + 1) * CHUNK)
        for b in range(nb):
            for h in range(GDN_HEADS):
                cs = slice(h * D, (h + 1) * D)
                st = state_ref[b, h]
                wq = jnp.concatenate([w_ref[b, r, cs], qd_ref[b, r, cs]], axis=0)
                ws = _dot(wq, st.astype(BF16))
                v_new = (u_ref[b, r, cs].astype(F32) - ws[:CHUNK]).astype(BF16)
                vnew_ref[b, h, r, :] = v_new
                oq_ref[b, r, cs] = ws[CHUNK:]
                ktv = lax.dot_general(kd_ref[b, r, cs], v_new, (((0,), (0,)), ((), ())),
                                      preferred_element_type=F32)
                state_ref[b, h] = st * egl_ref[b, c:c + 1, cs] + ktv

    for b in range(nb):
        for h in range(GDN_HEADS):
            cs = slice(h * D, (h + 1) * D)
            o = oq_ref[b, :, cs] + _dot(attn_ref[b, :, h * TS:(h + 1) * TS], vnew_ref[b, h])
            o = _rms(o, gn_ref[...]) * _silu(z_ref[b, :, cs].astype(F32))
            o_ref[b, :, cs] = o.astype(o_ref.dtype)


def _gdn_scan(u, w, qd, kd, attn, egl, plain, gdn_norm_g, B, S, nb=4):
    TS = GDN_TS
    ns = S // TS
    HD = GDN_HEADS * GDN_D
    nb = min(nb, B)
    assert B % nb == 0
    seq = lambda a: a.reshape(B, S, a.shape[-1])
    tokw = pl.BlockSpec((nb, TS, HD), lambda b, s: (b, s, 0))
    out = pl.pallas_call(
        _gdn_scan_kernel,
        grid=(B // nb, ns),
        in_specs=[tokw, tokw, tokw, tokw,
                  pl.BlockSpec((nb, TS, GDN_HEADS * TS), lambda b, s: (b, s, 0)),
                  pl.BlockSpec((nb, 8, HD), lambda b, s: (b, s, 0)),
                  pl.BlockSpec((nb, TS, HD), lambda b, s: (b, s, P_Z * LANES // HD)),
                  pl.BlockSpec((1, GDN_D), lambda b, s: (0, 0))],
        out_specs=tokw,
        out_shape=jax.ShapeDtypeStruct((B, S, HD), BF16),
        scratch_shapes=[pltpu.VMEM((nb, GDN_HEADS, GDN_D, GDN_D), F32),
                        pltpu.VMEM((nb, GDN_HEADS, TS, GDN_D), BF16),
                        pltpu.VMEM((nb, TS, HD), F32)],
        compiler_params=pltpu.CompilerParams(
            dimension_semantics=("parallel", "arbitrary"), vmem_limit_bytes=VMEM_LIMIT),
        name="gdn_scan",
    )(seq(u), seq(w), seq(qd), seq(kd), seq(attn), egl.reshape(B, ns * 8, HD), seq(plain), gdn_norm_g)
    return out.reshape(B * S, HD)


def _diff_attn_kernel(q_ref, qn_ref, k_ref, v_ref, lq1_ref, lk1_ref, lq2_ref, lk2_ref, gn_ref, o_ref,
                      vext_ref, qz_ref, qnz_ref, sa_ref, sb_ref, sc_ref, sh_ref, m_ref, acc_ref,
                      *, tq, nq, lam_init):
    i = pl.program_id(2)
    dv = 2 * DIFF_DH
    tk = tq // 2

    def stack(src_ref, dst_ref):
        q = src_ref[...]
        map1 = _is_map1_lane(lax.broadcasted_iota(jnp.int32, q.shape, 1))
        zero = jnp.zeros_like(q)
        q1 = jnp.where(map1, q, zero)
        q2 = jnp.where(map1, zero, q)
        for half in range(2):
            rows = slice(half * tk, (half + 1) * tk)
            dst_ref[2 * half * tk:(2 * half + 1) * tk, :] = q1[rows]
            dst_ref[(2 * half + 1) * tk:(2 * half + 2) * tk, :] = q2[rows]

    def scores(j, s_ref, qsrc_ref, r0=0):
        off = pl.multiple_of(j * tk, tk)
        s_ref[...] = _dot_nt(qsrc_ref[r0:, :], k_ref[pl.ds(off, tk), :])

    def consume(j, s_ref, diag=None, r0=0, first=False):
        off = pl.multiple_of(j * tk, tk)
        sc = s_ref[...]
        if diag is not None:
            rr = lax.broadcasted_iota(jnp.int32, sc.shape, 0)
            cc = lax.broadcasted_iota(jnp.int32, sc.shape, 1)
            keep = (rr & (tk - 1)) >= cc
            if diag == "D1":
                keep = keep | (rr >= tq)
            sc = jnp.where(keep, sc, NEG_BIG)
        rows = slice(r0, 2 * tq)
        row_max = jnp.max(sc, axis=-1, keepdims=True)
        if first:
            m_new = jnp.broadcast_to(row_max, (sc.shape[0], LANES))
        else:
            m_prev = m_ref[rows, :]
            m_new = jnp.maximum(m_prev, row_max)
        p = jnp.concatenate([jnp.exp2(sc[:, c0:c0 + LANES] - m_new) for c0 in range(0, tk, LANES)],
                            axis=1)
        pv = _dot(p.astype(BF16), vext_ref[pl.ds(off, tk), :])
        if first:
            acc_ref[rows, :] = pv
        else:
            alpha = jnp.exp2(m_prev - m_new)
            for c0 in range(0, 2 * dv, LANES):
                acc_ref[rows, c0:c0 + LANES] = (alpha * acc_ref[rows, c0:c0 + LANES]
                                                + pv[:, c0:c0 + LANES])
        m_ref[rows, :] = m_new

    def next_d1():
        nxt = jnp.minimum(i + 1, nq - 1)
        stack(qn_ref, qnz_ref)
        scores(2 * nxt, sc_ref, qnz_ref)

    stack(q_ref, qz_ref)

    @pl.when(i == 0)
    def _():
        vext_ref[:, :dv] = v_ref[...]
        vext_ref[:, dv:] = jnp.ones((vext_ref.shape[0], dv), BF16)
        scores(0, sc_ref, qz_ref)

    scores(2 * i + 1, sh_ref, qz_ref, r0=tq)
    consume(2 * i, sc_ref, diag="D1", first=True)

    @pl.when(i == 0)
    def _():
        next_d1()
        consume(1, sh_ref, diag="D2", r0=tq)

    @pl.when(i > 0)
    def _():
        scores(0, sa_ref, qz_ref)
        consume(2 * i + 1, sh_ref, diag="D2", r0=tq)

        def pair(p, carry):
            j = 2 * p
            scores(j + 1, sb_ref, qz_ref)
            consume(j, sa_ref)
            scores(j + 2, sa_ref, qz_ref)
            consume(j + 1, sb_ref)
            return carry

        lax.fori_loop(0, i - 1, pair, 0)
        scores(2 * i - 1, sb_ref, qz_ref)
        consume(2 * i - 2, sa_ref)
        next_d1()
        consume(2 * i - 1, sb_ref)

    lam = (jnp.exp(jnp.sum(lq1_ref[...] * lk1_ref[...], axis=-1, keepdims=True))
           - jnp.exp(jnp.sum(lq2_ref[...] * lk2_ref[...], axis=-1, keepdims=True))
           + lam_init)
    for half in range(2):
        a1 = acc_ref[2 * half * tk:(2 * half + 1) * tk, :]
        a2 = acc_ref[(2 * half + 1) * tk:(2 * half + 2) * tk, :]
        o = a1[:, :dv] / a1[:, dv:] - lam * (a2[:, :dv] / a2[:, dv:])
        o = _rms(o, gn_ref[...]) * (1.0 - lam_init)
        o_ref[half * tk:(half + 1) * tk, :] = o.astype(o_ref.dtype)


def _diff_attn(rope, plain, lq1, lk1, lq2, lk2, diff_norm_g, lam_init, B, S, tq=1024):
    T = B * S
    nq = S // tq
    dv = 2 * DIFF_DH
    lam_spec = pl.BlockSpec((1, DIFF_DH), lambda b, h, i: (0, 0))
    score_buf = pltpu.VMEM((2 * tq, tq // 2), F32)
    stacked_q = pltpu.VMEM((2 * tq, LANES), BF16)
    return pl.pallas_call(
        functools.partial(_diff_attn_kernel, tq=tq, nq=nq, lam_init=lam_init),
        grid=(B, DIFF_HEADS, nq),
        in_specs=[pl.BlockSpec((tq, LANES), lambda b, h, i: (b * nq + i, R_Q + h)),
                  pl.BlockSpec((tq, LANES),
                               lambda b, h, i: (b * nq + jnp.minimum(i + 1, nq - 1), R_Q + h)),
                  pl.BlockSpec((S, LANES), lambda b, h, i: (b, R_K + h)),
                  pl.BlockSpec((S, LANES), lambda b, h, i: (b, P_VB + h)),
                  lam_spec, lam_spec, lam_spec, lam_spec,
                  pl.BlockSpec((1, dv), lambda b, h, i: (0, 0))],
        out_specs=pl.BlockSpec((tq, LANES), lambda b, h, i: (b * nq + i, h)),
        out_shape=jax.ShapeDtypeStruct((T, DIFF_HEADS * dv), BF16),
        scratch_shapes=[pltpu.VMEM((S, 2 * dv), BF16),
                        stacked_q, stacked_q,
                        score_buf, score_buf, score_buf,
                        pltpu.VMEM((tq, tq // 2), F32),
                        pltpu.VMEM((2 * tq, LANES), F32),
                        pltpu.VMEM((2 * tq, 2 * dv), F32)],
        compiler_params=pltpu.CompilerParams(
            dimension_semantics=("parallel", "parallel", "arbitrary"),
            vmem_limit_bytes=VMEM_LIMIT),
        name="diff_attn",
    )(rope, rope, rope, plain, lq1, lk1, lq2, lk2, diff_norm_g)


def _merge_kernel(oa_ref, ob_ref, ga_ref, gb_ref, x_ref, woa_ref, wob_ref, wo_ref, x1_ref):
    ya = _dot(oa_ref[...], woa_ref[...])
    yb = _dot(ob_ref[...], wob_ref[...])
    merged = _sigmoid(ga_ref[...].astype(F32)) * ya + _sigmoid(gb_ref[...].astype(F32)) * yb
    x1_ref[...] = x_ref[...] + _dot(merged.astype(BF16), wo_ref[...])


def _merge(o_a, o_b, plain, x2d, w_out_a, w_out_b, w_o, tm=512):
    T, D = x2d.shape
    tokd = pl.BlockSpec((tm, D), lambda i: (i, 0))
    wspec = pl.BlockSpec((D, D), lambda i: (0, 0))
    return pl.pallas_call(
        _merge_kernel,
        grid=(T // tm,),
        in_specs=[tokd, tokd,
                  pl.BlockSpec((tm, D), lambda i: (i, P_GA * LANES // D)),
                  pl.BlockSpec((tm, D), lambda i: (i, P_GB * LANES // D)),
                  tokd, wspec, wspec, wspec],
        out_specs=tokd,
        out_shape=jax.ShapeDtypeStruct((T, D), F32),
        compiler_params=pltpu.CompilerParams(
            dimension_semantics=("parallel",), vmem_limit_bytes=VMEM_LIMIT),
        name="merge",
    )(o_a, o_b, plain, plain, x2d, w_out_a, w_out_b, w_o)


def _mem_kv_kernel(m_ref, g_ref, w_ref, o_ref):
    o_ref[...] = _dot(_rms(m_ref[...], g_ref[...]).astype(BF16), w_ref[...]).astype(o_ref.dtype)


def _mem_kv(mem2d, g_mem, w_ckv):
    R, D = mem2d.shape
    N = w_ckv.shape[1]
    return pl.pallas_call(
        _mem_kv_kernel,
        grid=(1,),
        in_specs=[pl.BlockSpec((R, D), lambda i: (0, 0)),
                  pl.BlockSpec((1, D), lambda i: (0, 0)),
                  pl.BlockSpec((D, N), lambda i: (0, 0))],
        out_specs=pl.BlockSpec((R, N), lambda i: (0, 0)),
        out_shape=jax.ShapeDtypeStruct((R, N), BF16),
        compiler_params=pltpu.CompilerParams(vmem_limit_bytes=VMEM_LIMIT),
        name="mem_kv",
    )(mem2d, g_mem, w_ckv)


def _cross_kernel(x_ref, g_ref, wq_ref, kv_ref, wo_ref, o_ref):
    x = x_ref[...]
    hx = _rms(x, g_ref[...]).astype(BF16)
    qc = (_dot(hx, wq_ref[...]) * (X_DH ** -0.5)).astype(BF16)
    xw = X_HEADS * X_DH
    outs = []
    ones = jnp.ones((kv_ref.shape[0], X_DH), BF16)
    for hh in range(X_HEADS):
        cs = slice(hh * X_DH, (hh + 1) * X_DH)
        sc = _dot_nt(qc[:, cs], kv_ref[:, cs])
        m = jnp.broadcast_to(jnp.max(sc, axis=-1, keepdims=True), (sc.shape[0], LANES))
        p = jnp.concatenate([jnp.exp(sc[:, c0:c0 + LANES] - m) for c0 in range(0, sc.shape[1], LANES)],
                            axis=1)
        vext = jnp.concatenate([kv_ref[:, xw + hh * X_DH:xw + (hh + 1) * X_DH], ones], axis=1)
        oe = _dot(p.astype(BF16), vext)
        outs.append(oe[:, :X_DH] / oe[:, X_DH:])
    oc = jnp.concatenate(outs, axis=1).astype(BF16)
    o_ref[...] = x + _dot(oc, wo_ref[...])


def _cross(x1, g_cross, w_cq, kv, w_co, S, M, tm=1024):
    T, D = x1.shape
    xw = X_HEADS * X_DH
    per_b = S // tm
    tokd = pl.BlockSpec((tm, D), lambda i: (i, 0))
    return pl.pallas_call(
        _cross_kernel,
        grid=(T // tm,),
        in_specs=[tokd,
                  pl.BlockSpec((1, D), lambda i: (0, 0)),
                  pl.BlockSpec((D, xw), lambda i: (0, 0)),
                  pl.BlockSpec((M, 2 * xw), lambda i: (i // per_b, 0)),
                  pl.BlockSpec((xw, D), lambda i: (0, 0))],
        out_specs=tokd,
        out_shape=jax.ShapeDtypeStruct((T, D), F32),
        compiler_params=pltpu.CompilerParams(
            dimension_semantics=("parallel",), vmem_limit_bytes=VMEM_LIMIT),
        name="cross",
    )(x1, g_cross, w_cq, kv, w_co)


def _ffn_kernel(x_ref, g_ref, wi_ref, wo_ref, gf_ref, o_ref, *, tf):
    x = x_ref[...]
    hb = _rms(x, g_ref[...]).astype(BF16)
    F = wo_ref.shape[0]
    acc = x
    for f in range(F // tf):
        act = _silu(_dot(hb, wi_ref[:, f * tf:(f + 1) * tf])) * _dot(hb, wi_ref[:, F + f * tf:F + (f + 1) * tf])
        acc = acc + _dot(act.astype(BF16), wo_ref[f * tf:(f + 1) * tf, :])
    o_ref[...] = _rms(acc, gf_ref[...])


def _ffn(x2, g_ffn, w_ffn_in, w_ffn_out, g_final, tm=1024, tf=256):
    T, D = x2.shape
    F = w_ffn_out.shape[0]
    tokd = pl.BlockSpec((tm, D), lambda i: (i, 0))
    vec = pl.BlockSpec((1, D), lambda i: (0, 0))
    resident = pl.Buffered(1)
    return pl.pallas_call(
        functools.partial(_ffn_kernel, tf=tf),
        grid=(T // tm,),
        in_specs=[tokd, vec,
                  pl.BlockSpec((D, 2 * F), lambda i: (0, 0), pipeline_mode=resident),
                  pl.BlockSpec((F, D), lambda i: (0, 0), pipeline_mode=resident),
                  vec],
        out_specs=tokd,
        out_shape=jax.ShapeDtypeStruct((T, D), F32),
        compiler_params=pltpu.CompilerParams(
            dimension_semantics=("parallel",), vmem_limit_bytes=VMEM_LIMIT),
        name="ffn",
    )(x2, g_ffn, w_ffn_in, w_ffn_out, g_final)


def _pad_lanes(v):
    v = v.reshape(1, -1).astype(F32)
    return jnp.pad(v, ((0, 0), (0, LANES - v.shape[1])))


def kernel(x, mem, positions, g_mix, w_in, conv_w, a_log, dt_bias, gdn_norm_g, lambda_q1, lambda_k1, lambda_q2, lambda_k2, diff_norm_g, w_branch_gate, w_out_a, w_out_b, w_o, g_cross, g_mem, w_cq, w_ckv, w_co, g_ffn, w_ffn_in, w_ffn_out, g_final):
    B, S, D = x.shape
    M = mem.shape[1]
    depth = w_in.shape[0]
    assert depth == 1, "the final rmsnorm is fused into the (single) layer's ffn call"
    qkvz = 4 * GDN_HEADS * GDN_D
    xs = x.reshape(B * S, D)
    tables = _rope_tables(positions)
    kv = None
    for l in range(depth):
        ab_end = qkvz + 2 * GDN_HEADS
        qk_end = ab_end + 2 * DIFF_HEADS * 2 * DIFF_DH
        w16 = w_in[l].astype(BF16)
        w_plain = jnp.concatenate([w16[:, :qkvz], w16[:, qk_end:], w_branch_gate[l].astype(BF16)], axis=1)
        w_ab = jnp.pad(w16[:, qkvz:ab_end], ((0, 0), (0, LANES - 2 * GDN_HEADS)))
        plain, ab, h = _in_proj(xs, g_mix[l].reshape(1, D), w_plain, w_ab)
        src = jnp.asarray([hh * LANES + c for hh in range(2 * DIFF_HEADS) for c in _head_lane_source()],
                          dtype=jnp.int32)
        w_qk = jnp.take(w16[:, ab_end:qk_end], src, axis=1)
        rope = _rope_proj(h, w_qk, tables)

        u, w, qd, kd, attn, egl = _gdn_prep(plain, ab, conv_w[l], _pad_lanes(a_log[l]),
                                            _pad_lanes(dt_bias[l]), B, S)
        o_a = _gdn_scan(u, w, qd, kd, attn, egl, plain, gdn_norm_g[l].reshape(1, GDN_D), B, S)

        lam_init = 0.8 - 0.6 * math.exp(-0.3 * l)
        o_b = _diff_attn(rope, plain, lambda_q1[l].reshape(1, -1), lambda_k1[l].reshape(1, -1),
                         lambda_q2[l].reshape(1, -1), lambda_k2[l].reshape(1, -1),
                         diff_norm_g[l].reshape(1, -1), lam_init, B, S)

        xs = _merge(o_a, o_b, plain, xs,w_out_a[l].astype(BF16), w_out_b[l].astype(BF16),
                    w_o[l].astype(BF16))

        kv = _mem_kv(mem.reshape(B * M, D), g_mem[l].reshape(1, D), w_ckv[l].astype(BF16))
        xs = _cross(xs, g_cross[l].reshape(1, D), w_cq[l].astype(BF16), kv, w_co[l].astype(BF16), S, M)

        xs = _ffn(xs, g_ffn[l].reshape(1, D), w_ffn_in[l].astype(BF16), w_ffn_out[l].astype(BF16),
                  g_final.reshape(1, D))
    return xs.reshape(B, S, D)
```

```python
import functools
import math

import jax
import jax.numpy as jnp
from jax import lax
from jax.experimental import pallas as pl
from jax.experimental.pallas import tpu as pltpu

F32 = jnp.float32
BF16 = jnp.bfloat16

EPS = 1e-6
LANES = 128
GDN_HEADS = 8
GDN_D = 128
CONV_WIDTH = 4
CHUNK = 64
DIFF_HEADS = 8
DIFF_DH = 64
ROPE_DIM = DIFF_DH // 4
ROPE_THETA = 500000.0
X_HEADS = 4
X_DH = 128
NEG_BIG = -1e30
LOG2E = math.log2(math.e)

VMEM_LIMIT = 48 * 1024 * 1024

P_QA, P_KA, P_VA, P_Z, P_VB, P_GA, P_GB, P_END = 0, 8, 16, 24, 32, 40, 48, 56
R_Q, R_K = 0, 8


def _dot(a, b):
    return jnp.dot(a, b, preferred_element_type=F32)


def _dot_nt(a, b):
    return lax.dot_general(a, b, (((1,), (1,)), ((), ())), preferred_element_type=F32)


def _mm(a, b):
    return _dot(a.astype(BF16), b.astype(BF16))


def _split3(a):
    hi = a.astype(BF16)
    r = a - hi.astype(F32)
    mid = r.astype(BF16)
    lo = (r - mid.astype(F32)).astype(BF16)
    return hi, mid, lo


def _dot_mask(mask_bf16, b):
    hi, mid, lo = _split3(b)
    return _dot(mask_bf16, hi) + (_dot(mask_bf16, mid) + _dot(mask_bf16, lo))


def _dot_rmask(b, mask_bf16):
    hi, mid, lo = _split3(b)
    return _dot(hi, mask_bf16) + (_dot(mid, mask_bf16) + _dot(lo, mask_bf16))


def _rms(x, g):
    ms = jnp.mean(x * x, axis=-1, keepdims=True)
    return x * lax.rsqrt(ms + EPS) * g


def _sigmoid(x):
    return 0.5 * jnp.tanh(0.5 * x) + 0.5


def _silu(x):
    half = 0.5 * x
    return half * jnp.tanh(half) + half


def _softplus(x):
    return jnp.maximum(x, 0.0) + jnp.log(1.0 + jnp.exp(-jnp.abs(x)))


ROPE_HALF = ROPE_DIM // 2
HALF_LANES = LANES // 2


def _head_lane_source():
    src = [0] * LANES
    for m in range(2):
        for d in range(DIFF_DH):
            if d < ROPE_HALF:
                lane = m * ROPE_HALF + d
            elif d < ROPE_DIM:
                lane = HALF_LANES + m * ROPE_HALF + (d - ROPE_HALF)
            else:
                lane = (ROPE_DIM if m == 0 else HALF_LANES + ROPE_DIM) + (d - ROPE_DIM)
            src[lane] = m * DIFF_DH + d
    return src


def _is_map1_lane(lane):
    return (lane < ROPE_HALF) | ((lane >= ROPE_DIM) & (lane < HALF_LANES + ROPE_HALF))


ROPE_GROUPS = LANES // ROPE_HALF


def _rope_table_kernel(pos_ref, inv_ref, c_ref, sg_ref):
    ang = pos_ref[...] * inv_ref[...]
    cos = jnp.cos(ang)
    sin = jnp.sin(ang)
    rows = ang.shape[0]
    lane = lax.broadcasted_iota(jnp.int32, ang.shape, 1)
    first = lane < ROPE_DIM
    second = (lane >= HALF_LANES) & (lane < HALF_LANES + ROPE_DIM)
    freq = lane & (ROPE_HALF - 1)
    for a in range(ROPE_GROUPS):
        idx = freq + a * ROPE_HALF
        cos_a = jnp.take_along_axis(cos, idx, axis=1)
        sin_a = jnp.take_along_axis(sin, idx, axis=1)
        c_ref[a * rows:(a + 1) * rows, :] = jnp.where(first | second, cos_a, 1.0)
        sg_ref[a * rows:(a + 1) * rows, :] = jnp.where(first, -sin_a, jnp.where(second, sin_a, 0.0))


def _rope_tables(positions, tm=1024):
    T = positions.size
    rows = tm // ROPE_GROUPS
    pos = positions.astype(F32).reshape(T // tm, ROPE_GROUPS, rows).transpose(0, 2, 1)
    pos = jnp.repeat(pos, ROPE_HALF, axis=2).reshape(T // ROPE_GROUPS, LANES)
    inv_freq = ROPE_THETA ** (-jnp.arange(0, ROPE_DIM, 2, dtype=F32) / ROPE_DIM)
    inv = jnp.tile(inv_freq, ROPE_GROUPS).reshape(1, LANES)
    out = jax.ShapeDtypeStruct((T, LANES), F32)
    return pl.pallas_call(
        _rope_table_kernel,
        grid=(T // tm,),
        in_specs=[pl.BlockSpec((rows, LANES), lambda i: (i, 0)),
                  pl.BlockSpec((1, LANES), lambda i: (0, 0))],
        out_specs=[pl.BlockSpec((tm, LANES), lambda i: (i, 0))] * 2,
        out_shape=[out, out],
        name="rope_tables",
    )(pos, inv)


PROJ_SUB = 256


def _in_proj_kernel(x_ref, g_ref, w_ref, wab_ref, out_ref, ab_ref, h_ref):
    @pl.when(pl.program_id(1) == 0)
    def _():
        hb = _rms(x_ref[...], g_ref[...]).astype(BF16)
        h_ref[...] = hb
        ab_ref[...] = _dot(hb, wab_ref[...])

    hb = h_ref[...]
    for c0 in range(0, out_ref.shape[1], PROJ_SUB):
        cs = slice(c0, c0 + PROJ_SUB)
        out_ref[:, cs] = _dot(hb, w_ref[:, cs]).astype(out_ref.dtype)


def _in_proj(x2d, g_mix, w_plain, w_ab, tm=1024, tn=1792):
    T, D = x2d.shape
    N = w_plain.shape[1]
    tm = min(tm, T)
    return pl.pallas_call(
        _in_proj_kernel,
        grid=(T // tm, N // tn),
        in_specs=[pl.BlockSpec((tm, D), lambda i, j: (i, 0)),
                  pl.BlockSpec((1, D), lambda i, j: (0, 0)),
                  pl.BlockSpec((D, tn), lambda i, j: (0, j)),
                  pl.BlockSpec((D, LANES), lambda i, j: (0, 0))],
        out_specs=[pl.BlockSpec((tm, tn), lambda i, j: (i, j)),
                   pl.BlockSpec((tm, LANES), lambda i, j: (i, 0)),
                   pl.BlockSpec((tm, D), lambda i, j: (i, 0))],
        out_shape=[jax.ShapeDtypeStruct((T, N), BF16),
                   jax.ShapeDtypeStruct((T, LANES), F32),
                   jax.ShapeDtypeStruct((T, D), BF16)],
        compiler_params=pltpu.CompilerParams(
            dimension_semantics=("parallel", "arbitrary"), vmem_limit_bytes=VMEM_LIMIT),
        name="in_proj",
    )(x2d, g_mix, w_plain, w_ab)


def _rope_proj_kernel(h_ref, w_ref, c_ref, sg_ref, out_ref):
    scale = jnp.where(pl.program_id(1) == 0, DIFF_DH ** -0.5 * LOG2E, 1.0)
    c = c_ref[...] * scale
    sg = sg_ref[...] * scale
    hb = h_ref[...]
    for c0 in range(0, out_ref.shape[1], PROJ_SUB):
        acc = _dot(hb, w_ref[:, c0:c0 + PROJ_SUB])
        for l0 in range(0, PROJ_SUB, LANES):
            a = acc[:, l0:l0 + LANES]
            y = a * c + pltpu.roll(a, HALF_LANES, 1) * sg
            out_ref[:, c0 + l0:c0 + l0 + LANES] = y.astype(out_ref.dtype)


def _rope_proj(h, w_qk, tables, tm=1024):
    T, D = h.shape
    N = w_qk.shape[1]
    tn = N // 2
    tm = min(tm, T)
    return pl.pallas_call(
        _rope_proj_kernel,
        grid=(T // tm, 2),
        in_specs=[pl.BlockSpec((tm, D), lambda i, j: (i, 0)),
                  pl.BlockSpec((D, tn), lambda i, j: (0, j)),
                  pl.BlockSpec((tm, LANES), lambda i, j: (i, 0)),
                  pl.BlockSpec((tm, LANES), lambda i, j: (i, 0))],
        out_specs=pl.BlockSpec((tm, tn), lambda i, j: (i, j)),
        out_shape=jax.ShapeDtypeStruct((T, N), BF16),
        compiler_params=pltpu.CompilerParams(
            dimension_semantics=("parallel", "parallel"), vmem_limit_bytes=VMEM_LIMIT),
        name="rope_proj",
    )(h, w_qk, *tables)


GDN_TS = 4 * CHUNK
GDN_HEAD_GROUP = 4
SOLVE_BLOCK = 16
CHUNK_SHIFT = CHUNK.bit_length() - 1
SOLVE_SHIFT = SOLVE_BLOCK.bit_length() - 1
assert (1 << CHUNK_SHIFT) == CHUNK and (1 << SOLVE_SHIFT) == SOLVE_BLOCK and CHUNK // SOLVE_BLOCK == 4


def _gdn_prep_kernel(qkv_ref, ab_ref, cw_ref, alog_ref, dtb_ref,
                     u_ref, w_ref, qd_ref, kd_ref, attn_ref, egl_ref, halo_ref):
    TS, H, D = GDN_TS, GDN_HEADS, GDN_D

    @pl.when(pl.program_id(1) == 0)
    def _():
        halo_ref[...] = jnp.zeros(halo_ref.shape, F32)

    def conv_silu(c0):
        cs = slice(c0, c0 + D)
        ext = jnp.concatenate([halo_ref[:, cs], qkv_ref[:, cs].astype(F32)], axis=0)
        z = ext * cw_ref[0:1, cs]
        for t in range(1, CONV_WIDTH):
            z = pltpu.roll(z, 1, 0) + ext * cw_ref[t:t + 1, cs]
        return _silu(z[8:])

    def l2_normalised(x):
        return x * lax.rsqrt(jnp.sum(x * x, axis=-1, keepdims=True) + EPS)

    ri = lax.broadcasted_iota(jnp.int32, (TS, TS), 0)
    ci = lax.broadcasted_iota(jnp.int32, (TS, TS), 1)
    same = (ri >> CHUNK_SHIFT) == (ci >> CHUNK_SHIFT)
    incl = same & (ri >= ci)
    strict = same & (ri > ci)
    blk = (ri >> SOLVE_SHIFT) == (ci >> SOLVE_SHIFT)
    eye = jnp.where(ri == ci, 1.0, 0.0)

    ab = ab_ref[...]
    g_all = -jnp.exp(alog_ref[...]) * _softplus(ab + dtb_ref[...])
    b_all = _sigmoid(ab)
    gc_all = _dot_mask(jnp.where(incl, 1.0, 0.0).astype(BF16), g_all)
    gl_all = _dot_mask(jnp.where(same, 1.0, 0.0).astype(BF16), g_all)
    gc_t = _dot_rmask(g_all.T, jnp.where(same & (ri <= ci), 1.0, 0.0).astype(BF16))

    r8 = lax.broadcasted_iota(jnp.int32, (8, TS), 0)
    c8 = lax.broadcasted_iota(jnp.int32, (8, TS), 1)
    sel = jnp.where(c8 == r8 * CHUNK, 1.0, 0.0).astype(BF16)
    gl8 = _dot_mask(sel, gl_all)

    def head_group(hs):
        ld, lo, x, rhs = {}, {}, {}, {}
        for h in hs:
            cs = slice(h * D, (h + 1) * D)
            q = l2_normalised(conv_silu(h * D)) * (D ** -0.5)
            k = l2_normalised(conv_silu((H + h) * D))
            v = conv_silu((2 * H + h) * D)
            gc_col = gc_all[:, h:h + 1]
            gl_col = gl_all[:, h:h + 1]
            beta = b_all[:, H + h:H + h + 1]
            decay = jnp.exp(jnp.minimum(gc_col - gc_t[h:h + 1, :], 0.0))
            eg = jnp.exp(gc_col)
            kb = k * beta
            kbf = k.astype(BF16)
            lmat = jnp.where(strict, _dot_nt(kb.astype(BF16), kbf) * decay, 0.0)
            attn = jnp.where(incl, _dot_nt(q.astype(BF16), kbf) * decay, 0.0)
            attn_ref[:, h * TS:(h + 1) * TS] = attn.astype(attn_ref.dtype)
            qd_ref[:, cs] = (q * eg).astype(qd_ref.dtype)
            kd_ref[:, cs] = (k * jnp.exp(gl_col - gc_col)).astype(kd_ref.dtype)
            egl_ref[:, cs] = jnp.exp(jnp.broadcast_to(gl8[:, h:h + 1], (8, D)))
            ldh = jnp.where(blk, lmat, 0.0)
            ld[h] = ldh.astype(BF16)
            lo[h] = (lmat - ldh).astype(BF16)
            x[h] = eye - ldh
            rhs[h] = jnp.concatenate([v * beta, kb * eg], axis=1).astype(BF16)

        p = {h: _dot(ld[h], ld[h]) for h in hs}
        for _ in range(2):
            pb = {h: p[h].astype(BF16) for h in hs}
            x = {h: x[h] + _mm(x[h], pb[h]) for h in hs}
            p = {h: _dot(pb[h], pb[h]) for h in hs}
        td = {h: x[h] + _mm(x[h], p[h]) for h in hs}
        tdb = {h: td[h].astype(BF16) for h in hs}
        n = {h: _dot(tdb[h], lo[h]).astype(BF16) for h in hs}
        n2 = {h: _dot(n[h], n[h]) for h in hs}
        m1 = {h: td[h] + _mm(n2[h], tdb[h]) for h in hs}
        tinv = {h: m1[h] - _mm(n[h], m1[h]) for h in hs}
        for h in hs:
            cs = slice(h * D, (h + 1) * D)
            uw = _mm(tinv[h], rhs[h])
            u_ref[:, cs] = uw[:, :D].astype(u_ref.dtype)
            w_ref[:, cs] = uw[:, D:].astype(w_ref.dtype)

    for g0 in range(0, H, GDN_HEAD_GROUP):
        head_group(range(g0, g0 + GDN_HEAD_GROUP))
    halo_ref[...] = qkv_ref[TS - 8:TS, :].astype(F32)


def _gdn_prep(plain, ab, conv_w, a_log, dt_bias, B, S):
    T = B * S
    TS = GDN_TS
    ns = S // TS
    HD = GDN_HEADS * GDN_D
    row = lambda b, s: b * ns + s
    vec = pl.BlockSpec((1, LANES), lambda b, s: (0, 0))
    tokw = pl.BlockSpec((TS, HD), lambda b, s: (row(b, s), 0))
    act = jax.ShapeDtypeStruct((T, HD), BF16)
    return pl.pallas_call(
        _gdn_prep_kernel,
        grid=(B, ns),
        in_specs=[pl.BlockSpec((TS, 3 * HD), lambda b, s: (row(b, s), 0)),
                  pl.BlockSpec((TS, LANES), lambda b, s: (row(b, s), 0)),
                  pl.BlockSpec((CONV_WIDTH, 3 * HD), lambda b, s: (0, 0)),
                  vec, vec],
        out_specs=[tokw, tokw, tokw, tokw,
                   pl.BlockSpec((TS, GDN_HEADS * TS), lambda b, s: (row(b, s), 0)),
                   pl.BlockSpec((8, HD), lambda b, s: (row(b, s), 0))],
        out_shape=[act, act, act, act,
                   jax.ShapeDtypeStruct((T, GDN_HEADS * TS), BF16),
                   jax.ShapeDtypeStruct((B * ns * 8, HD), F32)],
        scratch_shapes=[pltpu.VMEM((8, 3 * HD), F32)],
        compiler_params=pltpu.CompilerParams(
            dimension_semantics=("parallel", "arbitrary"), vmem_limit_bytes=VMEM_LIMIT),
        name="gdn_prep",
    )(plain, ab, conv_w, a_log, dt_bias)


def _gdn_scan_kernel(u_ref, w_ref, qd_ref, kd_ref, attn_ref, egl_ref, gn_ref, o_ref,
                     state_ref, vnew_ref, oq_ref):
    TS = GDN_TS
    D = GDN_D
    nb = u_ref.shape[0]
    s = pl.program_id(1)

    @pl.when(s == 0)
    def _():
        state_ref[...] = jnp.zeros(state_ref.shape, F32)

    for c in range(TS // CHUNK):
        r = slice(c * CHUNK, (c + 1) * CHUNK)
        for b in range(nb):
            for h in range(GDN_HEADS):
                cs = slice(h * D, (h + 1) * D)
                st = state_ref[b, h]
                wq = jnp.concatenate([w_ref[b, r, cs], qd_ref[b, r, cs]], axis=0)
                ws = _dot(wq, st.astype(BF16))
                v_new = (u_ref[b, r, cs].astype(F32) - ws[:CHUNK]).astype(BF16)
                vnew_ref[b, h, r, :] = v_new
                oq_ref[b, r, cs] = ws[CHUNK:]
                ktv = lax.dot_general(kd_ref[b, r, cs], v_new, (((0,), (0,)), ((), ())),
                                      preferred_element_type=F32)
                state_ref[b, h] = st * egl_ref[b, c:c + 1, cs] + ktv

    for b in range(nb):
        for h in range(GDN_HEADS):
            cs = slice(h * D, (h + 1) * D)
            o = oq_ref[b, :, cs] + _dot(attn_ref[b, :, h * TS:(h + 1) * TS], vnew_ref[b, h])
            o_ref[b, :, cs] = _rms(o, gn_ref[...]).astype(o_ref.dtype)


def _gdn_scan(u, w, qd, kd, attn, egl, gdn_norm_g, B, S, nb=4):
    TS = GDN_TS
    ns = S // TS
    HD = GDN_HEADS * GDN_D
    nb = min(nb, B)
    assert B % nb == 0
    seq = lambda a: a.reshape(B, S, a.shape[-1])
    tokw = pl.BlockSpec((nb, TS, HD), lambda b, s: (b, s, 0))
    out = pl.pallas_call(
        _gdn_scan_kernel,
        grid=(B // nb, ns),
        in_specs=[tokw, tokw, tokw, tokw,
                  pl.BlockSpec((nb, TS, GDN_HEADS * TS), lambda b, s: (b, s, 0)),
                  pl.BlockSpec((nb, 8, HD), lambda b, s: (b, s, 0)),
                  pl.BlockSpec((1, GDN_D), lambda b, s: (0, 0))],
        out_specs=tokw,
        out_shape=jax.ShapeDtypeStruct((B, S, HD), BF16),
        scratch_shapes=[pltpu.VMEM((nb, GDN_HEADS, GDN_D, GDN_D), F32),
                        pltpu.VMEM((nb, GDN_HEADS, TS, GDN_D), BF16),
                        pltpu.VMEM((nb, TS, HD), F32)],
        compiler_params=pltpu.CompilerParams(
            dimension_semantics=("parallel", "arbitrary"), vmem_limit_bytes=VMEM_LIMIT),
        name="gdn_scan",
    )(seq(u), seq(w), seq(qd), seq(kd), seq(attn), egl.reshape(B, ns * 8, HD), gdn_norm_g)
    return out.reshape(B * S, HD)


def _diff_attn_kernel(q_ref, qn_ref, k_ref, v_ref, lq1_ref, lk1_ref, lq2_ref, lk2_ref, gn_ref, o_ref,
                      vext_ref, qz_ref, qnz_ref, sa_ref, sb_ref, sc_ref, sh_ref, m_ref, acc_ref,
                      *, tq, nq, lam_init):
    i = pl.program_id(2)
    dv = 2 * DIFF_DH
    tk = tq // 2

    def stack(src_ref, dst_ref):
        q = src_ref[...]
        map1 = _is_map1_lane(lax.broadcasted_iota(jnp.int32, q.shape, 1))
        zero = jnp.zeros_like(q)
        q1 = jnp.where(map1, q, zero)
        q2 = jnp.where(map1, zero, q)
        for half in range(2):
            rows = slice(half * tk, (half + 1) * tk)
            dst_ref[2 * half * tk:(2 * half + 1) * tk, :] = q1[rows]
            dst_ref[(2 * half + 1) * tk:(2 * half + 2) * tk, :] = q2[rows]

    def scores(j, s_ref, qsrc_ref, r0=0):
        off = pl.multiple_of(j * tk, tk)
        s_ref[...] = _dot_nt(qsrc_ref[r0:, :], k_ref[pl.ds(off, tk), :])

    def consume(j, s_ref, diag=None, r0=0, first=False):
        off = pl.multiple_of(j * tk, tk)
        sc = s_ref[...]
        if diag is not None:
            rr = lax.broadcasted_iota(jnp.int32, sc.shape, 0)
            cc = lax.broadcasted_iota(jnp.int32, sc.shape, 1)
            keep = (rr & (tk - 1)) >= cc
            if diag == "D1":
                keep = keep | (rr >= tq)
            sc = jnp.where(keep, sc, NEG_BIG)
        rows = slice(r0, 2 * tq)
        row_max = jnp.max(sc, axis=-1, keepdims=True)
        if first:
            m_new = jnp.broadcast_to(row_max, (sc.shape[0], LANES))
        else:
            m_prev = m_ref[rows, :]
            m_new = jnp.maximum(m_prev, row_max)
        p = jnp.concatenate([jnp.exp2(sc[:, c0:c0 + LANES] - m_new) for c0 in range(0, tk, LANES)],
                            axis=1)
        pv = _dot(p.astype(BF16), vext_ref[pl.ds(off, tk), :])
        if first:
            acc_ref[rows, :] = pv
        else:
            alpha = jnp.exp2(m_prev - m_new)
            for c0 in range(0, 2 * dv, LANES):
                acc_ref[rows, c0:c0 + LANES] = (alpha * acc_ref[rows, c0:c0 + LANES]
                                                + pv[:, c0:c0 + LANES])
        m_ref[rows, :] = m_new

    def next_d1():
        nxt = jnp.minimum(i + 1, nq - 1)
        stack(qn_ref, qnz_ref)
        scores(2 * nxt, sc_ref, qnz_ref)

    stack(q_ref, qz_ref)

    @pl.when(i == 0)
    def _():
        vext_ref[:, :dv] = v_ref[...]
        vext_ref[:, dv:] = jnp.ones((vext_ref.shape[0], dv), BF16)
        scores(0, sc_ref, qz_ref)

    scores(2 * i + 1, sh_ref, qz_ref, r0=tq)
    consume(2 * i, sc_ref, diag="D1", first=True)

    @pl.when(i == 0)
    def _():
        next_d1()
        consume(1, sh_ref, diag="D2", r0=tq)

    @pl.when(i > 0)
    def _():
        scores(0, sa_ref, qz_ref)
        consume(2 * i + 1, sh_ref, diag="D2", r0=tq)

        def pair(p, carry):
            j = 2 * p
            scores(j + 1, sb_ref, qz_ref)
            consume(j, sa_ref)
            scores(j + 2, sa_ref, qz_ref)
            consume(j + 1, sb_ref)
            return carry

        lax.fori_loop(0, i - 1, pair, 0)
        scores(2 * i - 1, sb_ref, qz_ref)
        consume(2 * i - 2, sa_ref)
        next_d1()
        consume(2 * i - 1, sb_ref)

    lam = (jnp.exp(jnp.sum(lq1_ref[...] * lk1_ref[...], axis=-1, keepdims=True))
           - jnp.exp(jnp.sum(lq2_ref[...] * lk2_ref[...], axis=-1, keepdims=True))
           + lam_init)
    for half in range(2):
        a1 = acc_ref[2 * half * tk:(2 * half + 1) * tk, :]
        a2 = acc_ref[(2 * half + 1) * tk:(2 * half + 2) * tk, :]
        o = a1[:, :dv] / a1[:, dv:] - lam * (a2[:, :dv] / a2[:, dv:])
        o = _rms(o, gn_ref[...]) * (1.0 - lam_init)
        o_ref[half * tk:(half + 1) * tk, :] = o.astype(o_ref.dtype)


def _diff_attn(rope, plain, lq1, lk1, lq2, lk2, diff_norm_g, lam_init, B, S, tq=1024):
    T = B * S
    nq = S // tq
    dv = 2 * DIFF_DH
    lam_spec = pl.BlockSpec((1, DIFF_DH), lambda b, h, i: (0, 0))
    score_buf = pltpu.VMEM((2 * tq, tq // 2), F32)
    stacked_q = pltpu.VMEM((2 * tq, LANES), BF16)
    return pl.pallas_call(
        functools.partial(_diff_attn_kernel, tq=tq, nq=nq, lam_init=lam_init),
        grid=(B, DIFF_HEADS, nq),
        in_specs=[pl.BlockSpec((tq, LANES), lambda b, h, i: (b * nq + i, R_Q + h)),
                  pl.BlockSpec((tq, LANES),
                               lambda b, h, i: (b * nq + jnp.minimum(i + 1, nq - 1), R_Q + h)),
                  pl.BlockSpec((S, LANES), lambda b, h, i: (b, R_K + h)),
                  pl.BlockSpec((S, LANES), lambda b, h, i: (b, P_VB + h)),
                  lam_spec, lam_spec, lam_spec, lam_spec,
                  pl.BlockSpec((1, dv), lambda b, h, i: (0, 0))],
        out_specs=pl.BlockSpec((tq, LANES), lambda b, h, i: (b * nq + i, h)),
        out_shape=jax.ShapeDtypeStruct((T, DIFF_HEADS * dv), BF16),
        scratch_shapes=[pltpu.VMEM((S, 2 * dv), BF16),
                        stacked_q, stacked_q,
                        score_buf, score_buf, score_buf,
                        pltpu.VMEM((tq, tq // 2), F32),
                        pltpu.VMEM((2 * tq, LANES), F32),
                        pltpu.VMEM((2 * tq, 2 * dv), F32)],
        compiler_params=pltpu.CompilerParams(
            dimension_semantics=("parallel", "parallel", "arbitrary"),
            vmem_limit_bytes=VMEM_LIMIT),
        name="diff_attn",
    )(rope, rope, rope, plain, lq1, lk1, lq2, lk2, diff_norm_g)


def _merge_kernel(oa_ref, z_ref, ob_ref, ga_ref, gb_ref, x_ref, woa_ref, wob_ref, wo_ref, x1_ref):
    oa = (oa_ref[...].astype(F32) * _silu(z_ref[...].astype(F32))).astype(BF16)
    ya = _dot(oa, woa_ref[...])
    yb = _dot(ob_ref[...], wob_ref[...])
    merged = _sigmoid(ga_ref[...].astype(F32)) * ya + _sigmoid(gb_ref[...].astype(F32)) * yb
    x1_ref[...] = x_ref[...] + _dot(merged.astype(BF16), wo_ref[...])


def _merge(o_a, o_b, plain, x2d, w_out_a, w_out_b, w_o, tm=512):
    T, D = x2d.shape
    tokd = pl.BlockSpec((tm, D), lambda i: (i, 0))
    wspec = pl.BlockSpec((D, D), lambda i: (0, 0))
    return pl.pallas_call(
        _merge_kernel,
        grid=(T // tm,),
        in_specs=[tokd,
                  pl.BlockSpec((tm, D), lambda i: (i, P_Z * LANES // D)),
                  tokd,
                  pl.BlockSpec((tm, D), lambda i: (i, P_GA * LANES // D)),
                  pl.BlockSpec((tm, D), lambda i: (i, P_GB * LANES // D)),
                  tokd, wspec, wspec, wspec],
        out_specs=tokd,
        out_shape=jax.ShapeDtypeStruct((T, D), F32),
        compiler_params=pltpu.CompilerParams(
            dimension_semantics=("parallel",), vmem_limit_bytes=VMEM_LIMIT),
        name="merge",
    )(o_a, plain, o_b, plain, plain, x2d, w_out_a, w_out_b, w_o)


def _mem_kv_kernel(m_ref, g_ref, w_ref, o_ref):
    o_ref[...] = _dot(_rms(m_ref[...], g_ref[...]).astype(BF16), w_ref[...]).astype(o_ref.dtype)


def _mem_kv(mem2d, g_mem, w_ckv):
    R, D = mem2d.shape
    N = w_ckv.shape[1]
    return pl.pallas_call(
        _mem_kv_kernel,
        grid=(1,),
        in_specs=[pl.BlockSpec((R, D), lambda i: (0, 0)),
                  pl.BlockSpec((1, D), lambda i: (0, 0)),
                  pl.BlockSpec((D, N), lambda i: (0, 0))],
        out_specs=pl.BlockSpec((R, N), lambda i: (0, 0)),
        out_shape=jax.ShapeDtypeStruct((R, N), BF16),
        compiler_params=pltpu.CompilerParams(vmem_limit_bytes=VMEM_LIMIT),
        name="mem_kv",
    )(mem2d, g_mem, w_ckv)


def _cross_kernel(x_ref, g_ref, wq_ref, kv_ref, wo_ref, o_ref):
    x = x_ref[...]
    hx = _rms(x, g_ref[...]).astype(BF16)
    qc = (_dot(hx, wq_ref[...]) * (X_DH ** -0.5)).astype(BF16)
    xw = X_HEADS * X_DH
    outs = []
    ones = jnp.ones((kv_ref.shape[0], X_DH), BF16)
    for hh in range(X_HEADS):
        cs = slice(hh * X_DH, (hh + 1) * X_DH)
        sc = _dot_nt(qc[:, cs], kv_ref[:, cs])
        m = jnp.broadcast_to(jnp.max(sc, axis=-1, keepdims=True), (sc.shape[0], LANES))
        p = jnp.concatenate([jnp.exp(sc[:, c0:c0 + LANES] - m) for c0 in range(0, sc.shape[1], LANES)],
                            axis=1)
        vext = jnp.concatenate([kv_ref[:, xw + hh * X_DH:xw + (hh + 1) * X_DH], ones], axis=1)
        oe = _dot(p.astype(BF16), vext)
        outs.append(oe[:, :X_DH] / oe[:, X_DH:])
    oc = jnp.concatenate(outs, axis=1).astype(BF16)
    o_ref[...] = x + _dot(oc, wo_ref[...])


def _cross(x1, g_cross, w_cq, kv, w_co, S, M, tm=1024):
    T, D = x1.shape
    xw = X_HEADS * X_DH
    per_b = S // tm
    tokd = pl.BlockSpec((tm, D), lambda i: (i, 0))
    return pl.pallas_call(
        _cross_kernel,
        grid=(T // tm,),
        in_specs=[tokd,
                  pl.BlockSpec((1, D), lambda i: (0, 0)),
                  pl.BlockSpec((D, xw), lambda i: (0, 0)),
                  pl.BlockSpec((M, 2 * xw), lambda i: (i // per_b, 0)),
                  pl.BlockSpec((xw, D), lambda i: (0, 0))],
        out_specs=tokd,
        out_shape=jax.ShapeDtypeStruct((T, D), F32),
        compiler_params=pltpu.CompilerParams(
            dimension_semantics=("parallel",), vmem_limit_bytes=VMEM_LIMIT),
        name="cross",
    )(x1, g_cross, w_cq, kv, w_co)


def _ffn_kernel(x_ref, g_ref, wi_ref, wo_ref, gf_ref, o_ref, *, tf):
    x = x_ref[...]
    hb = _rms(x, g_ref[...]).astype(BF16)
    F = wo_ref.shape[0]
    acc = x
    for f in range(F // tf):
        act = _silu(_dot(hb, wi_ref[:, f * tf:(f + 1) * tf])) * _dot(hb, wi_ref[:, F + f * tf:F + (f + 1) * tf])
        acc = acc + _dot(act.astype(BF16), wo_ref[f * tf:(f + 1) * tf, :])
    o_ref[...] = _rms(acc, gf_ref[...])


def _ffn(x2, g_ffn, w_ffn_in, w_ffn_out, g_final, tm=1024, tf=256):
    T, D = x2.shape
    F = w_ffn_out.shape[0]
    tokd = pl.BlockSpec((tm, D), lambda i: (i, 0))
    vec = pl.BlockSpec((1, D), lambda i: (0, 0))
    resident = pl.Buffered(1)
    return pl.pallas_call(
        functools.partial(_ffn_kernel, tf=tf),
        grid=(T // tm,),
        in_specs=[tokd, vec,
                  pl.BlockSpec((D, 2 * F), lambda i: (0, 0), pipeline_mode=resident),
                  pl.BlockSpec((F, D), lambda i: (0, 0), pipeline_mode=resident),
                  vec],
        out_specs=tokd,
        out_shape=jax.ShapeDtypeStruct((T, D), F32),
        compiler_params=pltpu.CompilerParams(
            dimension_semantics=("parallel",), vmem_limit_bytes=VMEM_LIMIT),
        name="ffn",
    )(x2, g_ffn, w_ffn_in, w_ffn_out, g_final)


def _pad_lanes(v):
    v = v.reshape(1, -1).astype(F32)
    return jnp.pad(v, ((0, 0), (0, LANES - v.shape[1])))


def kernel(x, mem, positions, g_mix, w_in, conv_w, a_log, dt_bias, gdn_norm_g, lambda_q1, lambda_k1, lambda_q2, lambda_k2, diff_norm_g, w_branch_gate, w_out_a, w_out_b, w_o, g_cross, g_mem, w_cq, w_ckv, w_co, g_ffn, w_ffn_in, w_ffn_out, g_final):
    B, S, D = x.shape
    M = mem.shape[1]
    depth = w_in.shape[0]
    assert depth == 1, "the final rmsnorm is fused into the (single) layer's ffn call"
    qkvz = 4 * GDN_HEADS * GDN_D
    xs = x.reshape(B * S, D)
    tables = _rope_tables(positions)
    kv = None
    for l in range(depth):
        ab_end = qkvz + 2 * GDN_HEADS
        qk_end = ab_end + 2 * DIFF_HEADS * 2 * DIFF_DH
        w16 = w_in[l].astype(BF16)
        w_plain = jnp.concatenate([w16[:, :qkvz], w16[:, qk_end:], w_branch_gate[l].astype(BF16)], axis=1)
        w_ab = jnp.pad(w16[:, qkvz:ab_end], ((0, 0), (0, LANES - 2 * GDN_HEADS)))
        plain, ab, h = _in_proj(xs, g_mix[l].reshape(1, D), w_plain, w_ab)
        src = jnp.asarray([hh * LANES + c for hh in range(2 * DIFF_HEADS) for c in _head_lane_source()],
                          dtype=jnp.int32)
        w_qk = jnp.take(w16[:, ab_end:qk_end], src, axis=1)
        rope = _rope_proj(h, w_qk, tables)

        u, w, qd, kd, attn, egl = _gdn_prep(plain, ab, conv_w[l], _pad_lanes(a_log[l]),
                                            _pad_lanes(dt_bias[l]), B, S)
        o_a = _gdn_scan(u, w, qd, kd, attn, egl, gdn_norm_g[l].reshape(1, GDN_D), B, S)

        lam_init = 0.8 - 0.6 * math.exp(-0.3 * l)
        o_b = _diff_attn(rope, plain, lambda_q1[l].reshape(1, -1), lambda_k1[l].reshape(1, -1),
                         lambda_q2[l].reshape(1, -1), lambda_k2[l].reshape(1, -1),
                         diff_norm_g[l].reshape(1, -1), lam_init, B, S)

        xs = _merge(o_a, o_b, plain, xs,w_out_a[l].astype(BF16), w_out_b[l].astype(BF16),
                    w_o[l].astype(BF16))

        kv = _mem_kv(mem.reshape(B * M, D), g_mem[l].reshape(1, D), w_ckv[l].astype(BF16))
        xs = _cross(xs, g_cross[l].reshape(1, D), w_cq[l].astype(BF16), kv, w_co[l].astype(BF16), S, M)

        xs = _ffn(xs, g_ffn[l].reshape(1, D), w_ffn_in[l].astype(BF16), w_ffn_out[l].astype(BF16),
                  g_final.reshape(1, D))
    return xs.reshape(B, S, D)
```

```python
import functools
import math

import jax
import jax.numpy as jnp
from jax import lax
from jax.experimental import pallas as pl
from jax.experimental.pallas import tpu as pltpu

F32 = jnp.float32
BF16 = jnp.bfloat16

EPS = 1e-6
LANES = 128
GDN_HEADS = 8
GDN_D = 128
CONV_WIDTH = 4
CHUNK = 64
DIFF_HEADS = 8
DIFF_DH = 64
ROPE_DIM = DIFF_DH // 4
ROPE_THETA = 500000.0
X_HEADS = 4
X_DH = 128
NEG_BIG = -1e30
LOG2E = math.log2(math.e)

VMEM_LIMIT = 48 * 1024 * 1024

P_QA, P_KA, P_VA, P_Z, P_VB, P_GA, P_GB, P_END = 0, 8, 16, 24, 32, 40, 48, 56
R_Q, R_K = 0, 8


def _dot(a, b):
    return jnp.dot(a, b, preferred_element_type=F32)


def _dot_nt(a, b):
    return lax.dot_general(a, b, (((1,), (1,)), ((), ())), preferred_element_type=F32)


def _mm(a, b):
    return _dot(a.astype(BF16), b.astype(BF16))


def _split3(a):
    hi = a.astype(BF16)
    r = a - hi.astype(F32)
    mid = r.astype(BF16)
    lo = (r - mid.astype(F32)).astype(BF16)
    return hi, mid, lo


def _dot_mask(mask_bf16, b):
    hi, mid, lo = _split3(b)
    return _dot(mask_bf16, hi) + (_dot(mask_bf16, mid) + _dot(mask_bf16, lo))


def _dot_rmask(b, mask_bf16):
    hi, mid, lo = _split3(b)
    return _dot(hi, mask_bf16) + (_dot(mid, mask_bf16) + _dot(lo, mask_bf16))


def _rms(x, g):
    ms = jnp.mean(x * x, axis=-1, keepdims=True)
    return x * lax.rsqrt(ms + EPS) * g


def _sigmoid(x):
    return 0.5 * jnp.tanh(0.5 * x) + 0.5


def _silu(x):
    half = 0.5 * x
    return half * jnp.tanh(half) + half


def _softplus(x):
    return jnp.maximum(x, 0.0) + jnp.log(1.0 + jnp.exp(-jnp.abs(x)))


ROPE_HALF = ROPE_DIM // 2
HALF_LANES = LANES // 2


def _head_lane_source():
    src = [0] * LANES
    for m in range(2):
        for d in range(DIFF_DH):
            if d < ROPE_HALF:
                lane = m * ROPE_HALF + d
            elif d < ROPE_DIM:
                lane = HALF_LANES + m * ROPE_HALF + (d - ROPE_HALF)
            else:
                lane = (ROPE_DIM if m == 0 else HALF_LANES + ROPE_DIM) + (d - ROPE_DIM)
            src[lane] = m * DIFF_DH + d
    return src


def _is_map1_lane(lane):
    return (lane < ROPE_HALF) | ((lane >= ROPE_DIM) & (lane < HALF_LANES + ROPE_HALF))


ROPE_GROUPS = LANES // ROPE_HALF


def _rope_table_kernel(pos_ref, inv_ref, c_ref, sg_ref):
    ang = pos_ref[...] * inv_ref[...]
    cos = jnp.cos(ang)
    sin = jnp.sin(ang)
    rows = ang.shape[0]
    lane = lax.broadcasted_iota(jnp.int32, ang.shape, 1)
    first = lane < ROPE_DIM
    second = (lane >= HALF_LANES) & (lane < HALF_LANES + ROPE_DIM)
    freq = lane & (ROPE_HALF - 1)
    for a in range(ROPE_GROUPS):
        idx = freq + a * ROPE_HALF
        cos_a = jnp.take_along_axis(cos, idx, axis=1)
        sin_a = jnp.take_along_axis(sin, idx, axis=1)
        c_ref[a * rows:(a + 1) * rows, :] = jnp.where(first | second, cos_a, 1.0)
        sg_ref[a * rows:(a + 1) * rows, :] = jnp.where(first, -sin_a, jnp.where(second, sin_a, 0.0))


def _rope_tables(positions, tm=1024):
    T = positions.size
    rows = tm // ROPE_GROUPS
    pos = positions.astype(F32).reshape(T // tm, ROPE_GROUPS, rows).transpose(0, 2, 1)
    pos = jnp.repeat(pos, ROPE_HALF, axis=2).reshape(T // ROPE_GROUPS, LANES)
    inv_freq = ROPE_THETA ** (-jnp.arange(0, ROPE_DIM, 2, dtype=F32) / ROPE_DIM)
    inv = jnp.tile(inv_freq, ROPE_GROUPS).reshape(1, LANES)
    out = jax.ShapeDtypeStruct((T, LANES), F32)
    return pl.pallas_call(
        _rope_table_kernel,
        grid=(T // tm,),
        in_specs=[pl.BlockSpec((rows, LANES), lambda i: (i, 0)),
                  pl.BlockSpec((1, LANES), lambda i: (0, 0))],
        out_specs=[pl.BlockSpec((tm, LANES), lambda i: (i, 0))] * 2,
        out_shape=[out, out],
        name="rope_tables",
    )(pos, inv)


PROJ_SUB = 256


def _in_proj_kernel(x_ref, g_ref, w_ref, wab_ref, out_ref, ab_ref, h_ref):
    @pl.when(pl.program_id(1) == 0)
    def _():
        hb = _rms(x_ref[...], g_ref[...]).astype(BF16)
        h_ref[...] = hb
        ab_ref[...] = _dot(hb, wab_ref[...])

    hb = h_ref[...]
    for c0 in range(0, out_ref.shape[1], PROJ_SUB):
        cs = slice(c0, c0 + PROJ_SUB)
        out_ref[:, cs] = _dot(hb, w_ref[:, cs]).astype(out_ref.dtype)


def _in_proj(x2d, g_mix, w_plain, w_ab, tm=1024, tn=1792):
    T, D = x2d.shape
    N = w_plain.shape[1]
    tm = min(tm, T)
    return pl.pallas_call(
        _in_proj_kernel,
        grid=(T // tm, N // tn),
        in_specs=[pl.BlockSpec((tm, D), lambda i, j: (i, 0)),
                  pl.BlockSpec((1, D), lambda i, j: (0, 0)),
                  pl.BlockSpec((D, tn), lambda i, j: (0, j)),
                  pl.BlockSpec((D, LANES), lambda i, j: (0, 0))],
        out_specs=[pl.BlockSpec((tm, tn), lambda i, j: (i, j)),
                   pl.BlockSpec((tm, LANES), lambda i, j: (i, 0)),
                   pl.BlockSpec((tm, D), lambda i, j: (i, 0))],
        out_shape=[jax.ShapeDtypeStruct((T, N), BF16),
                   jax.ShapeDtypeStruct((T, LANES), F32),
                   jax.ShapeDtypeStruct((T, D), BF16)],
        compiler_params=pltpu.CompilerParams(
            dimension_semantics=("parallel", "arbitrary"), vmem_limit_bytes=VMEM_LIMIT),
        name="in_proj",
    )(x2d, g_mix, w_plain, w_ab)


def _rope_proj_kernel(h_ref, w_ref, c_ref, sg_ref, out_ref):
    scale = jnp.where(pl.program_id(1) == 0, DIFF_DH ** -0.5 * LOG2E, 1.0)
    c = c_ref[...] * scale
    sg = sg_ref[...] * scale
    hb = h_ref[...]
    for c0 in range(0, out_ref.shape[1], PROJ_SUB):
        acc = _dot(hb, w_ref[:, c0:c0 + PROJ_SUB])
        for l0 in range(0, PROJ_SUB, LANES):
            a = acc[:, l0:l0 + LANES]
            y = a * c + pltpu.roll(a, HALF_LANES, 1) * sg
            out_ref[:, c0 + l0:c0 + l0 + LANES] = y.astype(out_ref.dtype)


def _rope_proj(h, w_qk, tables, tm=1024):
    T, D = h.shape
    N = w_qk.shape[1]
    tn = N // 2
    tm = min(tm, T)
    return pl.pallas_call(
        _rope_proj_kernel,
        grid=(T // tm, 2),
        in_specs=[pl.BlockSpec((tm, D), lambda i, j: (i, 0)),
                  pl.BlockSpec((D, tn), lambda i, j: (0, j)),
                  pl.BlockSpec((tm, LANES), lambda i, j: (i, 0)),
                  pl.BlockSpec((tm, LANES), lambda i, j: (i, 0))],
        out_specs=pl.BlockSpec((tm, tn), lambda i, j: (i, j)),
        out_shape=jax.ShapeDtypeStruct((T, N), BF16),
        compiler_params=pltpu.CompilerParams(
            dimension_semantics=("parallel", "parallel"), vmem_limit_bytes=VMEM_LIMIT),
        name="rope_proj",
    )(h, w_qk, *tables)


GDN_TS = 4 * CHUNK
GDN_HEAD_GROUP = 4
SOLVE_BLOCK = 16
CHUNK_SHIFT = CHUNK.bit_length() - 1
SOLVE_SHIFT = SOLVE_BLOCK.bit_length() - 1
assert (1 << CHUNK_SHIFT) == CHUNK and (1 << SOLVE_SHIFT) == SOLVE_BLOCK and CHUNK // SOLVE_BLOCK == 4


def _gdn_prep_kernel(qkv_ref, ab_ref, cw_ref, alog_ref, dtb_ref,
                     u_ref, w_ref, qd_ref, kd_ref, attn_ref, egl_ref, halo_ref):
    TS, H, D = GDN_TS, GDN_HEADS, GDN_D

    @pl.when(pl.program_id(1) == 0)
    def _():
        halo_ref[...] = jnp.zeros(halo_ref.shape, F32)

    def conv_silu(c0):
        cs = slice(c0, c0 + D)
        ext = jnp.concatenate([halo_ref[:, cs], qkv_ref[:, cs].astype(F32)], axis=0)
        z = ext * cw_ref[0:1, cs]
        for t in range(1, CONV_WIDTH):
            z = pltpu.roll(z, 1, 0) + ext * cw_ref[t:t + 1, cs]
        return _silu(z[8:])

    def l2_normalised(x):
        return x * lax.rsqrt(jnp.sum(x * x, axis=-1, keepdims=True) + EPS)

    ri = lax.broadcasted_iota(jnp.int32, (TS, TS), 0)
    ci = lax.broadcasted_iota(jnp.int32, (TS, TS), 1)
    same = (ri >> CHUNK_SHIFT) == (ci >> CHUNK_SHIFT)
    incl = same & (ri >= ci)
    strict = same & (ri > ci)
    blk = (ri >> SOLVE_SHIFT) == (ci >> SOLVE_SHIFT)
    eye = jnp.where(ri == ci, 1.0, 0.0)

    ab = ab_ref[...]
    g_all = -jnp.exp(alog_ref[...]) * _softplus(ab + dtb_ref[...])
    b_all = _sigmoid(ab)
    gc_all = _dot_mask(jnp.where(incl, 1.0, 0.0).astype(BF16), g_all)
    gl_all = _dot_mask(jnp.where(same, 1.0, 0.0).astype(BF16), g_all)
    gc_t = _dot_rmask(g_all.T, jnp.where(same & (ri <= ci), 1.0, 0.0).astype(BF16))

    r8 = lax.broadcasted_iota(jnp.int32, (8, TS), 0)
    c8 = lax.broadcasted_iota(jnp.int32, (8, TS), 1)
    sel = jnp.where(c8 == r8 * CHUNK, 1.0, 0.0).astype(BF16)
    gl8 = _dot_mask(sel, gl_all)

    def head_group(hs):
        ld, lo, x, rhs = {}, {}, {}, {}
        for h in hs:
            cs = slice(h * D, (h + 1) * D)
            q = l2_normalised(conv_silu(h * D)) * (D ** -0.5)
            k = l2_normalised(conv_silu((H + h) * D))
            v = conv_silu((2 * H + h) * D)
            gc_col = gc_all[:, h:h + 1]
            gl_col = gl_all[:, h:h + 1]
            beta = b_all[:, H + h:H + h + 1]
            decay = jnp.exp(jnp.minimum(gc_col - gc_t[h:h + 1, :], 0.0))
            eg = jnp.exp(gc_col)
            kb = k * beta
            kbf = k.astype(BF16)
            lmat = jnp.where(strict, _dot_nt(kb.astype(BF16), kbf) * decay, 0.0)
            attn = jnp.where(incl, _dot_nt(q.astype(BF16), kbf) * decay, 0.0)
            attn_ref[:, h * TS:(h + 1) * TS] = attn.astype(attn_ref.dtype)
            qd_ref[:, cs] = (q * eg).astype(qd_ref.dtype)
            kd_ref[:, cs] = (k * jnp.exp(gl_col - gc_col)).astype(kd_ref.dtype)
            egl_ref[:, cs] = jnp.exp(jnp.broadcast_to(gl8[:, h:h + 1], (8, D)))
            ldh = jnp.where(blk, lmat, 0.0)
            ld[h] = ldh.astype(BF16)
            lo[h] = (lmat - ldh).astype(BF16)
            x[h] = eye - ldh
            rhs[h] = jnp.concatenate([v * beta, kb * eg], axis=1).astype(BF16)

        p = {h: _dot(ld[h], ld[h]) for h in hs}
        for _ in range(2):
            pb = {h: p[h].astype(BF16) for h in hs}
            x = {h: x[h] + _mm(x[h], pb[h]) for h in hs}
            p = {h: _dot(pb[h], pb[h]) for h in hs}
        td = {h: x[h] + _mm(x[h], p[h]) for h in hs}
        tdb = {h: td[h].astype(BF16) for h in hs}
        n = {h: _dot(tdb[h], lo[h]).astype(BF16) for h in hs}
        n2 = {h: _dot(n[h], n[h]) for h in hs}
        m1 = {h: td[h] + _mm(n2[h], tdb[h]) for h in hs}
        tinv = {h: m1[h] - _mm(n[h], m1[h]) for h in hs}
        for h in hs:
            cs = slice(h * D, (h + 1) * D)
            uw = _mm(tinv[h], rhs[h])
            u_ref[:, cs] = uw[:, :D].astype(u_ref.dtype)
            w_ref[:, cs] = uw[:, D:].astype(w_ref.dtype)

    for g0 in range(0, H, GDN_HEAD_GROUP):
        head_group(range(g0, g0 + GDN_HEAD_GROUP))
    halo_ref[...] = qkv_ref[TS - 8:TS, :].astype(F32)


def _gdn_prep(plain, ab, conv_w, a_log, dt_bias, B, S):
    T = B * S
    TS = GDN_TS
    ns = S // TS
    HD = GDN_HEADS * GDN_D
    row = lambda b, s: b * ns + s
    vec = pl.BlockSpec((1, LANES), lambda b, s: (0, 0))
    tokw = pl.BlockSpec((TS, HD), lambda b, s: (row(b, s), 0))
    act = jax.ShapeDtypeStruct((T, HD), BF16)
    return pl.pallas_call(
        _gdn_prep_kernel,
        grid=(B, ns),
        in_specs=[pl.BlockSpec((TS, 3 * HD), lambda b, s: (row(b, s), 0)),
                  pl.BlockSpec((TS, LANES), lambda b, s: (row(b, s), 0)),
                  pl.BlockSpec((CONV_WIDTH, 3 * HD), lambda b, s: (0, 0)),
                  vec, vec],
        out_specs=[tokw, tokw, tokw, tokw,
                   pl.BlockSpec((TS, GDN_HEADS * TS), lambda b, s: (row(b, s), 0)),
                   pl.BlockSpec((8, HD), lambda b, s: (row(b, s), 0))],
        out_shape=[act, act, act, act,
                   jax.ShapeDtypeStruct((T, GDN_HEADS * TS), BF16),
                   jax.ShapeDtypeStruct((B * ns * 8, HD), F32)],
        scratch_shapes=[pltpu.VMEM((8, 3 * HD), F32)],
        compiler_params=pltpu.CompilerParams(
            dimension_semantics=("parallel", "arbitrary"), vmem_limit_bytes=VMEM_LIMIT),
        name="gdn_prep",
    )(plain, ab, conv_w, a_log, dt_bias)


def _gdn_scan_kernel(u_ref, w_ref, qd_ref, kd_ref, attn_ref, egl_ref, gn_ref, o_ref,
                     state_ref, vnew_ref, oq_ref):
    TS = GDN_TS
    D = GDN_D
    nb = u_ref.shape[0]
    s = pl.program_id(1)

    @pl.when(s == 0)
    def _():
        state_ref[...] = jnp.zeros(state_ref.shape, F32)

    for c in range(TS // CHUNK):
        r = slice(c * CHUNK, (c + 1) * CHUNK)
        for b in range(nb):
            for h in range(GDN_HEADS):
                cs = slice(h * D, (h + 1) * D)
                st = state_ref[b, h]
                wq = jnp.concatenate([w_ref[b, r, cs], qd_ref[b, r, cs]], axis=0)
                ws = _dot(wq, st.astype(BF16))
                v_new = (u_ref[b, r, cs].astype(F32) - ws[:CHUNK]).astype(BF16)
                vnew_ref[b, h, r, :] = v_new
                oq_ref[b, r, cs] = ws[CHUNK:]
                ktv = lax.dot_general(kd_ref[b, r, cs], v_new, (((0,), (0,)), ((), ())),
                                      preferred_element_type=F32)
                state_ref[b, h] = st * egl_ref[b, c:c + 1, cs] + ktv

    for b in range(nb):
        for h in range(GDN_HEADS):
            cs = slice(h * D, (h + 1) * D)
            o = oq_ref[b, :, cs] + _dot(attn_ref[b, :, h * TS:(h + 1) * TS], vnew_ref[b, h])
            o_ref[b, :, cs] = _rms(o, gn_ref[...]).astype(o_ref.dtype)


def _gdn_scan(u, w, qd, kd, attn, egl, gdn_norm_g, B, S, nb=4):
    TS = GDN_TS
    ns = S // TS
    HD = GDN_HEADS * GDN_D
    nb = min(nb, B)
    assert B % nb == 0
    seq = lambda a: a.reshape(B, S, a.shape[-1])
    tokw = pl.BlockSpec((nb, TS, HD), lambda b, s: (b, s, 0))
    out = pl.pallas_call(
        _gdn_scan_kernel,
        grid=(B // nb, ns),
        in_specs=[tokw, tokw, tokw, tokw,
                  pl.BlockSpec((nb, TS, GDN_HEADS * TS), lambda b, s: (b, s, 0)),
                  pl.BlockSpec((nb, 8, HD), lambda b, s: (b, s, 0)),
                  pl.BlockSpec((1, GDN_D), lambda b, s: (0, 0))],
        out_specs=tokw,
        out_shape=jax.ShapeDtypeStruct((B, S, HD), BF16),
        scratch_shapes=[pltpu.VMEM((nb, GDN_HEADS, GDN_D, GDN_D), F32),
                        pltpu.VMEM((nb, GDN_HEADS, TS, GDN_D), BF16),
                        pltpu.VMEM((nb, TS, HD), F32)],
        compiler_params=pltpu.CompilerParams(
            dimension_semantics=("parallel", "arbitrary"), vmem_limit_bytes=VMEM_LIMIT),
        name="gdn_scan",
    )(seq(u), seq(w), seq(qd), seq(kd), seq(attn), egl.reshape(B, ns * 8, HD), gdn_norm_g)
    return out.reshape(B * S, HD)


def _diff_attn_kernel(q_ref, qn_ref, k_ref, v_ref, lq1_ref, lk1_ref, lq2_ref, lk2_ref, gn_ref, o_ref,
                      vext_ref, qz_ref, qnz_ref, sa_ref, sb_ref, sc_ref, sh_ref, m_ref, acc_ref,
                      *, tq, nq, lam_init):
    i = pl.program_id(2)
    dv = 2 * DIFF_DH
    tk = tq // 2

    def stack(src_ref, dst_ref):
        q = src_ref[...]
        map1 = _is_map1_lane(lax.broadcasted_iota(jnp.int32, q.shape, 1))
        zero = jnp.zeros_like(q)
        q1 = jnp.where(map1, q, zero)
        q2 = jnp.where(map1, zero, q)
        for half in range(2):
            rows = slice(half * tk, (half + 1) * tk)
            dst_ref[2 * half * tk:(2 * half + 1) * tk, :] = q1[rows]
            dst_ref[(2 * half + 1) * tk:(2 * half + 2) * tk, :] = q2[rows]

    def scores(j, s_ref, qsrc_ref, r0=0):
        off = pl.multiple_of(j * tk, tk)
        s_ref[...] = _dot_nt(qsrc_ref[r0:, :], k_ref[pl.ds(off, tk), :])

    def consume(j, s_ref, diag=None, r0=0, first=False):
        off = pl.multiple_of(j * tk, tk)
        sc = s_ref[...]
        if diag is not None:
            rr = lax.broadcasted_iota(jnp.int32, sc.shape, 0)
            cc = lax.broadcasted_iota(jnp.int32, sc.shape, 1)
            keep = (rr & (tk - 1)) >= cc
            if diag == "D1":
                keep = keep | (rr >= tq)
            sc = jnp.where(keep, sc, NEG_BIG)
        rows = slice(r0, 2 * tq)
        row_max = jnp.max(sc, axis=-1, keepdims=True)
        if first:
            m_new = jnp.broadcast_to(row_max, (sc.shape[0], LANES))
        else:
            m_prev = m_ref[rows, :]
            m_new = jnp.maximum(m_prev, row_max)
        p = jnp.concatenate([jnp.exp2(sc[:, c0:c0 + LANES] - m_new) for c0 in range(0, tk, LANES)],
                            axis=1)
        pv = _dot(p.astype(BF16), vext_ref[pl.ds(off, tk), :])
        if first:
            acc_ref[rows, :] = pv
        else:
            alpha = jnp.exp2(m_prev - m_new)
            for c0 in range(0, 2 * dv, LANES):
                acc_ref[rows, c0:c0 + LANES] = (alpha * acc_ref[rows, c0:c0 + LANES]
                                                + pv[:, c0:c0 + LANES])
        m_ref[rows, :] = m_new

    def next_d1():
        nxt = jnp.minimum(i + 1, nq - 1)
        stack(qn_ref, qnz_ref)
        scores(2 * nxt, sc_ref, qnz_ref)

    stack(q_ref, qz_ref)

    @pl.when(i == 0)
    def _():
        vext_ref[:, :dv] = v_ref[...]
        vext_ref[:, dv:] = jnp.ones((vext_ref.shape[0], dv), BF16)
        scores(0, sc_ref, qz_ref)

    scores(2 * i + 1, sh_ref, qz_ref, r0=tq)
    consume(2 * i, sc_ref, diag="D1", first=True)

    @pl.when(i == 0)
    def _():
        next_d1()
        consume(1, sh_ref, diag="D2", r0=tq)

    @pl.when(i > 0)
    def _():
        scores(0, sa_ref, qz_ref)
        consume(2 * i + 1, sh_ref, diag="D2", r0=tq)

        def pair(p, carry):
            j = 2 * p
            scores(j + 1, sb_ref, qz_ref)
            consume(j, sa_ref)
            scores(j + 2, sa_ref, qz_ref)
            consume(j + 1, sb_ref)
            return carry

        lax.fori_loop(0, i - 1, pair, 0)
        scores(2 * i - 1, sb_ref, qz_ref)
        consume(2 * i - 2, sa_ref)
        next_d1()
        consume(2 * i - 1, sb_ref)

    lam = (jnp.exp(jnp.sum(lq1_ref[...] * lk1_ref[...], axis=-1, keepdims=True))
           - jnp.exp(jnp.sum(lq2_ref[...] * lk2_ref[...], axis=-1, keepdims=True))
           + lam_init)
    for half in range(2):
        a1 = acc_ref[2 * half * tk:(2 * half + 1) * tk, :]
        a2 = acc_ref[(2 * half + 1) * tk:(2 * half + 2) * tk, :]
        o = a1[:, :dv] / a1[:, dv:] - lam * (a2[:, :dv] / a2[:, dv:])
        o_ref[half * tk:(half + 1) * tk, :] = o.astype(o_ref.dtype)


def _diff_attn(rope, plain, lq1, lk1, lq2, lk2, diff_norm_g, lam_init, B, S, tq=1024):
    T = B * S
    nq = S // tq
    dv = 2 * DIFF_DH
    lam_spec = pl.BlockSpec((1, DIFF_DH), lambda b, h, i: (0, 0))
    score_buf = pltpu.VMEM((2 * tq, tq // 2), F32)
    stacked_q = pltpu.VMEM((2 * tq, LANES), BF16)
    return pl.pallas_call(
        functools.partial(_diff_attn_kernel, tq=tq, nq=nq, lam_init=lam_init),
        grid=(B, DIFF_HEADS, nq),
        in_specs=[pl.BlockSpec((tq, LANES), lambda b, h, i: (b * nq + i, R_Q + h)),
                  pl.BlockSpec((tq, LANES),
                               lambda b, h, i: (b * nq + jnp.minimum(i + 1, nq - 1), R_Q + h)),
                  pl.BlockSpec((S, LANES), lambda b, h, i: (b, R_K + h)),
                  pl.BlockSpec((S, LANES), lambda b, h, i: (b, P_VB + h)),
                  lam_spec, lam_spec, lam_spec, lam_spec,
                  pl.BlockSpec((1, dv), lambda b, h, i: (0, 0))],
        out_specs=pl.BlockSpec((tq, LANES), lambda b, h, i: (b * nq + i, h)),
        out_shape=jax.ShapeDtypeStruct((T, DIFF_HEADS * dv), BF16),
        scratch_shapes=[pltpu.VMEM((S, 2 * dv), BF16),
                        stacked_q, stacked_q,
                        score_buf, score_buf, score_buf,
                        pltpu.VMEM((tq, tq // 2), F32),
                        pltpu.VMEM((2 * tq, LANES), F32),
                        pltpu.VMEM((2 * tq, 2 * dv), F32)],
        compiler_params=pltpu.CompilerParams(
            dimension_semantics=("parallel", "parallel", "arbitrary"),
            vmem_limit_bytes=VMEM_LIMIT),
        name="diff_attn",
    )(rope, rope, rope, plain, lq1, lk1, lq2, lk2, diff_norm_g)


def _merge_kernel(oa_ref, z_ref, ob_ref, ga_ref, gb_ref, x_ref, woa_ref, wob_ref, wo_ref, gnb_ref, x1_ref,
                  *, ob_scale):
    oa = (oa_ref[...].astype(F32) * _silu(z_ref[...].astype(F32))).astype(BF16)
    ya = _dot(oa, woa_ref[...])
    ob = ob_ref[...].astype(F32)
    dv = gnb_ref.shape[1]
    ob = jnp.concatenate([_rms(ob[:, c0:c0 + dv], gnb_ref[...]) for c0 in range(0, ob.shape[1], dv)],
                         axis=1) * ob_scale
    yb = _dot(ob.astype(BF16), wob_ref[...])
    merged = _sigmoid(ga_ref[...].astype(F32)) * ya + _sigmoid(gb_ref[...].astype(F32)) * yb
    x1_ref[...] = x_ref[...] + _dot(merged.astype(BF16), wo_ref[...])


def _merge(o_a, o_b, plain, x2d, w_out_a, w_out_b, w_o, diff_norm_g, ob_scale, tm=512):
    T, D = x2d.shape
    tokd = pl.BlockSpec((tm, D), lambda i: (i, 0))
    wspec = pl.BlockSpec((D, D), lambda i: (0, 0))
    return pl.pallas_call(
        functools.partial(_merge_kernel, ob_scale=ob_scale),
        grid=(T // tm,),
        in_specs=[tokd,
                  pl.BlockSpec((tm, D), lambda i: (i, P_Z * LANES // D)),
                  tokd,
                  pl.BlockSpec((tm, D), lambda i: (i, P_GA * LANES // D)),
                  pl.BlockSpec((tm, D), lambda i: (i, P_GB * LANES // D)),
                  tokd, wspec, wspec, wspec,
                  pl.BlockSpec(diff_norm_g.shape, lambda i: (0, 0))],
        out_specs=tokd,
        out_shape=jax.ShapeDtypeStruct((T, D), F32),
        compiler_params=pltpu.CompilerParams(
            dimension_semantics=("parallel",), vmem_limit_bytes=VMEM_LIMIT),
        name="merge",
    )(o_a, plain, o_b, plain, plain, x2d, w_out_a, w_out_b, w_o, diff_norm_g)


def _mem_kv_kernel(m_ref, g_ref, w_ref, o_ref):
    o_ref[...] = _dot(_rms(m_ref[...], g_ref[...]).astype(BF16), w_ref[...]).astype(o_ref.dtype)


def _mem_kv(mem2d, g_mem, w_ckv):
    R, D = mem2d.shape
    N = w_ckv.shape[1]
    return pl.pallas_call(
        _mem_kv_kernel,
        grid=(1,),
        in_specs=[pl.BlockSpec((R, D), lambda i: (0, 0)),
                  pl.BlockSpec((1, D), lambda i: (0, 0)),
                  pl.BlockSpec((D, N), lambda i: (0, 0))],
        out_specs=pl.BlockSpec((R, N), lambda i: (0, 0)),
        out_shape=jax.ShapeDtypeStruct((R, N), BF16),
        compiler_params=pltpu.CompilerParams(vmem_limit_bytes=VMEM_LIMIT),
        name="mem_kv",
    )(mem2d, g_mem, w_ckv)


def _cross_kernel(x_ref, g_ref, wq_ref, kv_ref, wo_ref, o_ref):
    x = x_ref[...]
    hx = _rms(x, g_ref[...]).astype(BF16)
    qc = (_dot(hx, wq_ref[...]) * (X_DH ** -0.5)).astype(BF16)
    xw = X_HEADS * X_DH
    outs = []
    ones = jnp.ones((kv_ref.shape[0], X_DH), BF16)
    for hh in range(X_HEADS):
        cs = slice(hh * X_DH, (hh + 1) * X_DH)
        sc = _dot_nt(qc[:, cs], kv_ref[:, cs])
        m = jnp.broadcast_to(jnp.max(sc, axis=-1, keepdims=True), (sc.shape[0], LANES))
        p = jnp.concatenate([jnp.exp(sc[:, c0:c0 + LANES] - m) for c0 in range(0, sc.shape[1], LANES)],
                            axis=1)
        vext = jnp.concatenate([kv_ref[:, xw + hh * X_DH:xw + (hh + 1) * X_DH], ones], axis=1)
        oe = _dot(p.astype(BF16), vext)
        outs.append(oe[:, :X_DH] / oe[:, X_DH:])
    oc = jnp.concatenate(outs, axis=1).astype(BF16)
    o_ref[...] = x + _dot(oc, wo_ref[...])


def _cross(x1, g_cross, w_cq, kv, w_co, S, M, tm=1024):
    T, D = x1.shape
    xw = X_HEADS * X_DH
    per_b = S // tm
    tokd = pl.BlockSpec((tm, D), lambda i: (i, 0))
    return pl.pallas_call(
        _cross_kernel,
        grid=(T // tm,),
        in_specs=[tokd,
                  pl.BlockSpec((1, D), lambda i: (0, 0)),
                  pl.BlockSpec((D, xw), lambda i: (0, 0)),
                  pl.BlockSpec((M, 2 * xw), lambda i: (i // per_b, 0)),
                  pl.BlockSpec((xw, D), lambda i: (0, 0))],
        out_specs=tokd,
        out_shape=jax.ShapeDtypeStruct((T, D), F32),
        compiler_params=pltpu.CompilerParams(
            dimension_semantics=("parallel",), vmem_limit_bytes=VMEM_LIMIT),
        name="cross",
    )(x1, g_cross, w_cq, kv, w_co)


def _ffn_kernel(x_ref, g_ref, wi_ref, wo_ref, gf_ref, o_ref, *, tf):
    x = x_ref[...]
    hb = _rms(x, g_ref[...]).astype(BF16)
    F = wo_ref.shape[0]
    acc = x
    for f in range(F // tf):
        act = _silu(_dot(hb, wi_ref[:, f * tf:(f + 1) * tf])) * _dot(hb, wi_ref[:, F + f * tf:F + (f + 1) * tf])
        acc = acc + _dot(act.astype(BF16), wo_ref[f * tf:(f + 1) * tf, :])
    o_ref[...] = _rms(acc, gf_ref[...])


def _ffn(x2, g_ffn, w_ffn_in, w_ffn_out, g_final, tm=1024, tf=256):
    T, D = x2.shape
    F = w_ffn_out.shape[0]
    tokd = pl.BlockSpec((tm, D), lambda i: (i, 0))
    vec = pl.BlockSpec((1, D), lambda i: (0, 0))
    resident = pl.Buffered(1)
    return pl.pallas_call(
        functools.partial(_ffn_kernel, tf=tf),
        grid=(T // tm,),
        in_specs=[tokd, vec,
                  pl.BlockSpec((D, 2 * F), lambda i: (0, 0), pipeline_mode=resident),
                  pl.BlockSpec((F, D), lambda i: (0, 0), pipeline_mode=resident),
                  vec],
        out_specs=tokd,
        out_shape=jax.ShapeDtypeStruct((T, D), F32),
        compiler_params=pltpu.CompilerParams(
            dimension_semantics=("parallel",), vmem_limit_bytes=VMEM_LIMIT),
        name="ffn",
    )(x2, g_ffn, w_ffn_in, w_ffn_out, g_final)


def _pad_lanes(v):
    v = v.reshape(1, -1).astype(F32)
    return jnp.pad(v, ((0, 0), (0, LANES - v.shape[1])))


def kernel(x, mem, positions, g_mix, w_in, conv_w, a_log, dt_bias, gdn_norm_g, lambda_q1, lambda_k1, lambda_q2, lambda_k2, diff_norm_g, w_branch_gate, w_out_a, w_out_b, w_o, g_cross, g_mem, w_cq, w_ckv, w_co, g_ffn, w_ffn_in, w_ffn_out, g_final):
    B, S, D = x.shape
    M = mem.shape[1]
    depth = w_in.shape[0]
    assert depth == 1, "the final rmsnorm is fused into the (single) layer's ffn call"
    qkvz = 4 * GDN_HEADS * GDN_D
    xs = x.reshape(B * S, D)
    tables = _rope_tables(positions)
    kv = None
    for l in range(depth):
        ab_end = qkvz + 2 * GDN_HEADS
        qk_end = ab_end + 2 * DIFF_HEADS * 2 * DIFF_DH
        w16 = w_in[l].astype(BF16)
        w_plain = jnp.concatenate([w16[:, :qkvz], w16[:, qk_end:], w_branch_gate[l].astype(BF16)], axis=1)
        w_ab = jnp.pad(w16[:, qkvz:ab_end], ((0, 0), (0, LANES - 2 * GDN_HEADS)))
        plain, ab, h = _in_proj(xs, g_mix[l].reshape(1, D), w_plain, w_ab)
        src = jnp.asarray([hh * LANES + c for hh in range(2 * DIFF_HEADS) for c in _head_lane_source()],
                          dtype=jnp.int32)
        w_qk = jnp.take(w16[:, ab_end:qk_end], src, axis=1)
        rope = _rope_proj(h, w_qk, tables)

        u, w, qd, kd, attn, egl = _gdn_prep(plain, ab, conv_w[l], _pad_lanes(a_log[l]),
                                            _pad_lanes(dt_bias[l]), B, S)
        o_a = _gdn_scan(u, w, qd, kd, attn, egl, gdn_norm_g[l].reshape(1, GDN_D), B, S)

        lam_init = 0.8 - 0.6 * math.exp(-0.3 * l)
        o_b = _diff_attn(rope, plain, lambda_q1[l].reshape(1, -1), lambda_k1[l].reshape(1, -1),
                         lambda_q2[l].reshape(1, -1), lambda_k2[l].reshape(1, -1),
                         diff_norm_g[l].reshape(1, -1), lam_init, B, S)

        xs = _merge(o_a, o_b, plain, xs,w_out_a[l].astype(BF16), w_out_b[l].astype(BF16),
                    w_o[l].astype(BF16), diff_norm_g[l].reshape(1, -1), 1.0 - lam_init)

        kv = _mem_kv(mem.reshape(B * M, D), g_mem[l].reshape(1, D), w_ckv[l].astype(BF16))
        xs = _cross(xs, g_cross[l].reshape(1, D), w_cq[l].astype(BF16), kv, w_co[l].astype(BF16), S, M)

        xs = _ffn(xs, g_ffn[l].reshape(1, D), w_ffn_in[l].astype(BF16), w_ffn_out[l].astype(BF16),
                  g_final.reshape(1, D))
    return xs.reshape(B, S, D)
```
